```python
import math
import jax
import jax.numpy as jnp
from jax import lax
import numpy as np

D_MODEL = 2048
BATCH = 2
SEQ = 4096
DEPTH = 4

GRID_W = 64
CTX_LEN = 256

HEAD_W = 128
HY_C = 512
HY_ORDER = 2
HY_SHORT = 3
HY_BANDS = 16
HY_EMB = 1 + 2 * HY_BANDS
HY_FFN = 64
HY_FAST_DECAY = 0.3
HY_SLOW_DECAY = 1.5
HY_DECAY_TARGET = 1e-2

LRU_C = 512
LRU_BLOCKS = 4
LRU_BW = LRU_C // LRU_BLOCKS
LRU_CONV = 4
RG_C = 8.0

MLA_H = 8
Q_RANK = 512
KV_RANK = 256
NOPE_D = 128
ROPE_D = 64
V_D = 128
ROPE_BASE = 10000.0
Q_BLOCK = 128
MLA_SCALE = (NOPE_D + ROPE_D) ** -0.5

D_MIX = HY_C + LRU_C + MLA_H * V_D
N_MIX_HEADS = D_MIX // HEAD_W
D_FF = 5632
FFN_CONV = 3
N_MOD = 6
EPS = 1e-6

OFF_HY = 0
OFF_LRU_G = OFF_HY + (HY_ORDER + 1) * HY_C
OFF_MLA_Q = OFF_LRU_G + LRU_C
OFF_LRU_X = OFF_MLA_Q + Q_RANK
OFF_MLA_KV = OFF_LRU_X + LRU_C
OFF_MLA_KR = OFF_MLA_KV + KV_RANK
D_IN = OFF_MLA_KR + ROPE_D

kernel_name = 'hymba_style_hyena_rglru_mla_dit'

F32 = jnp.float32


def rmsnorm(x, g):
    xf = x.astype(F32)
    y = xf * lax.rsqrt(jnp.mean(xf * xf, axis=-1, keepdims=True) + EPS)
    return (y * g.astype(F32)).astype(x.dtype)


def modulate(h, shift, scale):
    return h * (1 + scale) + shift


def dwconv(x, w, left):
    k = w.shape[0]
    length = x.shape[1]
    xp = jnp.pad(x, ((0, 0), (left, k - 1 - left), (0, 0)))
    w = w.astype(x.dtype)
    out = xp[:, 0:length] * w[0]
    for j in range(1, k):
        out = out + xp[:, j:j + length] * w[j]
    return out


def hyena_filter_spectrum(length, w1, b1, w2, b2, w3):
    t = jnp.arange(length, dtype=F32) / length
    ang = 2.0 * math.pi * t[:, None] * jnp.arange(1, HY_BANDS + 1, dtype=F32)
    feats = jnp.concatenate([t[:, None], jnp.sin(ang), jnp.cos(ang)], axis=-1)
    hid = jnp.sin(feats @ w1.astype(F32) + b1.astype(F32))
    hid = jnp.sin(hid @ w2.astype(F32) + b2.astype(F32))
    h = (hid @ w3.astype(F32)).reshape(length, HY_ORDER, 2, HY_C)
    deltas = jnp.abs(jnp.linspace(math.log(HY_DECAY_TARGET) / HY_SLOW_DECAY,
                                  math.log(HY_DECAY_TARGET) / HY_FAST_DECAY, HY_C, dtype=F32))
    h = h * jnp.exp(-t[:, None] * deltas)[:, None, None, :]
    fwd, bwd = h[:, :, 0], h[:, :, 1]
    k = jnp.concatenate([fwd, jnp.zeros((1, HY_ORDER, HY_C), F32), bwd[:0:-1]], axis=0)
    k = k / jnp.sum(jnp.abs(k), axis=0, keepdims=True)
    return jnp.fft.rfft(k, axis=0)


def long_conv(z, spec):
    length = z.shape[1]
    zf = jnp.fft.rfft(z, n=2 * length, axis=1)
    return jnp.fft.irfft(zf * spec, n=2 * length, axis=1)[:, :length]


def hyena_mix(u, conv_w, w1, b1, w2, b2, w3, bias):
    length = u.shape[1]
    parts = jnp.split(dwconv(u, conv_w, HY_SHORT // 2).astype(F32), HY_ORDER + 1, axis=-1)
    spec = hyena_filter_spectrum(length, w1, b1, w2, b2, w3)
    z = parts[0]
    for o in range(HY_ORDER):
        z = parts[o + 1] * (long_conv(z, spec[:, o]) + z * bias[o].astype(F32))
    return z.astype(u.dtype)


def _lru_combine(left, right):
    a1, b1 = left
    a2, b2 = right
    return a1 * a2, a2 * b1 + b2


def rglru_scan(xr, conv_w, wa, ba, wx, bx, lam, h0, reverse):
    bsz, length, _ = xr.shape
    xc = dwconv(xr, conv_w, 0 if reverse else LRU_CONV - 1)
    xb = xc.reshape(bsz, length, LRU_BLOCKS, LRU_BW)
    r = jax.nn.sigmoid(jnp.einsum('blnd,nde->blne', xb, wa).reshape(bsz, length, LRU_C).astype(F32) + ba.astype(F32))
    i = jax.nn.sigmoid(jnp.einsum('blnd,nde->blne', xb, wx).reshape(bsz, length, LRU_C).astype(F32) + bx.astype(F32))
    log_a = -RG_C * r * jax.nn.softplus(-lam.astype(F32))
    a = jnp.exp(log_a)
    b = jnp.sqrt(-jnp.expm1(2.0 * log_a)) * (i * xc.astype(F32))
    if h0 is not None:
        first = length - 1 if reverse else 0
        b = b.at[:, first].add(a[:, first] * h0)
    _, h = lax.associative_scan(_lru_combine, (a, b), reverse=reverse, axis=1)
    return h


def lru_output(gate_pre, h):
    return (jax.nn.gelu(gate_pre.astype(F32), approximate=True) * h).astype(gate_pre.dtype)


def axial_rope_tables(length):
    rows = length // GRID_W
    row = jnp.repeat(jnp.arange(rows, dtype=F32), GRID_W)
    col = jnp.tile(jnp.arange(GRID_W, dtype=F32), rows)
    n_freq = ROPE_D // 4
    inv = ROPE_BASE ** (-jnp.arange(n_freq, dtype=F32) / n_freq)
    ang = jnp.concatenate([row[:, None] * inv, col[:, None] * inv], axis=-1)
    return jnp.cos(ang), jnp.sin(ang)


def apply_rope(x, cos, sin):
    half = x.shape[-1] // 2
    x1 = x[..., :half].astype(F32)
    x2 = x[..., half:].astype(F32)
    return jnp.concatenate([x1 * cos - x2 * sin, x2 * cos + x1 * sin], axis=-1).astype(x.dtype)


def mla_query(uq, gq, wuq, rope):
    bsz, length, _ = uq.shape
    q = (rmsnorm(uq, gq) @ wuq).reshape(bsz, length, MLA_H, NOPE_D + ROPE_D)
    if rope is not None:
        cos, sin = rope
        q = jnp.concatenate([q[..., :NOPE_D], apply_rope(q[..., NOPE_D:], cos[:, None], sin[:, None])], axis=-1)
    return q


def mla_keys_values(ukv, ukr, gkv, wukv, rope):
    bsz, length, _ = ukv.shape
    kv = (rmsnorm(ukv, gkv) @ wukv).reshape(bsz, length, MLA_H, NOPE_D + V_D)
    k_nope, v = kv[..., :NOPE_D], kv[..., NOPE_D:]
    k_rope = ukr if rope is None else apply_rope(ukr, rope[0], rope[1])
    k = jnp.concatenate([k_nope, jnp.broadcast_to(k_rope[:, :, None], (bsz, length, MLA_H, ROPE_D))], axis=-1)
    return k, v


def attend(q, k, v):
    s = jnp.einsum('bqhd,bkhd->bhqk', q, k, preferred_element_type=F32) * MLA_SCALE
    p = jax.nn.softmax(s, axis=-1).astype(v.dtype)
    return jnp.einsum('bhqk,bkhd->bqhd', p, v)


def latent_attention(q, k_lat, v_lat, k_ctx, v_ctx):
    bsz, length = q.shape[:2]
    k = jnp.concatenate([k_ctx, k_lat], axis=1)
    v = jnp.concatenate([v_ctx, v_lat], axis=1)
    nb = length // Q_BLOCK
    qb = q.reshape(bsz, nb, Q_BLOCK, MLA_H, NOPE_D + ROPE_D).swapaxes(0, 1)
    o = lax.map(lambda qi: attend(qi, k, v), qb)
    return o.swapaxes(0, 1).reshape(bsz, length, MLA_H * V_D)


def mix_out(y_hy, y_lru, y_mla, head_g, w_out):
    y = jnp.concatenate([y_hy, y_lru, y_mla], axis=-1)
    bsz, length, _ = y.shape
    y = rmsnorm(y.reshape(bsz, length, N_MIX_HEADS, HEAD_W), head_g.reshape(N_MIX_HEADS, HEAD_W))
    return y.reshape(bsz, length, D_MIX) @ w_out


def conv_ffn(h, w_up, w_conv, w_down):
    up = dwconv(h @ w_up, w_conv, FFN_CONV // 2)
    gate, val = jnp.split(up, 2, axis=-1)
    return (jax.nn.gelu(gate, approximate=True) * val) @ w_down


def setup_inputs(seed: int = 0) -> dict:
    key = jax.random.key(seed)
    ks = iter(jax.random.split(key, 40))

    def nrm(shape, scale):
        return scale * jax.random.normal(next(ks), shape, F32)

    def gain(shape):
        return 1.0 + 0.05 * jax.random.normal(next(ks), shape, F32)

    a0 = jax.random.uniform(next(ks), (DEPTH, 2, LRU_C), F32, 0.9, 0.999) ** (1.0 / RG_C)
    lru_lam = jnp.log(a0) - jnp.log1p(-a0)
    return {
        'x': nrm((BATCH, SEQ, D_MODEL), 1.0),
        'c': nrm((BATCH, D_MODEL), 1.0),
        'ctx': nrm((BATCH, CTX_LEN, D_MODEL), 1.0),
        'c_ctx': nrm((D_MODEL,), 1.0),
        'ada_w': nrm((DEPTH, D_MODEL, N_MOD * D_MODEL), 0.5 * D_MODEL ** -0.5),
        'ada_b': nrm((DEPTH, N_MOD * D_MODEL), 0.02),
        'norm_g': gain((DEPTH, 4, D_MODEL)),
        'w_in': nrm((DEPTH, D_MODEL, D_IN), D_MODEL ** -0.5),
        'hy_conv': nrm((DEPTH, HY_SHORT, (HY_ORDER + 1) * HY_C), HY_SHORT ** -0.5),
        'hy_w1': nrm((DEPTH, HY_EMB, HY_FFN), HY_EMB ** -0.5),
        'hy_b1': nrm((DEPTH, HY_FFN), 0.1),
        'hy_w2': nrm((DEPTH, HY_FFN, HY_FFN), HY_FFN ** -0.5),
        'hy_b2': nrm((DEPTH, HY_FFN), 0.1),
        'hy_w3': nrm((DEPTH, HY_FFN, HY_ORDER * 2 * HY_C), HY_FFN ** -0.5),
        'hy_bias': nrm((DEPTH, HY_ORDER, HY_C), 0.5),
        'lru_conv': nrm((DEPTH, 2, LRU_CONV, LRU_C), LRU_CONV ** -0.5),
        'lru_wa': nrm((DEPTH, 2, LRU_BLOCKS, LRU_BW, LRU_BW), LRU_BW ** -0.5),
        'lru_ba': nrm((DEPTH, 2, LRU_C), 0.02),
        'lru_wx': nrm((DEPTH, 2, LRU_BLOCKS, LRU_BW, LRU_BW), LRU_BW ** -0.5),
        'lru_bx': nrm((DEPTH, 2, LRU_C), 0.02),
        'lru_lam': lru_lam,
        'mla_gq': gain((DEPTH, Q_RANK)),
        'mla_gkv': gain((DEPTH, KV_RANK)),
        'mla_wuq': nrm((DEPTH, Q_RANK, MLA_H * (NOPE_D + ROPE_D)), Q_RANK ** -0.5),
        'mla_wukv': nrm((DEPTH, KV_RANK, MLA_H * (NOPE_D + V_D)), KV_RANK ** -0.5),
        'head_g': gain((DEPTH, D_MIX)),
        'w_out': nrm((DEPTH, D_MIX, D_MODEL), D_MIX ** -0.5),
        'ffn_up': nrm((DEPTH, D_MODEL, 2 * D_FF), D_MODEL ** -0.5),
        'ffn_conv': nrm((DEPTH, FFN_CONV, 2 * D_FF), FFN_CONV ** -0.5),
        'ffn_down': nrm((DEPTH, D_FF, D_MODEL), D_FF ** -0.5),
    }


def reference(x, c, ctx, c_ctx, ada_w, ada_b, norm_g, w_in, hy_conv, hy_w1, hy_b1, hy_w2, hy_b2,
              hy_w3, hy_bias, lru_conv, lru_wa, lru_ba, lru_wx, lru_bx, lru_lam, mla_gq, mla_gkv,
              mla_wuq, mla_wukv, head_g, w_out, ffn_up, ffn_conv, ffn_down):
    bsz, seq, _ = x.shape
    ctx_len = ctx.shape[1]
    rope = axial_rope_tables(seq)
    xc = ctx
    for l in range(DEPTH):
        need_ctx = l < DEPTH - 1
        ng = norm_g[l]
        mod = jax.nn.silu(c) @ ada_w[l] + ada_b[l]
        sh1, sc1, g1, sh2, sc2, g2 = jnp.split(mod[:, None, :], N_MOD, axis=-1)
        mod_c = jax.nn.silu(c_ctx) @ ada_w[l] + ada_b[l]
        csh1, csc1, cg1, csh2, csc2, cg2 = jnp.split(mod_c, N_MOD, axis=-1)
        hy_p = (hy_conv[l], hy_w1[l], hy_b1[l], hy_w2[l], hy_b2[l], hy_w3[l], hy_bias[l])
        lru_f = (lru_conv[l, 0], lru_wa[l, 0], lru_ba[l, 0], lru_wx[l, 0], lru_bx[l, 0], lru_lam[l, 0])
        lru_b = (lru_conv[l, 1], lru_wa[l, 1], lru_ba[l, 1], lru_wx[l, 1], lru_bx[l, 1], lru_lam[l, 1])

        h = modulate(rmsnorm(x, ng[0]), sh1, sc1)
        hc = modulate(rmsnorm(xc, ng[0]), csh1, csc1)
        u = h @ w_in[l]
        uc_s = hc @ w_in[l, :, OFF_LRU_X:]

        xr_c = uc_s[..., :LRU_C]
        hc_f = rglru_scan(xr_c, *lru_f, None, False)
        hc_b = rglru_scan(xr_c, *lru_b, None, True)
        k_c, v_c = mla_keys_values(uc_s[..., OFF_MLA_KV - OFF_LRU_X:OFF_MLA_KR - OFF_LRU_X],
                                   uc_s[..., OFF_MLA_KR - OFF_LRU_X:], mla_gkv[l], mla_wukv[l], None)

        y_hy = hyena_mix(u[..., OFF_HY:OFF_LRU_G], *hy_p)
        xr = u[..., OFF_LRU_X:OFF_MLA_KV]
        h_f = rglru_scan(xr, *lru_f, hc_f[:, -1], False)
        h_b = rglru_scan(xr, *lru_b, hc_b[:, 0], True)
        y_lru = lru_output(u[..., OFF_LRU_G:OFF_MLA_Q], h_f + h_b)
        q = mla_query(u[..., OFF_MLA_Q:OFF_LRU_X], mla_gq[l], mla_wuq[l], rope)
        k, v = mla_keys_values(u[..., OFF_MLA_KV:OFF_MLA_KR], u[..., OFF_MLA_KR:], mla_gkv[l], mla_wukv[l], rope)
        y_mla = latent_attention(q, k, v, k_c, v_c)
        x = x + g1 * rmsnorm(mix_out(y_hy, y_lru, y_mla, head_g[l], w_out[l]), ng[1])

        if need_ctx:
            uc_o = hc @ w_in[l, :, :OFF_LRU_X]
            yc_hy = hyena_mix(uc_o[..., OFF_HY:OFF_LRU_G], *hy_p)
            yc_lru = lru_output(uc_o[..., OFF_LRU_G:OFF_MLA_Q], hc_f + hc_b)
            q_c = mla_query(uc_o[..., OFF_MLA_Q:], mla_gq[l], mla_wuq[l], None)
            yc_mla = attend(q_c, k_c, v_c).reshape(bsz, ctx_len, MLA_H * V_D)
            xc = xc + cg1 * rmsnorm(mix_out(yc_hy, yc_lru, yc_mla, head_g[l], w_out[l]), ng[1])

        h2 = modulate(rmsnorm(x, ng[2]), sh2, sc2)
        x = x + g2 * rmsnorm(conv_ffn(h2, ffn_up[l], ffn_conv[l], ffn_down[l]), ng[3])
        if need_ctx:
            hc2 = modulate(rmsnorm(xc, ng[2]), csh2, csc2)
            xc = xc + cg2 * rmsnorm(conv_ffn(hc2, ffn_up[l], ffn_conv[l], ffn_down[l]), ng[3])
    return x
```

```python
import functools
import math

import jax
import jax.numpy as jnp
from jax import lax
from jax.experimental import pallas as pl
from jax.experimental.pallas import tpu as pltpu

F32 = jnp.float32
BF16 = jnp.bfloat16

D_MODEL = 2048
GRID_W = 64
HEAD_W = 128
HY_C = 512
HY_ORDER = 2
HY_SHORT = 3
HY_BANDS = 16
HY_FAST_DECAY = 0.3
HY_SLOW_DECAY = 1.5
HY_DECAY_TARGET = 1e-2
LRU_C = 512
LRU_BLOCKS = 4
LRU_BW = LRU_C // LRU_BLOCKS
LRU_CONV = 4
RG_C = 8.0
MLA_H = 8
Q_RANK = 512
KV_RANK = 256
NOPE_D = 128
ROPE_D = 64
V_D = 128
ROPE_BASE = 10000.0
MLA_SCALE = (NOPE_D + ROPE_D) ** -0.5
D_MIX = HY_C + LRU_C + MLA_H * V_D
N_MIX_HEADS = D_MIX // HEAD_W
D_FF = 5632
FFN_CONV = 3
N_MOD = 6
EPS = 1e-6

OFF_HY = 0
OFF_LRU_G = OFF_HY + (HY_ORDER + 1) * HY_C
OFF_MLA_Q = OFF_LRU_G + LRU_C
OFF_LRU_X = OFF_MLA_Q + Q_RANK
OFF_MLA_KV = OFF_LRU_X + LRU_C
OFF_MLA_KR = OFF_MLA_KV + KV_RANK
D_IN = OFF_MLA_KR + ROPE_D

HEAD_QK = 2 * NOPE_D
VMEM_LIMIT = 48 * 1024 * 1024


def _pick(n, prefs):
    for p in prefs:
        if n % p == 0:
            return p
    return n


def _mm_kernel(a_ref, b_ref, o_ref):
    a = a_ref[...].astype(BF16)
    b = b_ref[...].astype(BF16)
    o_ref[...] = jnp.dot(a, b, preferred_element_type=F32).astype(o_ref.dtype)


def _mm(a, b, out_dtype=F32):
    m, k = a.shape
    _, n = b.shape
    mp = -(-m // 8) * 8
    if mp != m:
        a = jnp.pad(a, ((0, mp - m), (0, 0)))
    assert n % 128 == 0, n
    tn = max(t for t in range(128, 1153, 128) if n % t == 0)
    tm = 8
    for t in (1024, 512, 256, 128, 64, 32, 16, 8):
        need = 2 * (t * k * a.dtype.itemsize + k * tn * b.dtype.itemsize + t * tn * 4)
        if mp % t == 0 and need <= VMEM_LIMIT - (8 << 20):
            tm = t
            break
    out = pl.pallas_call(
        _mm_kernel,
        grid=(mp // tm, n // tn),
        in_specs=[pl.BlockSpec((tm, k), lambda i, j: (i, 0)),
                  pl.BlockSpec((k, tn), lambda i, j: (0, j))],
        out_specs=pl.BlockSpec((tm, tn), lambda i, j: (i, j)),
        out_shape=jax.ShapeDtypeStruct((mp, n), out_dtype),
        compiler_params=pltpu.CompilerParams(
            dimension_semantics=("parallel", "parallel"), vmem_limit_bytes=VMEM_LIMIT),
        name="mm",
    )(a, b)
    return out[:m] if mp != m else out


def _attn_kernel(*refs, n_seg):
    q_ref = refs[0]
    seg_refs = refs[1:1 + 3 * n_seg]
    o_ref = refs[1 + 3 * n_seg]
    k_scr, v_scr = refs[2 + 3 * n_seg:]

    @pl.when(pl.program_id(2) == 0)
    def _():
        off = 0
        for s in range(n_seg):
            kn_ref, v_ref, kr_ref = seg_refs[3 * s:3 * s + 3]
            n = kn_ref.shape[1]
            k_scr[off:off + n, 0:NOPE_D] = kn_ref[0]
            k_scr[off:off + n, NOPE_D:HEAD_QK] = kr_ref[0]
            v_scr[off:off + n, :] = v_ref[0]
            off += n

    q = q_ref[0]
    s = lax.dot_general(q, k_scr[...], (((1,), (1,)), ((), ())), preferred_element_type=F32)
    m = jnp.max(s, axis=-1, keepdims=True)
    p = jnp.exp(s - m)
    l = jnp.sum(p, axis=-1, keepdims=True)
    o = jnp.dot(p.astype(BF16), v_scr[...], preferred_element_type=F32)
    o_ref[0] = o / l


def _attention(q, segs):
    bsz, lq, _ = q.shape
    tq = _pick(lq, (256, 128))
    lk = sum(kv.shape[1] for kv, _ in segs)
    in_specs = [pl.BlockSpec((1, tq, HEAD_QK), lambda b, h, i: (b, i, h))]
    args = [q]
    for kv, kr in segs:
        n = kv.shape[1]
        in_specs += [pl.BlockSpec((1, n, NOPE_D), lambda b, h, i: (b, 0, 2 * h)),
                     pl.BlockSpec((1, n, V_D), lambda b, h, i: (b, 0, 2 * h + 1)),
                     pl.BlockSpec((1, n, NOPE_D), lambda b, h, i: (b, 0, 0))]
        args += [kv, kv, kr]
    return pl.pallas_call(
        functools.partial(_attn_kernel, n_seg=len(segs)),
        grid=(bsz, MLA_H, lq // tq),
        in_specs=in_specs,
        out_specs=pl.BlockSpec((1, tq, V_D), lambda b, h, i: (b, i, h)),
        out_shape=jax.ShapeDtypeStruct((bsz, lq, MLA_H * V_D), F32),
        scratch_shapes=[pltpu.VMEM((lk, HEAD_QK), BF16), pltpu.VMEM((lk, V_D), BF16)],
        compiler_params=pltpu.CompilerParams(
            dimension_semantics=("parallel", "parallel", "arbitrary"), vmem_limit_bytes=VMEM_LIMIT),
        name="attn",
    )(*args)


def _rmsnorm(x, g):
    xf = x.astype(F32)
    y = xf * lax.rsqrt(jnp.mean(xf * xf, axis=-1, keepdims=True) + EPS)
    return y * g.astype(F32)


def _modulate(h, shift, scale):
    return h * (1 + scale) + shift


def _dwconv(x, w, left):
    k = w.shape[0]
    length = x.shape[1]
    xp = jnp.pad(x, ((0, 0), (left, k - 1 - left), (0, 0)))
    out = xp[:, 0:length] * w[0]
    for j in range(1, k):
        out = out + xp[:, j:j + length] * w[j]
    return out


def _lin(x, w, out_dtype=F32):
    shp = x.shape
    return _mm(x.reshape(-1, shp[-1]), w, out_dtype).reshape(shp[:-1] + (w.shape[-1],))


def _hyena_filter_spectrum(length, w1, b1, w2, b2, w3):
    hp = lax.Precision.HIGHEST
    t = jnp.arange(length, dtype=F32) / length
    ang = 2.0 * math.pi * t[:, None] * jnp.arange(1, HY_BANDS + 1, dtype=F32)
    feats = jnp.concatenate([t[:, None], jnp.sin(ang), jnp.cos(ang)], axis=-1)
    hid = jnp.sin(jnp.dot(feats, w1, precision=hp) + b1)
    hid = jnp.sin(jnp.dot(hid, w2, precision=hp) + b2)
    h = jnp.dot(hid, w3, precision=hp).reshape(length, HY_ORDER, 2, HY_C)
    deltas = jnp.abs(jnp.linspace(math.log(HY_DECAY_TARGET) / HY_SLOW_DECAY,
                                  math.log(HY_DECAY_TARGET) / HY_FAST_DECAY, HY_C, dtype=F32))
    h = h * jnp.exp(-t[:, None] * deltas)[:, None, None, :]
    fwd, bwd = h[:, :, 0], h[:, :, 1]
    k = jnp.concatenate([fwd, jnp.zeros((1, HY_ORDER, HY_C), F32), bwd[:0:-1]], axis=0)
    k = k / jnp.sum(jnp.abs(k), axis=0, keepdims=True)
    return jnp.fft.rfft(k, axis=0)


def _long_conv(z, spec):
    length = z.shape[1]
    zf = jnp.fft.rfft(z, n=2 * length, axis=1)
    return jnp.fft.irfft(zf * spec, n=2 * length, axis=1)[:, :length]


def _hyena_mix(u, conv_w, w1, b1, w2, b2, w3, bias):
    length = u.shape[1]
    parts = jnp.split(_dwconv(u, conv_w, HY_SHORT // 2), HY_ORDER + 1, axis=-1)
    spec = _hyena_filter_spectrum(length, w1, b1, w2, b2, w3)
    z = parts[0]
    for o in range(HY_ORDER):
        z = parts[o + 1] * (_long_conv(z, spec[:, o]) + z * bias[o])
    return z


def _lru_combine(left, right):
    a1, b1 = left
    a2, b2 = right
    return a1 * a2, a2 * b1 + b2


def _rglru_scan(xr, conv_w, wa, ba, wx, bx, lam, h0, reverse):
    bsz, length, _ = xr.shape
    xc = _dwconv(xr, conv_w, 0 if reverse else LRU_CONV - 1)
    xb = xc.reshape(bsz, length, LRU_BLOCKS, LRU_BW).astype(BF16)
    ga = jnp.einsum('blnd,nde->blne', xb, wa.astype(BF16), preferred_element_type=F32)
    gx = jnp.einsum('blnd,nde->blne', xb, wx.astype(BF16), preferred_element_type=F32)
    r = jax.nn.sigmoid(ga.reshape(bsz, length, LRU_C) + ba)
    i = jax.nn.sigmoid(gx.reshape(bsz, length, LRU_C) + bx)
    log_a = -RG_C * r * jax.nn.softplus(-lam)
    a = jnp.exp(log_a)
    b = jnp.sqrt(-jnp.expm1(2.0 * log_a)) * (i * xc)
    if h0 is not None:
        first = length - 1 if reverse else 0
        b = b.at[:, first].add(a[:, first] * h0)
    _, h = lax.associative_scan(_lru_combine, (a, b), reverse=reverse, axis=1)
    return h


def _rope_tables(length):
    rows = length // GRID_W
    row = jnp.repeat(jnp.arange(rows, dtype=F32), GRID_W)
    col = jnp.tile(jnp.arange(GRID_W, dtype=F32), rows)
    n_freq = ROPE_D // 4
    inv = ROPE_BASE ** (-jnp.arange(n_freq, dtype=F32) / n_freq)
    ang = jnp.concatenate([row[:, None] * inv, col[:, None] * inv], axis=-1)
    return jnp.cos(ang), jnp.sin(ang)


def _apply_rope(x, cos, sin):
    half = x.shape[-1] // 2
    x1 = x[..., :half]
    x2 = x[..., half:]
    return jnp.concatenate([x1 * cos - x2 * sin, x2 * cos + x1 * sin], axis=-1)


def _pad_heads_q(wuq):
    w = wuq.reshape(Q_RANK, MLA_H, NOPE_D + ROPE_D)
    w = jnp.pad(w, ((0, 0), (0, 0), (0, HEAD_QK - NOPE_D - ROPE_D)))
    return w.reshape(Q_RANK, MLA_H * HEAD_QK)


def _mla_query(uq, gq, wuq_pad, rope):
    bsz, length, _ = uq.shape
    q = _lin(_rmsnorm(uq, gq), wuq_pad).reshape(bsz, length, MLA_H, HEAD_QK)
    qn, qr = q[..., :NOPE_D], q[..., NOPE_D:NOPE_D + ROPE_D]
    if rope is not None:
        qr = _apply_rope(qr, rope[0][:, None], rope[1][:, None])
    q = jnp.concatenate([qn, qr, jnp.zeros_like(qr)], axis=-1) * MLA_SCALE
    return q.reshape(bsz, length, MLA_H * HEAD_QK).astype(BF16)


def _mla_keys_values(ukv, ukr, gkv, wukv, rope):
    kv = _lin(_rmsnorm(ukv, gkv), wukv, BF16)
    kr = ukr if rope is None else _apply_rope(ukr, rope[0], rope[1])
    kr = jnp.concatenate([kr, jnp.zeros_like(kr)], axis=-1).astype(BF16)
    return kv, kr


def _mix_out(y_hy, y_lru, y_mla, head_g, w_out):
    y = jnp.concatenate([y_hy, y_lru, y_mla], axis=-1)
    bsz, length, _ = y.shape
    y = _rmsnorm(y.reshape(bsz, length, N_MIX_HEADS, HEAD_W), head_g.reshape(N_MIX_HEADS, HEAD_W))
    return _lin(y.reshape(bsz, length, D_MIX), w_out)


def _conv_ffn(h, w_up, w_conv, w_down):
    up = _dwconv(_lin(h, w_up), w_conv, FFN_CONV // 2)
    gate, val = jnp.split(up, 2, axis=-1)
    return _lin(jax.nn.gelu(gate, approximate=True) * val, w_down)


def kernel(x, c, ctx, c_ctx, ada_w, ada_b, norm_g, w_in, hy_conv, hy_w1, hy_b1, hy_w2, hy_b2, hy_w3, hy_bias, lru_conv, lru_wa, lru_ba, lru_wx, lru_bx, lru_lam, mla_gq, mla_gkv, mla_wuq, mla_wukv, head_g, w_out, ffn_up, ffn_conv, ffn_down):
    depth = ada_w.shape[0]
    bsz, seq, _ = x.shape
    ctx_len = ctx.shape[1]
    rope = _rope_tables(seq)
    xc = ctx
    for l in range(depth):
        need_ctx = l < depth - 1
        ng = norm_g[l]
        cc = jnp.concatenate([jax.nn.silu(c), jax.nn.silu(c_ctx)[None]], axis=0)
        mod_all = _mm(cc, ada_w[l]) + ada_b[l]
        sh1, sc1, g1, sh2, sc2, g2 = jnp.split(mod_all[:bsz, None, :], N_MOD, axis=-1)
        csh1, csc1, cg1, csh2, csc2, cg2 = jnp.split(mod_all[bsz], N_MOD, axis=-1)
        hy_p = (hy_conv[l], hy_w1[l], hy_b1[l], hy_w2[l], hy_b2[l], hy_w3[l], hy_bias[l])
        lru_f = (lru_conv[l, 0], lru_wa[l, 0], lru_ba[l, 0], lru_wx[l, 0], lru_bx[l, 0], lru_lam[l, 0])
        lru_b = (lru_conv[l, 1], lru_wa[l, 1], lru_ba[l, 1], lru_wx[l, 1], lru_bx[l, 1], lru_lam[l, 1])
        w_in_b = jnp.pad(w_in[l], ((0, 0), (0, NOPE_D - ROPE_D))).astype(BF16)
        wuq_pad = _pad_heads_q(mla_wuq[l]).astype(BF16)
        wukv_b = mla_wukv[l].astype(BF16)
        w_out_b = w_out[l].astype(BF16)
        up_b = ffn_up[l].astype(BF16)
        down_b = ffn_down[l].astype(BF16)

        h = _modulate(_rmsnorm(x, ng[0]), sh1, sc1).astype(BF16)
        hc = _modulate(_rmsnorm(xc, ng[0]), csh1, csc1).astype(BF16)
        u = _lin(h, w_in_b)
        uc_s = _lin(hc, w_in_b[:, OFF_LRU_X:])

        xr_c = uc_s[..., :LRU_C]
        hc_f = _rglru_scan(xr_c, *lru_f, None, False)
        hc_b = _rglru_scan(xr_c, *lru_b, None, True)
        kv_c, kr_c = _mla_keys_values(uc_s[..., OFF_MLA_KV - OFF_LRU_X:OFF_MLA_KR - OFF_LRU_X],
                                      uc_s[..., OFF_MLA_KR - OFF_LRU_X:D_IN - OFF_LRU_X], mla_gkv[l], wukv_b, None)

        y_hy = _hyena_mix(u[..., OFF_HY:OFF_LRU_G], *hy_p)
        xr = u[..., OFF_LRU_X:OFF_MLA_KV]
        h_f = _rglru_scan(xr, *lru_f, hc_f[:, -1], False)
        h_b = _rglru_scan(xr, *lru_b, hc_b[:, 0], True)
        y_lru = jax.nn.gelu(u[..., OFF_LRU_G:OFF_MLA_Q], approximate=True) * (h_f + h_b)
        q = _mla_query(u[..., OFF_MLA_Q:OFF_LRU_X], mla_gq[l], wuq_pad, rope)
        kv, kr = _mla_keys_values(u[..., OFF_MLA_KV:OFF_MLA_KR], u[..., OFF_MLA_KR:D_IN], mla_gkv[l], wukv_b, rope)
        y_mla = _attention(q, [(kv_c, kr_c), (kv, kr)])
        x = x + g1 * _rmsnorm(_mix_out(y_hy, y_lru, y_mla, head_g[l], w_out_b), ng[1])

        if need_ctx:
            uc_o = _lin(hc, w_in_b[:, :OFF_LRU_X])
            yc_hy = _hyena_mix(uc_o[..., OFF_HY:OFF_LRU_G], *hy_p)
            yc_lru = jax.nn.gelu(uc_o[..., OFF_LRU_G:OFF_MLA_Q], approximate=True) * (hc_f + hc_b)
            q_c = _mla_query(uc_o[..., OFF_MLA_Q:], mla_gq[l], wuq_pad, None)
            yc_mla = _attention(q_c, [(kv_c, kr_c)])
            xc = xc + cg1 * _rmsnorm(_mix_out(yc_hy, yc_lru, yc_mla, head_g[l], w_out_b), ng[1])

        h2 = _modulate(_rmsnorm(x, ng[2]), sh2, sc2).astype(BF16)
        x = x + g2 * _rmsnorm(_conv_ffn(h2, up_b, ffn_conv[l], down_b), ng[3])
        if need_ctx:
            hc2 = _modulate(_rmsnorm(xc, ng[2]), csh2, csc2).astype(BF16)
            xc = xc + cg2 * _rmsnorm(_conv_ffn(hc2, up_b, ffn_conv[l], down_b), ng[3])
    return x
```

```python
import functools
import math

import jax
import jax.numpy as jnp
from jax import lax
from jax.experimental import pallas as pl
from jax.experimental.pallas import tpu as pltpu

F32 = jnp.float32
BF16 = jnp.bfloat16

D_MODEL = 2048
GRID_W = 64
HEAD_W = 128
HY_C = 512
HY_ORDER = 2
HY_SHORT = 3
HY_BANDS = 16
HY_FAST_DECAY = 0.3
HY_SLOW_DECAY = 1.5
HY_DECAY_TARGET = 1e-2
LRU_C = 512
LRU_BLOCKS = 4
LRU_BW = LRU_C // LRU_BLOCKS
LRU_CONV = 4
RG_C = 8.0
MLA_H = 8
Q_RANK = 512
KV_RANK = 256
NOPE_D = 128
ROPE_D = 64
V_D = 128
ROPE_BASE = 10000.0
MLA_SCALE = (NOPE_D + ROPE_D) ** -0.5
D_MIX = HY_C + LRU_C + MLA_H * V_D
N_MIX_HEADS = D_MIX // HEAD_W
D_FF = 5632
FFN_CONV = 3
N_MOD = 6
EPS = 1e-6

OFF_HY = 0
OFF_LRU_G = OFF_HY + (HY_ORDER + 1) * HY_C
OFF_MLA_Q = OFF_LRU_G + LRU_C
OFF_LRU_X = OFF_MLA_Q + Q_RANK
OFF_MLA_KV = OFF_LRU_X + LRU_C
OFF_MLA_KR = OFF_MLA_KV + KV_RANK
D_IN = OFF_MLA_KR + ROPE_D

HEAD_QK = 2 * NOPE_D
VMEM_LIMIT = 48 * 1024 * 1024


def _pick(n, prefs):
    for p in prefs:
        if n % p == 0:
            return p
    return n


def _mm_kernel(a_ref, b_ref, o_ref):
    a = a_ref[...].astype(BF16)
    b = b_ref[...].astype(BF16)
    o_ref[...] = jnp.dot(a, b, preferred_element_type=F32).astype(o_ref.dtype)


def _mm(a, b, out_dtype=F32):
    m, k = a.shape
    _, n = b.shape
    mp = -(-m // 8) * 8
    if mp != m:
        a = jnp.pad(a, ((0, mp - m), (0, 0)))
    assert n % 128 == 0, n
    tn = max(t for t in range(128, 1153, 128) if n % t == 0)
    tm = 8
    for t in (1024, 512, 256, 128, 64, 32, 16, 8):
        need = 2 * (t * k * a.dtype.itemsize + k * tn * b.dtype.itemsize + t * tn * 4)
        if mp % t == 0 and need <= VMEM_LIMIT - (8 << 20):
            tm = t
            break
    out = pl.pallas_call(
        _mm_kernel,
        grid=(mp // tm, n // tn),
        in_specs=[pl.BlockSpec((tm, k), lambda i, j: (i, 0)),
                  pl.BlockSpec((k, tn), lambda i, j: (0, j))],
        out_specs=pl.BlockSpec((tm, tn), lambda i, j: (i, j)),
        out_shape=jax.ShapeDtypeStruct((mp, n), out_dtype),
        compiler_params=pltpu.CompilerParams(
            dimension_semantics=("parallel", "parallel"), vmem_limit_bytes=VMEM_LIMIT),
        name="mm",
    )(a, b)
    return out[:m] if mp != m else out


def _attn_kernel(*refs, n_seg):
    q_ref = refs[0]
    seg_refs = refs[1:1 + 3 * n_seg]
    o_ref = refs[1 + 3 * n_seg]
    k_scr, v_scr = refs[2 + 3 * n_seg:]

    @pl.when(pl.program_id(2) == 0)
    def _():
        off = 0
        for s in range(n_seg):
            kn_ref, v_ref, kr_ref = seg_refs[3 * s:3 * s + 3]
            n = kn_ref.shape[1]
            k_scr[off:off + n, 0:NOPE_D] = kn_ref[0]
            k_scr[off:off + n, NOPE_D:HEAD_QK] = kr_ref[0]
            v_scr[off:off + n, :] = v_ref[0]
            off += n

    q = q_ref[0]
    s = lax.dot_general(q, k_scr[...], (((1,), (1,)), ((), ())), preferred_element_type=F32)
    m = jnp.max(s, axis=-1, keepdims=True)
    p = jnp.exp(s - m)
    l = jnp.sum(p, axis=-1, keepdims=True)
    o = jnp.dot(p.astype(BF16), v_scr[...], preferred_element_type=F32)
    o_ref[0] = o / l


def _attention(q, segs):
    bsz, lq, _ = q.shape
    tq = _pick(lq, (256, 128))
    lk = sum(kv.shape[1] for kv, _ in segs)
    in_specs = [pl.BlockSpec((1, tq, HEAD_QK), lambda b, h, i: (b, i, h))]
    args = [q]
    for kv, kr in segs:
        n = kv.shape[1]
        in_specs += [pl.BlockSpec((1, n, NOPE_D), lambda b, h, i: (b, 0, 2 * h)),
                     pl.BlockSpec((1, n, V_D), lambda b, h, i: (b, 0, 2 * h + 1)),
                     pl.BlockSpec((1, n, NOPE_D), lambda b, h, i: (b, 0, 0))]
        args += [kv, kv, kr]
    return pl.pallas_call(
        functools.partial(_attn_kernel, n_seg=len(segs)),
        grid=(bsz, MLA_H, lq // tq),
        in_specs=in_specs,
        out_specs=pl.BlockSpec((1, tq, V_D), lambda b, h, i: (b, i, h)),
        out_shape=jax.ShapeDtypeStruct((bsz, lq, MLA_H * V_D), F32),
        scratch_shapes=[pltpu.VMEM((lk, HEAD_QK), BF16), pltpu.VMEM((lk, V_D), BF16)],
        compiler_params=pltpu.CompilerParams(
            dimension_semantics=("parallel", "parallel", "arbitrary"), vmem_limit_bytes=VMEM_LIMIT),
        name="attn",
    )(*args)


def _rmsnorm(x, g):
    xf = x.astype(F32)
    y = xf * lax.rsqrt(jnp.mean(xf * xf, axis=-1, keepdims=True) + EPS)
    return y * g.astype(F32)


def _modulate(h, shift, scale):
    return h * (1 + scale) + shift


def _dwconv(x, w, left):
    k = w.shape[0]
    length = x.shape[1]
    xp = jnp.pad(x, ((0, 0), (left, k - 1 - left), (0, 0)))
    out = xp[:, 0:length] * w[0]
    for j in range(1, k):
        out = out + xp[:, j:j + length] * w[j]
    return out


def _lin(x, w, out_dtype=F32):
    shp = x.shape
    return _mm(x.reshape(-1, shp[-1]), w, out_dtype).reshape(shp[:-1] + (w.shape[-1],))


def _hyena_filter_spectrum(length, w1, b1, w2, b2, w3):
    hp = lax.Precision.HIGHEST
    t = jnp.arange(length, dtype=F32) / length
    ang = 2.0 * math.pi * t[:, None] * jnp.arange(1, HY_BANDS + 1, dtype=F32)
    feats = jnp.concatenate([t[:, None], jnp.sin(ang), jnp.cos(ang)], axis=-1)
    hid = jnp.sin(jnp.dot(feats, w1, precision=hp) + b1)
    hid = jnp.sin(jnp.dot(hid, w2, precision=hp) + b2)
    h = jnp.dot(hid, w3, precision=hp).reshape(length, HY_ORDER, 2, HY_C)
    deltas = jnp.abs(jnp.linspace(math.log(HY_DECAY_TARGET) / HY_SLOW_DECAY,
                                  math.log(HY_DECAY_TARGET) / HY_FAST_DECAY, HY_C, dtype=F32))
    h = h * jnp.exp(-t[:, None] * deltas)[:, None, None, :]
    fwd, bwd = h[:, :, 0], h[:, :, 1]
    k = jnp.concatenate([fwd, jnp.zeros((1, HY_ORDER, HY_C), F32), bwd[:0:-1]], axis=0)
    k = k / jnp.sum(jnp.abs(k), axis=0, keepdims=True)
    return jnp.fft.rfft(k, axis=0)


def _long_conv(z, spec):
    length = z.shape[1]
    zf = jnp.fft.rfft(z, n=2 * length, axis=1)
    return jnp.fft.irfft(zf * spec, n=2 * length, axis=1)[:, :length]


def _hyena_mix(u, conv_w, w1, b1, w2, b2, w3, bias):
    length = u.shape[1]
    parts = jnp.split(_dwconv(u, conv_w, HY_SHORT // 2), HY_ORDER + 1, axis=-1)
    spec = _hyena_filter_spectrum(length, w1, b1, w2, b2, w3)
    z = parts[0]
    for o in range(HY_ORDER):
        z = parts[o + 1] * (_long_conv(z, spec[:, o]) + z * bias[o])
    return z


SUBLANES = 8
PAD = LRU_CONV - 1


def _lru_kernel(x_ref, wg_ref, pv_ref, h0_ref, o_ref, end_ref, xp, a_scr, b_scr, *, steps, chunk):
    sub = lax.broadcasted_iota(jnp.int32, (SUBLANES, LRU_BW), 0)
    xp[PAD:PAD + steps] = x_ref[0]
    for k in range(PAD):
        tail = pltpu.roll(x_ref[0, steps - PAD + k], 1, axis=0)
        xp[k] = jnp.where(sub == 0, 0.0, tail)
        head = pltpu.roll(x_ref[0, k], SUBLANES - 1, axis=0)
        xp[PAD + steps + k] = jnp.where(sub == SUBLANES - 1, 0.0, head)

    n_chunks = steps // chunk
    for d in range(2):
        ba = pv_ref[d, 0:1, :]
        bx = pv_ref[d, 1:2, :]
        lam = pv_ref[d, 2:3, :]
        neg_c_sp = -RG_C * jax.nn.softplus(-lam)
        taps = [pv_ref[d, 3 + k:4 + k, :].reshape(1, 1, LRU_BW) for k in range(LRU_CONV)]
        w_gate = wg_ref[d, 0]
        base = 0 if d == 0 else PAD

        def gates(c, carry):
            j0 = pl.multiple_of(c * chunk, chunk)
            xc = taps[0] * xp[pl.ds(j0 + base, chunk)]
            for k in range(1, LRU_CONV):
                xc = xc + taps[k] * xp[pl.ds(j0 + base + k, chunk)]
            xc2 = xc.reshape(chunk * SUBLANES, LRU_BW)
            g = jnp.dot(xc2.astype(BF16), w_gate, preferred_element_type=F32)
            r = jax.nn.sigmoid(g[:, :LRU_BW] + ba)
            i = jax.nn.sigmoid(g[:, LRU_BW:] + bx)
            log_a = neg_c_sp * r
            a = jnp.exp(log_a)
            t = jnp.tanh(log_a)
            b = jnp.sqrt(-2.0 * t / (1.0 - t)) * (i * xc2)
            a_scr[pl.ds(j0, chunk)] = a.reshape(chunk, SUBLANES, LRU_BW)
            b_scr[pl.ds(j0, chunk)] = b.reshape(chunk, SUBLANES, LRU_BW)
            return carry

        lax.fori_loop(0, n_chunks, gates, 0)

        def step(t, carry):
            h, p = carry
            j = t if d == 0 else steps - 1 - t
            a = a_scr[j]
            h = a * h + b_scr[j]
            p = a * p
            b_scr[j] = h
            a_scr[j] = p
            return h, p

        h_loc, p_loc = lax.fori_loop(
            0, steps, step,
            (jnp.zeros((SUBLANES, LRU_BW), F32), jnp.ones((SUBLANES, LRU_BW), F32)), unroll=8)

        carry = h0_ref[0, d:d + 1, :]
        cin = jnp.zeros((SUBLANES, LRU_BW), F32)
        order = range(SUBLANES) if d == 0 else range(SUBLANES - 1, -1, -1)
        for s in order:
            cin = jnp.where(sub == s, jnp.broadcast_to(carry, (SUBLANES, LRU_BW)), cin)
            carry = p_loc[s:s + 1, :] * carry + h_loc[s:s + 1, :]
        end_ref[0, d:d + 1, :] = carry

        def fix(c, carry_):
            j0 = pl.multiple_of(c * chunk, chunk)
            h = b_scr[pl.ds(j0, chunk)] + a_scr[pl.ds(j0, chunk)] * cin[None]
            if d == 0:
                o_ref[0, pl.ds(j0, chunk)] = h
            else:
                o_ref[0, pl.ds(j0, chunk)] += h
            return carry_

        lax.fori_loop(0, n_chunks, fix, 0)


def _rglru(xr, lconv, lwa, lba, lwx, lbx, llam, h0):
    bsz, length, _ = xr.shape
    steps = length // SUBLANES
    chunk = min(32, steps)
    xs = xr.reshape(bsz, SUBLANES, steps, LRU_C).transpose(0, 2, 1, 3)
    wg = jnp.concatenate([lwa, lwx], axis=-1).astype(BF16)
    pv = jnp.concatenate([lba[:, None], lbx[:, None], llam[:, None], lconv,
                          jnp.zeros((2, 1, LRU_C), F32)], axis=1)
    o, ends = pl.pallas_call(
        functools.partial(_lru_kernel, steps=steps, chunk=chunk),
        grid=(bsz, LRU_BLOCKS),
        in_specs=[pl.BlockSpec((1, steps, SUBLANES, LRU_BW), lambda b, n: (b, 0, 0, n)),
                  pl.BlockSpec((2, 1, LRU_BW, 2 * LRU_BW), lambda b, n: (0, n, 0, 0)),
                  pl.BlockSpec((2, SUBLANES, LRU_BW), lambda b, n: (0, 0, n)),
                  pl.BlockSpec((1, 2, LRU_BW), lambda b, n: (b, 0, n))],
        out_specs=[pl.BlockSpec((1, steps, SUBLANES, LRU_BW), lambda b, n: (b, 0, 0, n)),
                   pl.BlockSpec((1, 2, LRU_BW), lambda b, n: (b, 0, n))],
        out_shape=[jax.ShapeDtypeStruct((bsz, steps, SUBLANES, LRU_C), F32),
                   jax.ShapeDtypeStruct((bsz, 2, LRU_C), F32)],
        scratch_shapes=[pltpu.VMEM((steps + 2 * PAD, SUBLANES, LRU_BW), F32),
                        pltpu.VMEM((steps, SUBLANES, LRU_BW), F32),
                        pltpu.VMEM((steps, SUBLANES, LRU_BW), F32)],
        compiler_params=pltpu.CompilerParams(
            dimension_semantics=("parallel", "parallel"), vmem_limit_bytes=VMEM_LIMIT),
        name="rglru",
    )(xs, wg, pv, h0)
    return o.transpose(0, 2, 1, 3).reshape(bsz, length, LRU_C), ends


def _rope_tables(length):
    rows = length // GRID_W
    row = jnp.repeat(jnp.arange(rows, dtype=F32), GRID_W)
    col = jnp.tile(jnp.arange(GRID_W, dtype=F32), rows)
    n_freq = ROPE_D // 4
    inv = ROPE_BASE ** (-jnp.arange(n_freq, dtype=F32) / n_freq)
    ang = jnp.concatenate([row[:, None] * inv, col[:, None] * inv], axis=-1)
    return jnp.cos(ang), jnp.sin(ang)


def _apply_rope(x, cos, sin):
    half = x.shape[-1] // 2
    x1 = x[..., :half]
    x2 = x[..., half:]
    return jnp.concatenate([x1 * cos - x2 * sin, x2 * cos + x1 * sin], axis=-1)


def _pad_heads_q(wuq):
    w = wuq.reshape(Q_RANK, MLA_H, NOPE_D + ROPE_D)
    w = jnp.pad(w, ((0, 0), (0, 0), (0, HEAD_QK - NOPE_D - ROPE_D)))
    return w.reshape(Q_RANK, MLA_H * HEAD_QK)


def _mla_query(uq, gq, wuq_pad, rope):
    bsz, length, _ = uq.shape
    q = _lin(_rmsnorm(uq, gq), wuq_pad).reshape(bsz, length, MLA_H, HEAD_QK)
    qn, qr = q[..., :NOPE_D], q[..., NOPE_D:NOPE_D + ROPE_D]
    if rope is not None:
        qr = _apply_rope(qr, rope[0][:, None], rope[1][:, None])
    q = jnp.concatenate([qn, qr, jnp.zeros_like(qr)], axis=-1) * MLA_SCALE
    return q.reshape(bsz, length, MLA_H * HEAD_QK).astype(BF16)


def _mla_keys_values(ukv, ukr, gkv, wukv, rope):
    kv = _lin(_rmsnorm(ukv, gkv), wukv, BF16)
    kr = ukr if rope is None else _apply_rope(ukr, rope[0], rope[1])
    kr = jnp.concatenate([kr, jnp.zeros_like(kr)], axis=-1).astype(BF16)
    return kv, kr


def _mix_out(y_hy, y_lru, y_mla, head_g, w_out):
    y = jnp.concatenate([y_hy, y_lru, y_mla], axis=-1)
    bsz, length, _ = y.shape
    y = _rmsnorm(y.reshape(bsz, length, N_MIX_HEADS, HEAD_W), head_g.reshape(N_MIX_HEADS, HEAD_W))
    return _lin(y.reshape(bsz, length, D_MIX), w_out)


HALO = 16


def _ffn_kernel(x_ref, xprev_ref, xnext_ref, ng_ref, mod_ref, wg_ref, wv_ref, cg_ref, cv_ref, wd_ref,
                o_ref, hbuf, acc, *, blocks_per_seq):
    i = pl.program_id(0)
    j = pl.program_id(1)
    tm = x_ref.shape[0]

    def norm_mod(xv):
        y = xv * lax.rsqrt(jnp.mean(xv * xv, axis=-1, keepdims=True) + EPS) * ng_ref[0:1, :]
        return (y * (1.0 + mod_ref[0, 1:2, :]) + mod_ref[0, 0:1, :]).astype(BF16)

    @pl.when(j == 0)
    def _():
        first = (i % blocks_per_seq) == 0
        last = (i % blocks_per_seq) == blocks_per_seq - 1
        hp = norm_mod(xprev_ref[...])
        hn = norm_mod(xnext_ref[...])
        hbuf[0:HALO] = jnp.where(first, jnp.zeros_like(hp), hp)
        hbuf[HALO:HALO + tm] = norm_mod(x_ref[...])
        hbuf[HALO + tm:] = jnp.where(last, jnp.zeros_like(hn), hn)
        acc[...] = jnp.zeros_like(acc)

    hb = hbuf[...]

    def conv(up, c_ref):
        prev = pltpu.roll(up, 1, axis=0)[HALO:HALO + tm]
        nxt = pltpu.roll(up, tm + 2 * HALO - 1, axis=0)[HALO:HALO + tm]
        return c_ref[0:1, :] * prev + c_ref[1:2, :] * up[HALO:HALO + tm] + c_ref[2:3, :] * nxt

    gate = conv(jnp.dot(hb, wg_ref[...], preferred_element_type=F32), cg_ref)
    val = conv(jnp.dot(hb, wv_ref[...], preferred_element_type=F32), cv_ref)
    act = (jax.nn.gelu(gate, approximate=True) * val).astype(BF16)
    acc[...] += jnp.dot(act, wd_ref[...], preferred_element_type=F32)

    @pl.when(j == pl.num_programs(1) - 1)
    def _():
        y = acc[...]
        y = y * lax.rsqrt(jnp.mean(y * y, axis=-1, keepdims=True) + EPS) * ng_ref[1:2, :]
        o_ref[...] = x_ref[...] + mod_ref[0, 2:3, :] * y


def _ffn_sublayer(x, ng_pre, ng_post, shift, scale, gate, w_up, w_conv, w_down):
    bsz, length, d = x.shape
    m = bsz * length
    tm = _pick(length, (512, 256, 128))
    tf = _pick(D_FF, (512, 256, 128))
    nf = D_FF // tf
    bps = length // tm
    hb = tm // HALO
    nrow = m // HALO
    per_batch = shift.shape[0] > 1
    ng = jnp.stack([ng_pre, ng_post])
    mod = jnp.stack([shift, scale, gate], axis=1)
    out = pl.pallas_call(
        functools.partial(_ffn_kernel, blocks_per_seq=bps),
        grid=(m // tm, nf),
        in_specs=[pl.BlockSpec((tm, d), lambda i, j: (i, 0)),
                  pl.BlockSpec((HALO, d), lambda i, j: (jnp.maximum(i * hb - 1, 0), 0)),
                  pl.BlockSpec((HALO, d), lambda i, j: (jnp.minimum((i + 1) * hb, nrow - 1), 0)),
                  pl.BlockSpec((2, d), lambda i, j: (0, 0)),
                  pl.BlockSpec((1, 3, d), (lambda i, j: (i // bps, 0, 0)) if per_batch else (lambda i, j: (0, 0, 0))),
                  pl.BlockSpec((d, tf), lambda i, j: (0, j)),
                  pl.BlockSpec((d, tf), lambda i, j: (0, j + nf)),
                  pl.BlockSpec((FFN_CONV, tf), lambda i, j: (0, j)),
                  pl.BlockSpec((FFN_CONV, tf), lambda i, j: (0, j + nf)),
                  pl.BlockSpec((tf, d), lambda i, j: (j, 0))],
        out_specs=pl.BlockSpec((tm, d), lambda i, j: (i, 0)),
        out_shape=jax.ShapeDtypeStruct((m, d), F32),
        scratch_shapes=[pltpu.VMEM((tm + 2 * HALO, d), BF16), pltpu.VMEM((tm, d), F32)],
        compiler_params=pltpu.CompilerParams(
            dimension_semantics=("parallel", "arbitrary"), vmem_limit_bytes=VMEM_LIMIT),
        name="ffn",
    )(x.reshape(m, d), x.reshape(m, d), x.reshape(m, d), ng, mod, w_up, w_up, w_conv, w_conv, w_down)
    return out.reshape(bsz, length, d)


def kernel(x, c, ctx, c_ctx, ada_w, ada_b, norm_g, w_in, hy_conv, hy_w1, hy_b1, hy_w2, hy_b2, hy_w3, hy_bias, lru_conv, lru_wa, lru_ba, lru_wx, lru_bx, lru_lam, mla_gq, mla_gkv, mla_wuq, mla_wukv, head_g, w_out, ffn_up, ffn_conv, ffn_down):
    depth = ada_w.shape[0]
    bsz, seq, _ = x.shape
    ctx_len = ctx.shape[1]
    rope = _rope_tables(seq)
    xc = ctx
    for l in range(depth):
        need_ctx = l < depth - 1
        ng = norm_g[l]
        cc = jnp.concatenate([jax.nn.silu(c), jax.nn.silu(c_ctx)[None]], axis=0)
        mod_all = _mm(cc, ada_w[l]) + ada_b[l]
        sh1, sc1, g1, sh2, sc2, g2 = jnp.split(mod_all[:bsz, None, :], N_MOD, axis=-1)
        csh1, csc1, cg1, csh2, csc2, cg2 = jnp.split(mod_all[bsz], N_MOD, axis=-1)
        hy_p = (hy_conv[l], hy_w1[l], hy_b1[l], hy_w2[l], hy_b2[l], hy_w3[l], hy_bias[l])
        lru_p = (lru_conv[l], lru_wa[l], lru_ba[l], lru_wx[l], lru_bx[l], lru_lam[l])
        w_in_b = jnp.pad(w_in[l], ((0, 0), (0, NOPE_D - ROPE_D))).astype(BF16)
        wuq_pad = _pad_heads_q(mla_wuq[l]).astype(BF16)
        wukv_b = mla_wukv[l].astype(BF16)
        w_out_b = w_out[l].astype(BF16)
        up_b = ffn_up[l].astype(BF16)
        down_b = ffn_down[l].astype(BF16)

        h = _modulate(_rmsnorm(x, ng[0]), sh1, sc1).astype(BF16)
        hc = _modulate(_rmsnorm(xc, ng[0]), csh1, csc1).astype(BF16)
        u = _lin(h, w_in_b)
        uc_s = _lin(hc, w_in_b[:, OFF_LRU_X:])

        xr_c = uc_s[..., :LRU_C]
        hc_sum, hc_end = _rglru(xr_c, *lru_p, jnp.zeros((bsz, 2, LRU_C), F32))
        kv_c, kr_c = _mla_keys_values(uc_s[..., OFF_MLA_KV - OFF_LRU_X:OFF_MLA_KR - OFF_LRU_X],
                                      uc_s[..., OFF_MLA_KR - OFF_LRU_X:D_IN - OFF_LRU_X], mla_gkv[l], wukv_b, None)

        y_hy = _hyena_mix(u[..., OFF_HY:OFF_LRU_G], *hy_p)
        xr = u[..., OFF_LRU_X:OFF_MLA_KV]
        h_sum, _ = _rglru(xr, *lru_p, hc_end)
        y_lru = jax.nn.gelu(u[..., OFF_LRU_G:OFF_MLA_Q], approximate=True) * h_sum
        q = _mla_query(u[..., OFF_MLA_Q:OFF_LRU_X], mla_gq[l], wuq_pad, rope)
        kv, kr = _mla_keys_values(u[..., OFF_MLA_KV:OFF_MLA_KR], u[..., OFF_MLA_KR:D_IN], mla_gkv[l], wukv_b, rope)
        y_mla = _attention(q, [(kv_c, kr_c), (kv, kr)])
        x = x + g1 * _rmsnorm(_mix_out(y_hy, y_lru, y_mla, head_g[l], w_out_b), ng[1])

        if need_ctx:
            uc_o = _lin(hc, w_in_b[:, :OFF_LRU_X])
            yc_hy = _hyena_mix(uc_o[..., OFF_HY:OFF_LRU_G], *hy_p)
            yc_lru = jax.nn.gelu(uc_o[..., OFF_LRU_G:OFF_MLA_Q], approximate=True) * hc_sum
            q_c = _mla_query(uc_o[..., OFF_MLA_Q:], mla_gq[l], wuq_pad, None)
            yc_mla = _attention(q_c, [(kv_c, kr_c)])
            xc = xc + cg1 * _rmsnorm(_mix_out(yc_hy, yc_lru, yc_mla, head_g[l], w_out_b), ng[1])

        x = _ffn_sublayer(x, ng[2], ng[3], sh2[:, 0], sc2[:, 0], g2[:, 0], up_b, ffn_conv[l], down_b)
        if need_ctx:
            xc = _ffn_sublayer(xc, ng[2], ng[3], csh2[None], csc2[None], cg2[None], up_b, ffn_conv[l], down_b)
    return x
```

```python
import functools
import math

import jax
import jax.numpy as jnp
from jax import lax
from jax.experimental import pallas as pl
from jax.experimental.pallas import tpu as pltpu

F32 = jnp.float32
BF16 = jnp.bfloat16

D_MODEL = 2048
GRID_W = 64
HEAD_W = 128
HY_C = 512
HY_ORDER = 2
HY_SHORT = 3
HY_BANDS = 16
HY_FAST_DECAY = 0.3
HY_SLOW_DECAY = 1.5
HY_DECAY_TARGET = 1e-2
LRU_C = 512
LRU_BLOCKS = 4
LRU_BW = LRU_C // LRU_BLOCKS
LRU_CONV = 4
RG_C = 8.0
MLA_H = 8
Q_RANK = 512
KV_RANK = 256
NOPE_D = 128
ROPE_D = 64
V_D = 128
ROPE_BASE = 10000.0
MLA_SCALE = (NOPE_D + ROPE_D) ** -0.5
D_MIX = HY_C + LRU_C + MLA_H * V_D
N_MIX_HEADS = D_MIX // HEAD_W
D_FF = 5632
FFN_CONV = 3
N_MOD = 6
EPS = 1e-6

OFF_HY = 0
OFF_LRU_G = OFF_HY + (HY_ORDER + 1) * HY_C
OFF_MLA_Q = OFF_LRU_G + LRU_C
OFF_LRU_X = OFF_MLA_Q + Q_RANK
OFF_MLA_KV = OFF_LRU_X + LRU_C
OFF_MLA_KR = OFF_MLA_KV + KV_RANK
D_IN = OFF_MLA_KR + ROPE_D

HEAD_QK = 2 * NOPE_D
VMEM_LIMIT = 48 * 1024 * 1024


def _pick(n, prefs):
    for p in prefs:
        if n % p == 0:
            return p
    return n


def _mm_kernel(a_ref, b_ref, o_ref):
    a = a_ref[...].astype(BF16)
    b = b_ref[...].astype(BF16)
    o_ref[...] = jnp.dot(a, b, preferred_element_type=F32).astype(o_ref.dtype)


def _mm(a, b, out_dtype=F32):
    m, k = a.shape
    _, n = b.shape
    mp = -(-m // 8) * 8
    if mp != m:
        a = jnp.pad(a, ((0, mp - m), (0, 0)))
    assert n % 128 == 0, n
    tn = max(t for t in range(128, 1153, 128) if n % t == 0)
    tm = 8
    for t in (1024, 512, 256, 128, 64, 32, 16, 8):
        need = 2 * (t * k * a.dtype.itemsize + k * tn * b.dtype.itemsize + t * tn * 4)
        if mp % t == 0 and need <= VMEM_LIMIT - (8 << 20):
            tm = t
            break
    out = pl.pallas_call(
        _mm_kernel,
        grid=(mp // tm, n // tn),
        in_specs=[pl.BlockSpec((tm, k), lambda i, j: (i, 0)),
                  pl.BlockSpec((k, tn), lambda i, j: (0, j))],
        out_specs=pl.BlockSpec((tm, tn), lambda i, j: (i, j)),
        out_shape=jax.ShapeDtypeStruct((mp, n), out_dtype),
        compiler_params=pltpu.CompilerParams(
            dimension_semantics=("parallel", "parallel"), vmem_limit_bytes=VMEM_LIMIT),
        name="mm",
    )(a, b)
    return out[:m] if mp != m else out


def _attn_kernel(*refs, n_seg):
    q_ref = refs[0]
    seg_refs = refs[1:1 + 3 * n_seg]
    o_ref = refs[1 + 3 * n_seg]
    k_scr, v_scr = refs[2 + 3 * n_seg:]

    @pl.when(pl.program_id(2) == 0)
    def _():
        off = 0
        for s in range(n_seg):
            kn_ref, v_ref, kr_ref = seg_refs[3 * s:3 * s + 3]
            n = kn_ref.shape[1]
            k_scr[off:off + n, 0:NOPE_D] = kn_ref[0]
            k_scr[off:off + n, NOPE_D:HEAD_QK] = kr_ref[0]
            v_scr[off:off + n, :] = v_ref[0]
            off += n

    q = q_ref[0]
    s = lax.dot_general(q, k_scr[...], (((1,), (1,)), ((), ())), preferred_element_type=F32)
    m = jnp.max(s, axis=-1, keepdims=True)
    p = jnp.exp(s - m)
    l = jnp.sum(p, axis=-1, keepdims=True)
    o = jnp.dot(p.astype(BF16), v_scr[...], preferred_element_type=F32)
    o_ref[0] = o / l


def _attention(q, segs):
    bsz, lq, _ = q.shape
    tq = _pick(lq, (256, 128))
    lk = sum(kv.shape[1] for kv, _ in segs)
    in_specs = [pl.BlockSpec((1, tq, HEAD_QK), lambda b, h, i: (b, i, h))]
    args = [q]
    for kv, kr in segs:
        n = kv.shape[1]
        in_specs += [pl.BlockSpec((1, n, NOPE_D), lambda b, h, i: (b, 0, 2 * h)),
                     pl.BlockSpec((1, n, V_D), lambda b, h, i: (b, 0, 2 * h + 1)),
                     pl.BlockSpec((1, n, NOPE_D), lambda b, h, i: (b, 0, 0))]
        args += [kv, kv, kr]
    return pl.pallas_call(
        functools.partial(_attn_kernel, n_seg=len(segs)),
        grid=(bsz, MLA_H, lq // tq),
        in_specs=in_specs,
        out_specs=pl.BlockSpec((1, tq, V_D), lambda b, h, i: (b, i, h)),
        out_shape=jax.ShapeDtypeStruct((bsz, lq, MLA_H * V_D), F32),
        scratch_shapes=[pltpu.VMEM((lk, HEAD_QK), BF16), pltpu.VMEM((lk, V_D), BF16)],
        compiler_params=pltpu.CompilerParams(
            dimension_semantics=("parallel", "parallel", "arbitrary"), vmem_limit_bytes=VMEM_LIMIT),
        name="attn",
    )(*args)


def _rmsnorm(x, g):
    xf = x.astype(F32)
    y = xf * lax.rsqrt(jnp.mean(xf * xf, axis=-1, keepdims=True) + EPS)
    return y * g.astype(F32)


def _modulate(h, shift, scale):
    return h * (1 + scale) + shift


def _dwconv(x, w, left):
    k = w.shape[0]
    length = x.shape[1]
    xp = jnp.pad(x, ((0, 0), (left, k - 1 - left), (0, 0)))
    out = xp[:, 0:length] * w[0]
    for j in range(1, k):
        out = out + xp[:, j:j + length] * w[j]
    return out


def _lin(x, w, out_dtype=F32):
    shp = x.shape
    return _mm(x.reshape(-1, shp[-1]), w, out_dtype).reshape(shp[:-1] + (w.shape[-1],))


LANES = 128
PGROUP = 2


def _fft_dims(length):
    n = 2 * length
    bn = 128 if length >= 1024 else 32
    return n // bn, bn, bn + SUBLANES


def _dft_tables(length):
    a_n, bn, _ = _fft_dims(length)
    n, half = a_n * bn, a_n // 2
    p = jnp.arange(a_n, dtype=jnp.int32)
    b = jnp.arange(bn, dtype=jnp.int32)
    nn = bn * p[None, None, :] + b[:, None, None]
    ang = (2.0 * math.pi / n) * ((p[None, :, None] * nn) % n).astype(F32)
    cr, ci = jnp.cos(ang), -jnp.sin(ang)
    m1 = jnp.concatenate([jnp.concatenate([cr[:, :, :half], -ci[:, :, :half]], axis=2),
                          jnp.concatenate([ci[:, :, :half], cr[:, :, :half]], axis=2)], axis=1)
    m1f = jnp.concatenate([cr, ci], axis=1)
    crt, cit = jnp.swapaxes(cr, 1, 2)[:, :half], jnp.swapaxes(ci, 1, 2)[:, :half]
    m4 = jnp.concatenate([jnp.concatenate([crt, cit], axis=2),
                          jnp.concatenate([-cit, crt], axis=2)], axis=1) / n
    ang2 = (2.0 * math.pi / bn) * ((b[:, None] * b[None, :]) % bn).astype(F32)
    wr, wi = jnp.cos(ang2), -jnp.sin(ang2)
    m2 = jnp.concatenate([jnp.concatenate([wr, -wi], axis=1), jnp.concatenate([wi, wr], axis=1)], axis=0)
    m3 = jnp.concatenate([jnp.concatenate([wr, wi], axis=1), jnp.concatenate([-wi, wr], axis=1)], axis=0)
    return tuple(t.astype(BF16) for t in (m1, m1f, m2, m3, m4))


def _stage1(src, rows_per_part, parts, m_ref, t2, a_n, bn, pitch):
    def body(b, carry):
        rhs = jnp.concatenate([src[pl.ds(part * rows_per_part * pitch + b, rows_per_part, stride=pitch), :]
                               for part in range(parts)], axis=0)
        t = jnp.dot(m_ref[b], rhs.astype(BF16), preferred_element_type=F32)
        t2[pl.ds(b, a_n, stride=pitch), :] = t[:a_n]
        t2[pl.ds(a_n * pitch + b, a_n, stride=pitch), :] = t[a_n:]
        return carry
    lax.fori_loop(0, bn, body, 0, unroll=2)


def _stage2_rhs(t2, g, a_n, bn, pitch):
    cols = []
    for k in range(PGROUP):
        r0 = pl.multiple_of((g * PGROUP + k) * pitch, SUBLANES)
        cols.append(jnp.concatenate([t2[pl.ds(r0, bn), :], t2[pl.ds(a_n * pitch + r0, bn), :]], axis=0))
    return jnp.concatenate(cols, axis=1).astype(BF16)


def _fftconv_kernel(z_ref, spec_ref, m1_ref, m2_ref, m3_ref, m4_ref, y_ref, zb, t2, *, length):
    a_n, bn, pitch = _fft_dims(length)
    half = a_n // 2
    for bi in range(2):
        for a in range(half):
            zb[(bi * half + a) * pitch:(bi * half + a) * pitch + bn, :] = z_ref[bi, a * bn:(a + 1) * bn, :]
    _stage1(zb, half, 2, m1_ref, t2, a_n, bn, pitch)

    def mid(g, carry):
        x = jnp.dot(m2_ref[...], _stage2_rhs(t2, g, a_n, bn, pitch), preferred_element_type=F32)
        k = spec_ref[0, 0, g]
        xr, xi, kr, ki = x[:bn], x[bn:], k[:bn], k[bn:]
        y = jnp.concatenate([xr * kr - xi * ki, xr * ki + xi * kr], axis=0).astype(BF16)
        u = jnp.dot(m3_ref[...], y, preferred_element_type=F32)
        for kk in range(PGROUP):
            r0 = pl.multiple_of((g * PGROUP + kk) * pitch, SUBLANES)
            t2[pl.ds(r0, bn), :] = u[:bn, kk * LANES:(kk + 1) * LANES]
            t2[pl.ds(a_n * pitch + r0, bn), :] = u[bn:, kk * LANES:(kk + 1) * LANES]
        return carry
    lax.fori_loop(0, a_n // PGROUP, mid, 0)

    def last(b, carry):
        rhs = jnp.concatenate([t2[pl.ds(b, a_n, stride=pitch), :],
                               t2[pl.ds(a_n * pitch + b, a_n, stride=pitch), :]], axis=0)
        y = jnp.dot(m4_ref[b], rhs.astype(BF16), preferred_element_type=F32)
        zb[pl.ds(b, half, stride=pitch), :] = y[:half]
        zb[pl.ds(half * pitch + b, half, stride=pitch), :] = y[half:]
        return carry
    lax.fori_loop(0, bn, last, 0, unroll=2)
    for bi in range(2):
        for a in range(half):
            y_ref[bi, a * bn:(a + 1) * bn, :] = zb[(bi * half + a) * pitch:(bi * half + a) * pitch + bn, :]


def _fftconv(z, spec, tables):
    bsz, length, chans = z.shape
    assert bsz == 2, "the two batch rows are packed as one complex signal"
    a_n, bn, pitch = _fft_dims(length)
    m1, _, m2, m3, m4 = tables
    const3 = lambda c: (0, 0, 0)
    const2 = lambda c: (0, 0)
    once = pl.Buffered(1)
    return pl.pallas_call(
        functools.partial(_fftconv_kernel, length=length),
        grid=(chans // LANES,),
        in_specs=[pl.BlockSpec((2, length, LANES), lambda c: (0, 0, c)),
                  pl.BlockSpec((1, 1, a_n // PGROUP, 2 * bn, PGROUP * LANES), lambda c: (0, c, 0, 0, 0),
                               pipeline_mode=once),
                  pl.BlockSpec(m1.shape, const3, pipeline_mode=once),
                  pl.BlockSpec(m2.shape, const2, pipeline_mode=once),
                  pl.BlockSpec(m3.shape, const2, pipeline_mode=once),
                  pl.BlockSpec(m4.shape, const3, pipeline_mode=once)],
        out_specs=pl.BlockSpec((2, length, LANES), lambda c: (0, 0, c)),
        out_shape=jax.ShapeDtypeStruct(z.shape, F32),
        scratch_shapes=[pltpu.VMEM((a_n * pitch, LANES), F32), pltpu.VMEM((2 * a_n * pitch, LANES), F32)],
        compiler_params=pltpu.CompilerParams(dimension_semantics=("parallel",), vmem_limit_bytes=VMEM_LIMIT),
        name="fftconv",
    )(z, spec[None], m1, m2, m3, m4)


def _split_bf16(x):
    hi = x.astype(BF16)
    return hi, (x - hi.astype(F32)).astype(BF16)


def _dot_split(a, b):
    ah, al = _split_bf16(a)
    bh, bl = _split_bf16(b)
    d = functools.partial(jnp.dot, preferred_element_type=F32)
    return d(ah, bh) + (d(al, bh) + d(ah, bl))


def _filtspec_kernel(hf_ref, hb_ref, w3f_ref, w3b_ref, dec_ref, m1_ref, m2_ref, spec_ref, kb, t2, *, length):
    a_n, bn, pitch = _fft_dims(length)
    half = a_n // 2
    row = lax.broadcasted_iota(jnp.int32, (length, 1), 0).astype(F32)
    delta = dec_ref[...]
    kf = _dot_split(hf_ref[...], w3f_ref[...]) * jnp.exp(-(row / length) * delta)
    kbk = _dot_split(hb_ref[...], w3b_ref[...]) * jnp.exp(-((length - row) / length) * delta)
    kbk = jnp.where(row == 0.0, 0.0, kbk)
    norm = jnp.sum(jnp.abs(kf), axis=0, keepdims=True) + jnp.sum(jnp.abs(kbk), axis=0, keepdims=True)
    kf, kbk = kf / norm, kbk / norm
    for a in range(half):
        kb[a * pitch:a * pitch + bn, :] = kf[a * bn:(a + 1) * bn]
        kb[(half + a) * pitch:(half + a) * pitch + bn, :] = kbk[a * bn:(a + 1) * bn]
    _stage1(kb, a_n, 1, m1_ref, t2, a_n, bn, pitch)

    def mid(g, carry):
        spec_ref[0, 0, g] = jnp.dot(m2_ref[...], _stage2_rhs(t2, g, a_n, bn, pitch), preferred_element_type=F32)
        return carry
    lax.fori_loop(0, a_n // PGROUP, mid, 0)


def _filter_spectra(length, w1, b1, w2, b2, w3, tables):
    a_n, bn, pitch = _fft_dims(length)
    hp = lax.Precision.HIGHEST
    t_f = jnp.arange(length, dtype=F32) / length
    t_b = (length - jnp.arange(length, dtype=F32)) / length
    band = jnp.arange(1, HY_BANDS + 1, dtype=F32)

    def hidden(t):
        ang = 2.0 * math.pi * t[:, None] * band
        feats = jnp.concatenate([t[:, None], jnp.sin(ang), jnp.cos(ang)], axis=-1)
        hid = jnp.sin(jnp.dot(feats, w1, precision=hp) + b1)
        return jnp.sin(jnp.dot(hid, w2, precision=hp) + b2)

    deltas = jnp.abs(jnp.linspace(math.log(HY_DECAY_TARGET) / HY_SLOW_DECAY,
                                  math.log(HY_DECAY_TARGET) / HY_FAST_DECAY, HY_C, dtype=F32))[None]
    _, m1f, m2, _, _ = tables
    nct = HY_C // LANES
    hid_f, hid_b = hidden(t_f), hidden(t_b)
    hidden_w = hid_f.shape[1]
    once = pl.Buffered(1)
    return pl.pallas_call(
        functools.partial(_filtspec_kernel, length=length),
        grid=(HY_ORDER, nct),
        in_specs=[pl.BlockSpec((length, hidden_w), lambda o, c: (0, 0), pipeline_mode=once),
                  pl.BlockSpec((length, hidden_w), lambda o, c: (0, 0), pipeline_mode=once),
                  pl.BlockSpec((hidden_w, LANES), lambda o, c: (0, o * 2 * nct + c)),
                  pl.BlockSpec((hidden_w, LANES), lambda o, c: (0, o * 2 * nct + nct + c)),
                  pl.BlockSpec((1, LANES), lambda o, c: (0, c)),
                  pl.BlockSpec(m1f.shape, lambda o, c: (0, 0, 0), pipeline_mode=once),
                  pl.BlockSpec(m2.shape, lambda o, c: (0, 0), pipeline_mode=once)],
        out_specs=pl.BlockSpec((1, 1, a_n // PGROUP, 2 * bn, PGROUP * LANES), lambda o, c: (o, c, 0, 0, 0)),
        out_shape=jax.ShapeDtypeStruct((HY_ORDER, nct, a_n // PGROUP, 2 * bn, PGROUP * LANES), F32),
        scratch_shapes=[pltpu.VMEM((a_n * pitch, LANES), F32), pltpu.VMEM((2 * a_n * pitch, LANES), F32)],
        compiler_params=pltpu.CompilerParams(
            dimension_semantics=("parallel", "parallel"), vmem_limit_bytes=VMEM_LIMIT),
        name="filtspec",
    )(hid_f, hid_b, w3, w3, deltas, m1f, m2)


def _hygate_kernel(p_ref, w_ref, *rest, first):
    x = p_ref[0]
    n = x.shape[0]
    row = lax.broadcasted_iota(jnp.int32, x.shape, 0)
    prev = jnp.where(row == 0, 0.0, pltpu.roll(x, 1, axis=0))
    nxt = jnp.where(row == n - 1, 0.0, pltpu.roll(x, n - 1, axis=0))
    c = w_ref[0:1, :] * prev + w_ref[1:2, :] * x + w_ref[2:3, :] * nxt
    if first:
        (o_ref,) = rest
        o_ref[0] = c
    else:
        y_ref, z_ref, b_ref, o_ref = rest
        o_ref[0] = c * (y_ref[0] + z_ref[0] * b_ref[...])


def _hygate(u, part, conv_w, y=None, z=None, bias=None):
    bsz, length, _ = u.shape
    nct = HY_C // LANES
    blk = lambda b, c: (b, 0, c)
    in_specs = [pl.BlockSpec((1, length, LANES), lambda b, c: (b, 0, part * nct + c)),
                pl.BlockSpec((HY_SHORT, LANES), lambda b, c: (0, part * nct + c))]
    args = [u, conv_w]
    if y is not None:
        in_specs += [pl.BlockSpec((1, length, LANES), blk), pl.BlockSpec((1, length, LANES), blk),
                     pl.BlockSpec((1, LANES), lambda b, c: (0, c))]
        args += [y, z, bias]
    return pl.pallas_call(
        functools.partial(_hygate_kernel, first=y is None),
        grid=(bsz, nct),
        in_specs=in_specs,
        out_specs=pl.BlockSpec((1, length, LANES), blk),
        out_shape=jax.ShapeDtypeStruct((bsz, length, HY_C), F32),
        compiler_params=pltpu.CompilerParams(
            dimension_semantics=("parallel", "parallel"), vmem_limit_bytes=VMEM_LIMIT),
        name="hygate",
    )(*args)


def _hyena_mix(u, conv_w, spec, bias, tables):
    z = _hygate(u, 0, conv_w)
    for o in range(HY_ORDER):
        y = _fftconv(z, spec[o], tables)
        z = _hygate(u, o + 1, conv_w, y, z, bias[o:o + 1])
    return z


SUBLANES = 8
PAD = LRU_CONV - 1


def _lru_kernel(x_ref, wg_ref, pv_ref, h0_ref, o_ref, end_ref, xp, a_scr, b_scr, *, steps, chunk):
    sub = lax.broadcasted_iota(jnp.int32, (SUBLANES, LRU_BW), 0)
    xp[PAD:PAD + steps] = x_ref[0]
    for k in range(PAD):
        tail = pltpu.roll(x_ref[0, steps - PAD + k], 1, axis=0)
        xp[k] = jnp.where(sub == 0, 0.0, tail)
        head = pltpu.roll(x_ref[0, k], SUBLANES - 1, axis=0)
        xp[PAD + steps + k] = jnp.where(sub == SUBLANES - 1, 0.0, head)

    n_chunks = steps // chunk
    for d in range(2):
        ba = pv_ref[d, 0:1, :]
        bx = pv_ref[d, 1:2, :]
        lam = pv_ref[d, 2:3, :]
        neg_c_sp = -RG_C * jax.nn.softplus(-lam)
        taps = [pv_ref[d, 3 + k:4 + k, :].reshape(1, 1, LRU_BW) for k in range(LRU_CONV)]
        w_gate = wg_ref[d, 0]
        base = 0 if d == 0 else PAD

        def gates(c, carry):
            j0 = pl.multiple_of(c * chunk, chunk)
            xc = taps[0] * xp[pl.ds(j0 + base, chunk)]
            for k in range(1, LRU_CONV):
                xc = xc + taps[k] * xp[pl.ds(j0 + base + k, chunk)]
            xc2 = xc.reshape(chunk * SUBLANES, LRU_BW)
            g = jnp.dot(xc2.astype(BF16), w_gate, preferred_element_type=F32)
            r = jax.nn.sigmoid(g[:, :LRU_BW] + ba)
            i = jax.nn.sigmoid(g[:, LRU_BW:] + bx)
            log_a = neg_c_sp * r
            a = jnp.exp(log_a)
            t = jnp.tanh(log_a)
            b = jnp.sqrt(-2.0 * t / (1.0 - t)) * (i * xc2)
            a_scr[pl.ds(j0, chunk)] = a.reshape(chunk, SUBLANES, LRU_BW)
            b_scr[pl.ds(j0, chunk)] = b.reshape(chunk, SUBLANES, LRU_BW)
            return carry

        lax.fori_loop(0, n_chunks, gates, 0)

        def step(t, carry):
            h, p = carry
            j = t if d == 0 else steps - 1 - t
            a = a_scr[j]
            h = a * h + b_scr[j]
            p = a * p
            b_scr[j] = h
            a_scr[j] = p
            return h, p

        h_loc, p_loc = lax.fori_loop(
            0, steps, step,
            (jnp.zeros((SUBLANES, LRU_BW), F32), jnp.ones((SUBLANES, LRU_BW), F32)), unroll=8)

        carry = h0_ref[0, d:d + 1, :]
        cin = jnp.zeros((SUBLANES, LRU_BW), F32)
        order = range(SUBLANES) if d == 0 else range(SUBLANES - 1, -1, -1)
        for s in order:
            cin = jnp.where(sub == s, jnp.broadcast_to(carry, (SUBLANES, LRU_BW)), cin)
            carry = p_loc[s:s + 1, :] * carry + h_loc[s:s + 1, :]
        end_ref[0, d:d + 1, :] = carry

        def fix(c, carry_):
            j0 = pl.multiple_of(c * chunk, chunk)
            h = b_scr[pl.ds(j0, chunk)] + a_scr[pl.ds(j0, chunk)] * cin[None]
            if d == 0:
                o_ref[0, pl.ds(j0, chunk)] = h
            else:
                o_ref[0, pl.ds(j0, chunk)] += h
            return carry_

        lax.fori_loop(0, n_chunks, fix, 0)


def _rglru(xr, lconv, lwa, lba, lwx, lbx, llam, h0):
    bsz, length, _ = xr.shape
    steps = length // SUBLANES
    chunk = min(32, steps)
    xs = xr.reshape(bsz, SUBLANES, steps, LRU_C).transpose(0, 2, 1, 3)
    wg = jnp.concatenate([lwa, lwx], axis=-1).astype(BF16)
    pv = jnp.concatenate([lba[:, None], lbx[:, None], llam[:, None], lconv,
                          jnp.zeros((2, 1, LRU_C), F32)], axis=1)
    o, ends = pl.pallas_call(
        functools.partial(_lru_kernel, steps=steps, chunk=chunk),
        grid=(bsz, LRU_BLOCKS),
        in_specs=[pl.BlockSpec((1, steps, SUBLANES, LRU_BW), lambda b, n: (b, 0, 0, n)),
                  pl.BlockSpec((2, 1, LRU_BW, 2 * LRU_BW), lambda b, n: (0, n, 0, 0)),
                  pl.BlockSpec((2, SUBLANES, LRU_BW), lambda b, n: (0, 0, n)),
                  pl.BlockSpec((1, 2, LRU_BW), lambda b, n: (b, 0, n))],
        out_specs=[pl.BlockSpec((1, steps, SUBLANES, LRU_BW), lambda b, n: (b, 0, 0, n)),
                   pl.BlockSpec((1, 2, LRU_BW), lambda b, n: (b, 0, n))],
        out_shape=[jax.ShapeDtypeStruct((bsz, steps, SUBLANES, LRU_C), F32),
                   jax.ShapeDtypeStruct((bsz, 2, LRU_C), F32)],
        scratch_shapes=[pltpu.VMEM((steps + 2 * PAD, SUBLANES, LRU_BW), F32),
                        pltpu.VMEM((steps, SUBLANES, LRU_BW), F32),
                        pltpu.VMEM((steps, SUBLANES, LRU_BW), F32)],
        compiler_params=pltpu.CompilerParams(
            dimension_semantics=("parallel", "parallel"), vmem_limit_bytes=VMEM_LIMIT),
        name="rglru",
    )(xs, wg, pv, h0)
    return o.transpose(0, 2, 1, 3).reshape(bsz, length, LRU_C), ends


def _rope_tables(length):
    rows = length // GRID_W
    row = jnp.repeat(jnp.arange(rows, dtype=F32), GRID_W)
    col = jnp.tile(jnp.arange(GRID_W, dtype=F32), rows)
    n_freq = ROPE_D // 4
    inv = ROPE_BASE ** (-jnp.arange(n_freq, dtype=F32) / n_freq)
    ang = jnp.concatenate([row[:, None] * inv, col[:, None] * inv], axis=-1)
    return jnp.cos(ang), jnp.sin(ang)


def _apply_rope(x, cos, sin):
    half = x.shape[-1] // 2
    x1 = x[..., :half]
    x2 = x[..., half:]
    return jnp.concatenate([x1 * cos - x2 * sin, x2 * cos + x1 * sin], axis=-1)


def _pad_heads_q(wuq):
    w = wuq.reshape(Q_RANK, MLA_H, NOPE_D + ROPE_D)
    w = jnp.pad(w, ((0, 0), (0, 0), (0, HEAD_QK - NOPE_D - ROPE_D)))
    return w.reshape(Q_RANK, MLA_H * HEAD_QK)


def _mla_query(uq, gq, wuq_pad, rope):
    bsz, length, _ = uq.shape
    q = _lin(_rmsnorm(uq, gq), wuq_pad).reshape(bsz, length, MLA_H, HEAD_QK)
    qn, qr = q[..., :NOPE_D], q[..., NOPE_D:NOPE_D + ROPE_D]
    if rope is not None:
        qr = _apply_rope(qr, rope[0][:, None], rope[1][:, None])
    q = jnp.concatenate([qn, qr, jnp.zeros_like(qr)], axis=-1) * MLA_SCALE
    return q.reshape(bsz, length, MLA_H * HEAD_QK).astype(BF16)


def _mla_keys_values(ukv, ukr, gkv, wukv, rope):
    kv = _lin(_rmsnorm(ukv, gkv), wukv, BF16)
    kr = ukr if rope is None else _apply_rope(ukr, rope[0], rope[1])
    kr = jnp.concatenate([kr, jnp.zeros_like(kr)], axis=-1).astype(BF16)
    return kv, kr


def _mix_out(y_hy, y_lru, y_mla, head_g, w_out):
    y = jnp.concatenate([y_hy, y_lru, y_mla], axis=-1)
    bsz, length, _ = y.shape
    y = _rmsnorm(y.reshape(bsz, length, N_MIX_HEADS, HEAD_W), head_g.reshape(N_MIX_HEADS, HEAD_W))
    return _lin(y.reshape(bsz, length, D_MIX), w_out)


HALO = 16


def _ffn_kernel(x_ref, xprev_ref, xnext_ref, ng_ref, mod_ref, wg_ref, wv_ref, cg_ref, cv_ref, wd_ref,
                o_ref, hbuf, acc, *, blocks_per_seq):
    i = pl.program_id(0)
    j = pl.program_id(1)
    tm = x_ref.shape[0]

    def norm_mod(xv):
        y = xv * lax.rsqrt(jnp.mean(xv * xv, axis=-1, keepdims=True) + EPS) * ng_ref[0:1, :]
        return (y * (1.0 + mod_ref[0, 1:2, :]) + mod_ref[0, 0:1, :]).astype(BF16)

    @pl.when(j == 0)
    def _():
        first = (i % blocks_per_seq) == 0
        last = (i % blocks_per_seq) == blocks_per_seq - 1
        hp = norm_mod(xprev_ref[...])
        hn = norm_mod(xnext_ref[...])
        hbuf[0:HALO] = jnp.where(first, jnp.zeros_like(hp), hp)
        hbuf[HALO:HALO + tm] = norm_mod(x_ref[...])
        hbuf[HALO + tm:] = jnp.where(last, jnp.zeros_like(hn), hn)
        acc[...] = jnp.zeros_like(acc)

    hb = hbuf[...]

    def conv(up, c_ref):
        prev = pltpu.roll(up, 1, axis=0)[HALO:HALO + tm]
        nxt = pltpu.roll(up, tm + 2 * HALO - 1, axis=0)[HALO:HALO + tm]
        return c_ref[0:1, :] * prev + c_ref[1:2, :] * up[HALO:HALO + tm] + c_ref[2:3, :] * nxt

    gate = conv(jnp.dot(hb, wg_ref[...], preferred_element_type=F32), cg_ref)
    val = conv(jnp.dot(hb, wv_ref[...], preferred_element_type=F32), cv_ref)
    act = (jax.nn.gelu(gate, approximate=True) * val).astype(BF16)
    acc[...] += jnp.dot(act, wd_ref[...], preferred_element_type=F32)

    @pl.when(j == pl.num_programs(1) - 1)
    def _():
        y = acc[...]
        y = y * lax.rsqrt(jnp.mean(y * y, axis=-1, keepdims=True) + EPS) * ng_ref[1:2, :]
        o_ref[...] = x_ref[...] + mod_ref[0, 2:3, :] * y


def _ffn_sublayer(x, ng_pre, ng_post, shift, scale, gate, w_up, w_conv, w_down):
    bsz, length, d = x.shape
    m = bsz * length
    tm = _pick(length, (512, 256, 128))
    tf = _pick(D_FF, (512, 256, 128))
    nf = D_FF // tf
    bps = length // tm
    hb = tm // HALO
    nrow = m // HALO
    per_batch = shift.shape[0] > 1
    ng = jnp.stack([ng_pre, ng_post])
    mod = jnp.stack([shift, scale, gate], axis=1)
    out = pl.pallas_call(
        functools.partial(_ffn_kernel, blocks_per_seq=bps),
        grid=(m // tm, nf),
        in_specs=[pl.BlockSpec((tm, d), lambda i, j: (i, 0)),
                  pl.BlockSpec((HALO, d), lambda i, j: (jnp.maximum(i * hb - 1, 0), 0)),
                  pl.BlockSpec((HALO, d), lambda i, j: (jnp.minimum((i + 1) * hb, nrow - 1), 0)),
                  pl.BlockSpec((2, d), lambda i, j: (0, 0)),
                  pl.BlockSpec((1, 3, d), (lambda i, j: (i // bps, 0, 0)) if per_batch else (lambda i, j: (0, 0, 0))),
                  pl.BlockSpec((d, tf), lambda i, j: (0, j)),
                  pl.BlockSpec((d, tf), lambda i, j: (0, j + nf)),
                  pl.BlockSpec((FFN_CONV, tf), lambda i, j: (0, j)),
                  pl.BlockSpec((FFN_CONV, tf), lambda i, j: (0, j + nf)),
                  pl.BlockSpec((tf, d), lambda i, j: (j, 0))],
        out_specs=pl.BlockSpec((tm, d), lambda i, j: (i, 0)),
        out_shape=jax.ShapeDtypeStruct((m, d), F32),
        scratch_shapes=[pltpu.VMEM((tm + 2 * HALO, d), BF16), pltpu.VMEM((tm, d), F32)],
        compiler_params=pltpu.CompilerParams(
            dimension_semantics=("parallel", "arbitrary"), vmem_limit_bytes=VMEM_LIMIT),
        name="ffn",
    )(x.reshape(m, d), x.reshape(m, d), x.reshape(m, d), ng, mod, w_up, w_up, w_conv, w_conv, w_down)
    return out.reshape(bsz, length, d)


def kernel(x, c, ctx, c_ctx, ada_w, ada_b, norm_g, w_in, hy_conv, hy_w1, hy_b1, hy_w2, hy_b2, hy_w3, hy_bias, lru_conv, lru_wa, lru_ba, lru_wx, lru_bx, lru_lam, mla_gq, mla_gkv, mla_wuq, mla_wukv, head_g, w_out, ffn_up, ffn_conv, ffn_down):
    depth = ada_w.shape[0]
    bsz, seq, _ = x.shape
    ctx_len = ctx.shape[1]
    rope = _rope_tables(seq)
    dft_lat, dft_ctx = _dft_tables(seq), _dft_tables(ctx_len)
    xc = ctx
    for l in range(depth):
        need_ctx = l < depth - 1
        ng = norm_g[l]
        cc = jnp.concatenate([jax.nn.silu(c), jax.nn.silu(c_ctx)[None]], axis=0)
        mod_all = _mm(cc, ada_w[l]) + ada_b[l]
        sh1, sc1, g1, sh2, sc2, g2 = jnp.split(mod_all[:bsz, None, :], N_MOD, axis=-1)
        csh1, csc1, cg1, csh2, csc2, cg2 = jnp.split(mod_all[bsz], N_MOD, axis=-1)
        hy_f = (hy_w1[l], hy_b1[l], hy_w2[l], hy_b2[l], hy_w3[l])
        lru_p = (lru_conv[l], lru_wa[l], lru_ba[l], lru_wx[l], lru_bx[l], lru_lam[l])
        w_in_b = jnp.pad(w_in[l], ((0, 0), (0, NOPE_D - ROPE_D))).astype(BF16)
        wuq_pad = _pad_heads_q(mla_wuq[l]).astype(BF16)
        wukv_b = mla_wukv[l].astype(BF16)
        w_out_b = w_out[l].astype(BF16)
        up_b = ffn_up[l].astype(BF16)
        down_b = ffn_down[l].astype(BF16)

        h = _modulate(_rmsnorm(x, ng[0]), sh1, sc1).astype(BF16)
        hc = _modulate(_rmsnorm(xc, ng[0]), csh1, csc1).astype(BF16)
        u = _lin(h, w_in_b)
        uc_s = _lin(hc, w_in_b[:, OFF_LRU_X:])

        xr_c = uc_s[..., :LRU_C]
        hc_sum, hc_end = _rglru(xr_c, *lru_p, jnp.zeros((bsz, 2, LRU_C), F32))
        kv_c, kr_c = _mla_keys_values(uc_s[..., OFF_MLA_KV - OFF_LRU_X:OFF_MLA_KR - OFF_LRU_X],
                                      uc_s[..., OFF_MLA_KR - OFF_LRU_X:D_IN - OFF_LRU_X], mla_gkv[l], wukv_b, None)

        y_hy = _hyena_mix(u, hy_conv[l], _filter_spectra(seq, *hy_f, dft_lat), hy_bias[l], dft_lat)
        xr = u[..., OFF_LRU_X:OFF_MLA_KV]
        h_sum, _ = _rglru(xr, *lru_p, hc_end)
        y_lru = jax.nn.gelu(u[..., OFF_LRU_G:OFF_MLA_Q], approximate=True) * h_sum
        q = _mla_query(u[..., OFF_MLA_Q:OFF_LRU_X], mla_gq[l], wuq_pad, rope)
        kv, kr = _mla_keys_values(u[..., OFF_MLA_KV:OFF_MLA_KR], u[..., OFF_MLA_KR:D_IN], mla_gkv[l], wukv_b, rope)
        y_mla = _attention(q, [(kv_c, kr_c), (kv, kr)])
        x = x + g1 * _rmsnorm(_mix_out(y_hy, y_lru, y_mla, head_g[l], w_out_b), ng[1])

        if need_ctx:
            uc_o = _lin(hc, w_in_b[:, :OFF_LRU_X])
            yc_hy = _hyena_mix(uc_o, hy_conv[l], _filter_spectra(ctx_len, *hy_f, dft_ctx), hy_bias[l], dft_ctx)
            yc_lru = jax.nn.gelu(uc_o[..., OFF_LRU_G:OFF_MLA_Q], approximate=True) * hc_sum
            q_c = _mla_query(uc_o[..., OFF_MLA_Q:], mla_gq[l], wuq_pad, None)
            yc_mla = _attention(q_c, [(kv_c, kr_c)])
            xc = xc + cg1 * _rmsnorm(_mix_out(yc_hy, yc_lru, yc_mla, head_g[l], w_out_b), ng[1])

        x = _ffn_sublayer(x, ng[2], ng[3], sh2[:, 0], sc2[:, 0], g2[:, 0], up_b, ffn_conv[l], down_b)
        if need_ctx:
            xc = _ffn_sublayer(xc, ng[2], ng[3], csh2[None], csc2[None], cg2[None], up_b, ffn_conv[l], down_b)
    return x
```

```python
import functools
import math

import jax
import jax.numpy as jnp
from jax import lax
from jax.experimental import pallas as pl
from jax.experimental.pallas import tpu as pltpu

F32 = jnp.float32
BF16 = jnp.bfloat16

D_MODEL = 2048
GRID_W = 64
HEAD_W = 128
HY_C = 512
HY_ORDER = 2
HY_SHORT = 3
HY_BANDS = 16
HY_FAST_DECAY = 0.3
HY_SLOW_DECAY = 1.5
HY_DECAY_TARGET = 1e-2
LRU_C = 512
LRU_BLOCKS = 4
LRU_BW = LRU_C // LRU_BLOCKS
LRU_CONV = 4
RG_C = 8.0
MLA_H = 8
Q_RANK = 512
KV_RANK = 256
NOPE_D = 128
ROPE_D = 64
V_D = 128
ROPE_BASE = 10000.0
MLA_SCALE = (NOPE_D + ROPE_D) ** -0.5
D_MIX = HY_C + LRU_C + MLA_H * V_D
N_MIX_HEADS = D_MIX // HEAD_W
D_FF = 5632
FFN_CONV = 3
N_MOD = 6
EPS = 1e-6

OFF_HY = 0
OFF_LRU_G = OFF_HY + (HY_ORDER + 1) * HY_C
OFF_MLA_Q = OFF_LRU_G + LRU_C
OFF_LRU_X = OFF_MLA_Q + Q_RANK
OFF_MLA_KV = OFF_LRU_X + LRU_C
OFF_MLA_KR = OFF_MLA_KV + KV_RANK
D_IN = OFF_MLA_KR + ROPE_D

HEAD_QK = 2 * NOPE_D
VMEM_LIMIT = 48 * 1024 * 1024


def _pick(n, prefs):
    for p in prefs:
        if n % p == 0:
            return p
    return n


def _mm_kernel(a_ref, b_ref, o_ref):
    a = a_ref[...].astype(BF16)
    b = b_ref[...].astype(BF16)
    o_ref[...] = jnp.dot(a, b, preferred_element_type=F32).astype(o_ref.dtype)


def _mm(a, b, out_dtype=F32):
    m, k = a.shape
    _, n = b.shape
    mp = -(-m // 8) * 8
    if mp != m:
        a = jnp.pad(a, ((0, mp - m), (0, 0)))
    assert n % 128 == 0, n
    tn = max(t for t in range(128, 1153, 128) if n % t == 0)
    tm = 8
    for t in (1024, 512, 256, 128, 64, 32, 16, 8):
        need = 2 * (t * k * a.dtype.itemsize + k * tn * b.dtype.itemsize + t * tn * 4)
        if mp % t == 0 and need <= VMEM_LIMIT - (8 << 20):
            tm = t
            break
    out = pl.pallas_call(
        _mm_kernel,
        grid=(mp // tm, n // tn),
        in_specs=[pl.BlockSpec((tm, k), lambda i, j: (i, 0)),
                  pl.BlockSpec((k, tn), lambda i, j: (0, j))],
        out_specs=pl.BlockSpec((tm, tn), lambda i, j: (i, j)),
        out_shape=jax.ShapeDtypeStruct((mp, n), out_dtype),
        compiler_params=pltpu.CompilerParams(
            dimension_semantics=("parallel", "parallel"), vmem_limit_bytes=VMEM_LIMIT),
        name="mm",
    )(a, b)
    return out[:m] if mp != m else out


Q_SPLIT = 2

def _attn_kernel(*refs, n_seg):
    q_ref = refs[0]
    seg_refs = refs[1:1 + 3 * n_seg]
    o_ref = refs[1 + 3 * n_seg]
    k_scr, v_scr = refs[2 + 3 * n_seg:]

    @pl.when(pl.program_id(2) == 0)
    def _():
        off = 0
        for s in range(n_seg):
            kn_ref, v_ref, kr_ref = seg_refs[3 * s:3 * s + 3]
            n = kn_ref.shape[1]
            k_scr[off:off + n, 0:NOPE_D] = kn_ref[0]
            k_scr[off:off + n, NOPE_D:HEAD_QK] = kr_ref[0]
            v_scr[off:off + n, :] = v_ref[0]
            off += n

    sub = q_ref.shape[1] // Q_SPLIT
    for part in range(Q_SPLIT):
        rows = slice(part * sub, (part + 1) * sub)
        s = lax.dot_general(q_ref[0, rows], k_scr[...], (((1,), (1,)), ((), ())), preferred_element_type=F32)
        m = jnp.max(s, axis=-1, keepdims=True)
        p = jnp.exp(s - m)
        l = jnp.sum(p, axis=-1, keepdims=True)
        o = jnp.dot(p.astype(BF16), v_scr[...], preferred_element_type=F32)
        o_ref[0, rows] = o / l


def _attention(q, segs):
    bsz, lq, _ = q.shape
    tq = _pick(lq, (512, 256, 128))
    lk = sum(kv.shape[1] for kv, _ in segs)
    in_specs = [pl.BlockSpec((1, tq, HEAD_QK), lambda b, h, i: (b, i, h))]
    args = [q]
    for kv, kr in segs:
        n = kv.shape[1]
        in_specs += [pl.BlockSpec((1, n, NOPE_D), lambda b, h, i: (b, 0, 2 * h)),
                     pl.BlockSpec((1, n, V_D), lambda b, h, i: (b, 0, 2 * h + 1)),
                     pl.BlockSpec((1, n, NOPE_D), lambda b, h, i: (b, 0, 0))]
        args += [kv, kv, kr]
    return pl.pallas_call(
        functools.partial(_attn_kernel, n_seg=len(segs)),
        grid=(bsz, MLA_H, lq // tq),
        in_specs=in_specs,
        out_specs=pl.BlockSpec((1, tq, V_D), lambda b, h, i: (b, i, h)),
        out_shape=jax.ShapeDtypeStruct((bsz, lq, MLA_H * V_D), F32),
        scratch_shapes=[pltpu.VMEM((lk, HEAD_QK), BF16), pltpu.VMEM((lk, V_D), BF16)],
        compiler_params=pltpu.CompilerParams(
            dimension_semantics=("parallel", "parallel", "arbitrary"), vmem_limit_bytes=VMEM_LIMIT),
        name="attn",
    )(*args)


LANES = 128
PGROUP = 2


def _fft_dims(length):
    n = 2 * length
    bn = 128 if length >= 1024 else 32
    return n // bn, bn, bn + SUBLANES


def _dft_tables(length):
    a_n, bn, _ = _fft_dims(length)
    n, half = a_n * bn, a_n // 2
    p = jnp.arange(a_n, dtype=jnp.int32)
    b = jnp.arange(bn, dtype=jnp.int32)
    nn = bn * p[None, None, :] + b[:, None, None]
    ang = (2.0 * math.pi / n) * ((p[None, :, None] * nn) % n).astype(F32)
    cr, ci = jnp.cos(ang), -jnp.sin(ang)
    m1 = jnp.concatenate([jnp.concatenate([cr[:, :, :half], -ci[:, :, :half]], axis=2),
                          jnp.concatenate([ci[:, :, :half], cr[:, :, :half]], axis=2)], axis=1)
    m1f = jnp.concatenate([cr, ci], axis=1)
    crt, cit = jnp.swapaxes(cr, 1, 2)[:, :half], jnp.swapaxes(ci, 1, 2)[:, :half]
    m4 = jnp.concatenate([jnp.concatenate([crt, cit], axis=2),
                          jnp.concatenate([-cit, crt], axis=2)], axis=1) / n
    ang2 = (2.0 * math.pi / bn) * ((b[:, None] * b[None, :]) % bn).astype(F32)
    wr, wi = jnp.cos(ang2), -jnp.sin(ang2)
    m2 = jnp.concatenate([jnp.concatenate([wr, -wi], axis=1), jnp.concatenate([wi, wr], axis=1)], axis=0)
    m3 = jnp.concatenate([jnp.concatenate([wr, wi], axis=1), jnp.concatenate([-wi, wr], axis=1)], axis=0)
    return tuple(t.astype(BF16) for t in (m1, m1f, m2, m3, m4))


def _stage1(src, rows_per_part, parts, m_ref, t2, a_n, bn, pitch):
    def body(b, carry):
        rhs = jnp.concatenate([src[pl.ds(part * rows_per_part * pitch + b, rows_per_part, stride=pitch), :]
                               for part in range(parts)], axis=0)
        t = jnp.dot(m_ref[b], rhs.astype(BF16), preferred_element_type=F32)
        t2[pl.ds(b, a_n, stride=pitch), :] = t[:a_n]
        t2[pl.ds(a_n * pitch + b, a_n, stride=pitch), :] = t[a_n:]
        return carry
    lax.fori_loop(0, bn, body, 0, unroll=2)


def _stage2_rhs(t2, g, a_n, bn, pitch):
    cols = []
    for k in range(PGROUP):
        r0 = pl.multiple_of((g * PGROUP + k) * pitch, SUBLANES)
        cols.append(jnp.concatenate([t2[pl.ds(r0, bn), :], t2[pl.ds(a_n * pitch + r0, bn), :]], axis=0))
    return jnp.concatenate(cols, axis=1).astype(BF16)


def _fftconv_kernel(z_ref, spec_ref, m1_ref, m2_ref, m3_ref, m4_ref, y_ref, zb, t2, *, length):
    a_n, bn, pitch = _fft_dims(length)
    half = a_n // 2
    for bi in range(2):
        for a in range(half):
            zb[(bi * half + a) * pitch:(bi * half + a) * pitch + bn, :] = z_ref[bi, a * bn:(a + 1) * bn, :]
    _stage1(zb, half, 2, m1_ref, t2, a_n, bn, pitch)

    def mid(g, carry):
        x = jnp.dot(m2_ref[...], _stage2_rhs(t2, g, a_n, bn, pitch), preferred_element_type=F32)
        k = spec_ref[0, 0, g]
        xr, xi, kr, ki = x[:bn], x[bn:], k[:bn], k[bn:]
        y = jnp.concatenate([xr * kr - xi * ki, xr * ki + xi * kr], axis=0).astype(BF16)
        u = jnp.dot(m3_ref[...], y, preferred_element_type=F32)
        for kk in range(PGROUP):
            r0 = pl.multiple_of((g * PGROUP + kk) * pitch, SUBLANES)
            t2[pl.ds(r0, bn), :] = u[:bn, kk * LANES:(kk + 1) * LANES]
            t2[pl.ds(a_n * pitch + r0, bn), :] = u[bn:, kk * LANES:(kk + 1) * LANES]
        return carry
    lax.fori_loop(0, a_n // PGROUP, mid, 0)

    def last(b, carry):
        rhs = jnp.concatenate([t2[pl.ds(b, a_n, stride=pitch), :],
                               t2[pl.ds(a_n * pitch + b, a_n, stride=pitch), :]], axis=0)
        y = jnp.dot(m4_ref[b], rhs.astype(BF16), preferred_element_type=F32)
        zb[pl.ds(b, half, stride=pitch), :] = y[:half]
        zb[pl.ds(half * pitch + b, half, stride=pitch), :] = y[half:]
        return carry
    lax.fori_loop(0, bn, last, 0, unroll=2)
    for bi in range(2):
        for a in range(half):
            y_ref[bi, a * bn:(a + 1) * bn, :] = zb[(bi * half + a) * pitch:(bi * half + a) * pitch + bn, :]


def _fftconv(z, spec, tables):
    bsz, length, chans = z.shape
    assert bsz == 2, "the two batch rows are packed as one complex signal"
    a_n, bn, pitch = _fft_dims(length)
    m1, _, m2, m3, m4 = tables
    const3 = lambda c: (0, 0, 0)
    const2 = lambda c: (0, 0)
    once = pl.Buffered(1)
    return pl.pallas_call(
        functools.partial(_fftconv_kernel, length=length),
        grid=(chans // LANES,),
        in_specs=[pl.BlockSpec((2, length, LANES), lambda c: (0, 0, c)),
                  pl.BlockSpec((1, 1, a_n // PGROUP, 2 * bn, PGROUP * LANES), lambda c: (0, c, 0, 0, 0),
                               pipeline_mode=once),
                  pl.BlockSpec(m1.shape, const3, pipeline_mode=once),
                  pl.BlockSpec(m2.shape, const2, pipeline_mode=once),
                  pl.BlockSpec(m3.shape, const2, pipeline_mode=once),
                  pl.BlockSpec(m4.shape, const3, pipeline_mode=once)],
        out_specs=pl.BlockSpec((2, length, LANES), lambda c: (0, 0, c)),
        out_shape=jax.ShapeDtypeStruct(z.shape, F32),
        scratch_shapes=[pltpu.VMEM((a_n * pitch, LANES), F32), pltpu.VMEM((2 * a_n * pitch, LANES), F32)],
        compiler_params=pltpu.CompilerParams(dimension_semantics=("parallel",), vmem_limit_bytes=VMEM_LIMIT),
        name="fftconv",
    )(z, spec[None], m1, m2, m3, m4)


def _split_bf16(x):
    hi = x.astype(BF16)
    return hi, (x - hi.astype(F32)).astype(BF16)


def _dot_split(a, b):
    ah, al = _split_bf16(a)
    bh, bl = _split_bf16(b)
    d = functools.partial(jnp.dot, preferred_element_type=F32)
    return d(ah, bh) + (d(al, bh) + d(ah, bl))


def _filtspec_kernel(hf_ref, hb_ref, w3f_ref, w3b_ref, dec_ref, m1_ref, m2_ref, spec_ref, kb, t2, *, length):
    a_n, bn, pitch = _fft_dims(length)
    half = a_n // 2
    row = lax.broadcasted_iota(jnp.int32, (length, 1), 0).astype(F32)
    delta = dec_ref[...]
    kf = _dot_split(hf_ref[...], w3f_ref[...]) * jnp.exp(-(row / length) * delta)
    kbk = _dot_split(hb_ref[...], w3b_ref[...]) * jnp.exp(-((length - row) / length) * delta)
    kbk = jnp.where(row == 0.0, 0.0, kbk)
    norm = jnp.sum(jnp.abs(kf), axis=0, keepdims=True) + jnp.sum(jnp.abs(kbk), axis=0, keepdims=True)
    kf, kbk = kf / norm, kbk / norm
    for a in range(half):
        kb[a * pitch:a * pitch + bn, :] = kf[a * bn:(a + 1) * bn]
        kb[(half + a) * pitch:(half + a) * pitch + bn, :] = kbk[a * bn:(a + 1) * bn]
    _stage1(kb, a_n, 1, m1_ref, t2, a_n, bn, pitch)

    def mid(g, carry):
        spec_ref[0, 0, g] = jnp.dot(m2_ref[...], _stage2_rhs(t2, g, a_n, bn, pitch), preferred_element_type=F32)
        return carry
    lax.fori_loop(0, a_n // PGROUP, mid, 0)


def _filter_spectra(length, w1, b1, w2, b2, w3, tables):
    a_n, bn, pitch = _fft_dims(length)
    hp = lax.Precision.HIGHEST
    t_f = jnp.arange(length, dtype=F32) / length
    t_b = (length - jnp.arange(length, dtype=F32)) / length
    band = jnp.arange(1, HY_BANDS + 1, dtype=F32)

    def hidden(t):
        ang = 2.0 * math.pi * t[:, None] * band
        feats = jnp.concatenate([t[:, None], jnp.sin(ang), jnp.cos(ang)], axis=-1)
        hid = jnp.sin(jnp.dot(feats, w1, precision=hp) + b1)
        return jnp.sin(jnp.dot(hid, w2, precision=hp) + b2)

    deltas = jnp.abs(jnp.linspace(math.log(HY_DECAY_TARGET) / HY_SLOW_DECAY,
                                  math.log(HY_DECAY_TARGET) / HY_FAST_DECAY, HY_C, dtype=F32))[None]
    _, m1f, m2, _, _ = tables
    nct = HY_C // LANES
    hid_f, hid_b = hidden(t_f), hidden(t_b)
    hidden_w = hid_f.shape[1]
    once = pl.Buffered(1)
    return pl.pallas_call(
        functools.partial(_filtspec_kernel, length=length),
        grid=(HY_ORDER, nct),
        in_specs=[pl.BlockSpec((length, hidden_w), lambda o, c: (0, 0), pipeline_mode=once),
                  pl.BlockSpec((length, hidden_w), lambda o, c: (0, 0), pipeline_mode=once),
                  pl.BlockSpec((hidden_w, LANES), lambda o, c: (0, o * 2 * nct + c)),
                  pl.BlockSpec((hidden_w, LANES), lambda o, c: (0, o * 2 * nct + nct + c)),
                  pl.BlockSpec((1, LANES), lambda o, c: (0, c)),
                  pl.BlockSpec(m1f.shape, lambda o, c: (0, 0, 0), pipeline_mode=once),
                  pl.BlockSpec(m2.shape, lambda o, c: (0, 0), pipeline_mode=once)],
        out_specs=pl.BlockSpec((1, 1, a_n // PGROUP, 2 * bn, PGROUP * LANES), lambda o, c: (o, c, 0, 0, 0)),
        out_shape=jax.ShapeDtypeStruct((HY_ORDER, nct, a_n // PGROUP, 2 * bn, PGROUP * LANES), F32),
        scratch_shapes=[pltpu.VMEM((a_n * pitch, LANES), F32), pltpu.VMEM((2 * a_n * pitch, LANES), F32)],
        compiler_params=pltpu.CompilerParams(
            dimension_semantics=("parallel", "parallel"), vmem_limit_bytes=VMEM_LIMIT),
        name="filtspec",
    )(hid_f, hid_b, w3, w3, deltas, m1f, m2)


def _hygate_kernel(p_ref, w_ref, *rest, first):
    x = p_ref[0]
    n = x.shape[0]
    row = lax.broadcasted_iota(jnp.int32, x.shape, 0)
    prev = jnp.where(row == 0, 0.0, pltpu.roll(x, 1, axis=0))
    nxt = jnp.where(row == n - 1, 0.0, pltpu.roll(x, n - 1, axis=0))
    c = w_ref[0:1, :] * prev + w_ref[1:2, :] * x + w_ref[2:3, :] * nxt
    if first:
        (o_ref,) = rest
        o_ref[0] = c
    else:
        y_ref, z_ref, b_ref, o_ref = rest
        o_ref[0] = c * (y_ref[0] + z_ref[0] * b_ref[...])


def _hygate(u, part, conv_w, y=None, z=None, bias=None):
    bsz, length, _ = u.shape
    nct = HY_C // LANES
    blk = lambda b, c: (b, 0, c)
    in_specs = [pl.BlockSpec((1, length, LANES), lambda b, c: (b, 0, part * nct + c)),
                pl.BlockSpec((HY_SHORT, LANES), lambda b, c: (0, part * nct + c))]
    args = [u, conv_w]
    if y is not None:
        in_specs += [pl.BlockSpec((1, length, LANES), blk), pl.BlockSpec((1, length, LANES), blk),
                     pl.BlockSpec((1, LANES), lambda b, c: (0, c))]
        args += [y, z, bias]
    return pl.pallas_call(
        functools.partial(_hygate_kernel, first=y is None),
        grid=(bsz, nct),
        in_specs=in_specs,
        out_specs=pl.BlockSpec((1, length, LANES), blk),
        out_shape=jax.ShapeDtypeStruct((bsz, length, HY_C), F32),
        compiler_params=pltpu.CompilerParams(
            dimension_semantics=("parallel", "parallel"), vmem_limit_bytes=VMEM_LIMIT),
        name="hygate",
    )(*args)


def _hyena_mix(u, conv_w, spec, bias, tables):
    z = _hygate(u, 0, conv_w)
    for o in range(HY_ORDER):
        y = _fftconv(z, spec[o], tables)
        z = _hygate(u, o + 1, conv_w, y, z, bias[o:o + 1])
    return z


SUBLANES = 8
PAD = LRU_CONV - 1


def _lru_kernel(x_ref, wg_ref, pv_ref, h0_ref, o_ref, end_ref, xp, a_scr, b_scr, *, steps, chunk):
    sub = lax.broadcasted_iota(jnp.int32, (SUBLANES, LRU_BW), 0)
    xp[PAD:PAD + steps] = x_ref[0]
    for k in range(PAD):
        tail = pltpu.roll(x_ref[0, steps - PAD + k], 1, axis=0)
        xp[k] = jnp.where(sub == 0, 0.0, tail)
        head = pltpu.roll(x_ref[0, k], SUBLANES - 1, axis=0)
        xp[PAD + steps + k] = jnp.where(sub == SUBLANES - 1, 0.0, head)

    n_chunks = steps // chunk
    for d in range(2):
        ba = pv_ref[d, 0:1, :]
        bx = pv_ref[d, 1:2, :]
        lam = pv_ref[d, 2:3, :]
        neg_c_sp = -RG_C * jax.nn.softplus(-lam)
        taps = [pv_ref[d, 3 + k:4 + k, :].reshape(1, 1, LRU_BW) for k in range(LRU_CONV)]
        w_gate = wg_ref[d, 0]
        base = 0 if d == 0 else PAD

        def gates(c, carry):
            j0 = pl.multiple_of(c * chunk, chunk)
            xc = taps[0] * xp[pl.ds(j0 + base, chunk)]
            for k in range(1, LRU_CONV):
                xc = xc + taps[k] * xp[pl.ds(j0 + base + k, chunk)]
            xc2 = xc.reshape(chunk * SUBLANES, LRU_BW)
            g = jnp.dot(xc2.astype(BF16), w_gate, preferred_element_type=F32)
            r = jax.nn.sigmoid(g[:, :LRU_BW] + ba)
            i = jax.nn.sigmoid(g[:, LRU_BW:] + bx)
            log_a = neg_c_sp * r
            a = jnp.exp(log_a)
            t = jnp.tanh(log_a)
            b = jnp.sqrt(-2.0 * t / (1.0 - t)) * (i * xc2)
            a_scr[pl.ds(j0, chunk)] = a.reshape(chunk, SUBLANES, LRU_BW)
            b_scr[pl.ds(j0, chunk)] = b.reshape(chunk, SUBLANES, LRU_BW)
            return carry

        lax.fori_loop(0, n_chunks, gates, 0)

        def step(t, carry):
            h, p = carry
            j = t if d == 0 else steps - 1 - t
            a = a_scr[j]
            h = a * h + b_scr[j]
            p = a * p
            b_scr[j] = h
            a_scr[j] = p
            return h, p

        h_loc, p_loc = lax.fori_loop(
            0, steps, step,
            (jnp.zeros((SUBLANES, LRU_BW), F32), jnp.ones((SUBLANES, LRU_BW), F32)), unroll=8)

        carry = h0_ref[0, d:d + 1, :]
        cin = jnp.zeros((SUBLANES, LRU_BW), F32)
        order = range(SUBLANES) if d == 0 else range(SUBLANES - 1, -1, -1)
        for s in order:
            cin = jnp.where(sub == s, jnp.broadcast_to(carry, (SUBLANES, LRU_BW)), cin)
            carry = p_loc[s:s + 1, :] * carry + h_loc[s:s + 1, :]
        end_ref[0, d:d + 1, :] = carry

        def fix(c, carry_):
            j0 = pl.multiple_of(c * chunk, chunk)
            h = b_scr[pl.ds(j0, chunk)] + a_scr[pl.ds(j0, chunk)] * cin[None]
            if d == 0:
                o_ref[0, pl.ds(j0, chunk)] = h
            else:
                o_ref[0, pl.ds(j0, chunk)] += h
            return carry_

        lax.fori_loop(0, n_chunks, fix, 0)


def _rglru(xr, lconv, lwa, lba, lwx, lbx, llam, h0):
    bsz, length, _ = xr.shape
    steps = length // SUBLANES
    chunk = min(32, steps)
    xs = xr.reshape(bsz, SUBLANES, steps, LRU_C).transpose(0, 2, 1, 3)
    wg = jnp.concatenate([lwa, lwx], axis=-1).astype(BF16)
    pv = jnp.concatenate([lba[:, None], lbx[:, None], llam[:, None], lconv,
                          jnp.zeros((2, 1, LRU_C), F32)], axis=1)
    o, ends = pl.pallas_call(
        functools.partial(_lru_kernel, steps=steps, chunk=chunk),
        grid=(bsz, LRU_BLOCKS),
        in_specs=[pl.BlockSpec((1, steps, SUBLANES, LRU_BW), lambda b, n: (b, 0, 0, n)),
                  pl.BlockSpec((2, 1, LRU_BW, 2 * LRU_BW), lambda b, n: (0, n, 0, 0)),
                  pl.BlockSpec((2, SUBLANES, LRU_BW), lambda b, n: (0, 0, n)),
                  pl.BlockSpec((1, 2, LRU_BW), lambda b, n: (b, 0, n))],
        out_specs=[pl.BlockSpec((1, steps, SUBLANES, LRU_BW), lambda b, n: (b, 0, 0, n)),
                   pl.BlockSpec((1, 2, LRU_BW), lambda b, n: (b, 0, n))],
        out_shape=[jax.ShapeDtypeStruct((bsz, steps, SUBLANES, LRU_C), F32),
                   jax.ShapeDtypeStruct((bsz, 2, LRU_C), F32)],
        scratch_shapes=[pltpu.VMEM((steps + 2 * PAD, SUBLANES, LRU_BW), F32),
                        pltpu.VMEM((steps, SUBLANES, LRU_BW), F32),
                        pltpu.VMEM((steps, SUBLANES, LRU_BW), F32)],
        compiler_params=pltpu.CompilerParams(
            dimension_semantics=("parallel", "parallel"), vmem_limit_bytes=VMEM_LIMIT),
        name="rglru",
    )(xs, wg, pv, h0)
    return o.transpose(0, 2, 1, 3).reshape(bsz, length, LRU_C), ends


def _rope_tables(length):
    rows = length // GRID_W
    row = jnp.repeat(jnp.arange(rows, dtype=F32), GRID_W)
    col = jnp.tile(jnp.arange(GRID_W, dtype=F32), rows)
    n_freq = ROPE_D // 4
    inv = ROPE_BASE ** (-jnp.arange(n_freq, dtype=F32) / n_freq)
    ang = jnp.concatenate([row[:, None] * inv, col[:, None] * inv], axis=-1)
    return jnp.cos(ang), jnp.sin(ang)


def _rope_lane_tables(rope):
    cos, sin = rope
    z = jnp.zeros_like(cos)
    return (jnp.concatenate([cos, cos, z, z], axis=1), jnp.concatenate([-sin, z, z, z], axis=1),
            jnp.concatenate([z, sin, z, z], axis=1))


def _pad_heads_q(wuq):
    w = wuq.reshape(Q_RANK, MLA_H, NOPE_D + ROPE_D)
    w = jnp.pad(w, ((0, 0), (0, 0), (0, HEAD_QK - NOPE_D - ROPE_D)))
    return w.reshape(Q_RANK, MLA_H * HEAD_QK)


def _rms(xv, g):
    return xv * lax.rsqrt(jnp.mean(xv * xv, axis=-1, keepdims=True) + EPS) * g


def _proj_in_kernel(x_ref, g_ref, mod_ref, w_ref, o_ref, hbuf):
    @pl.when(pl.program_id(1) == 0)
    def _():
        y = _rms(x_ref[...], g_ref[...])
        hbuf[...] = (y * (1.0 + mod_ref[0, 1:2, :]) + mod_ref[0, 0:1, :]).astype(BF16)

    o_ref[...] = jnp.dot(hbuf[...], w_ref[...], preferred_element_type=F32)


def _proj_in(x, g, shift, scale, w):
    bsz, length, d = x.shape
    m, n = bsz * length, w.shape[1]
    tm = _pick(length, (512, 256, 128))
    tn = max(t for t in range(128, 1153, 128) if n % t == 0)
    bps = length // tm
    mod = jnp.stack([shift, scale], axis=1)
    mod_idx = (lambda i, j: (i // bps, 0, 0)) if shift.shape[0] > 1 else (lambda i, j: (0, 0, 0))
    return pl.pallas_call(
        _proj_in_kernel,
        grid=(m // tm, n // tn),
        in_specs=[pl.BlockSpec((tm, d), lambda i, j: (i, 0)),
                  pl.BlockSpec((1, d), lambda i, j: (0, 0)),
                  pl.BlockSpec((1, 2, d), mod_idx),
                  pl.BlockSpec((d, tn), lambda i, j: (0, j))],
        out_specs=pl.BlockSpec((tm, tn), lambda i, j: (i, j)),
        out_shape=jax.ShapeDtypeStruct((m, n), F32),
        scratch_shapes=[pltpu.VMEM((tm, d), BF16)],
        compiler_params=pltpu.CompilerParams(
            dimension_semantics=("parallel", "arbitrary"), vmem_limit_bytes=VMEM_LIMIT),
        name="proj_in",
    )(x.reshape(m, d), g[None], mod, w)


def _proj_qkv_kernel(*refs, rope, with_q):
    refs = list(refs)
    uq_ref = refs.pop(0) if with_q else None
    ukv_ref, ukr_ref = refs.pop(0), refs.pop(0)
    gq_ref, wq_ref = (refs.pop(0), refs.pop(0)) if with_q else (None, None)
    gkv_ref, wkv_ref = refs.pop(0), refs.pop(0)
    cos_ref, sa_ref, sb_ref = (refs.pop(0), refs.pop(0), refs.pop(0)) if rope else (None, None, None)
    q_ref = refs.pop(0) if with_q else None
    kv_ref, kr_ref = refs

    def rot(blk):
        if not rope:
            return blk
        return (blk * cos_ref[...] + pltpu.roll(blk, LANES - 32, axis=1) * sa_ref[...]
                + pltpu.roll(blk, 32, axis=1) * sb_ref[...])

    if with_q:
        nq = _rms(uq_ref[...], gq_ref[...]).astype(BF16)
        q = jnp.dot(nq, wq_ref[...], preferred_element_type=F32)
        for hd in range(MLA_H):
            lo = hd * HEAD_QK
            q_ref[:, lo:lo + NOPE_D] = (q[:, lo:lo + NOPE_D] * MLA_SCALE).astype(BF16)
            q_ref[:, lo + NOPE_D:lo + HEAD_QK] = (rot(q[:, lo + NOPE_D:lo + HEAD_QK]) * MLA_SCALE).astype(BF16)
    nkv = _rms(ukv_ref[...], gkv_ref[...]).astype(BF16)
    kv_ref[...] = jnp.dot(nkv, wkv_ref[...], preferred_element_type=F32).astype(BF16)
    kr_ref[...] = rot(ukr_ref[...]).astype(BF16)


def _proj_qkv(u, length, gq, wuq_pad, gkv, wukv, rope_tabs, with_q=True):
    m = u.shape[0]
    tm = _pick(length, (512, 256, 128))
    bps = length // tm
    row = lambda i: (i, 0)
    const = lambda i: (0, 0)
    in_specs, args = [], []
    if with_q:
        in_specs.append(pl.BlockSpec((tm, Q_RANK), lambda i: (i, OFF_MLA_Q // Q_RANK)))
        args.append(u)
    in_specs += [pl.BlockSpec((tm, KV_RANK), lambda i: (i, OFF_MLA_KV // KV_RANK)),
                 pl.BlockSpec((tm, LANES), lambda i: (i, OFF_MLA_KR // LANES))]
    args += [u, u]
    if with_q:
        in_specs += [pl.BlockSpec((1, Q_RANK), const), pl.BlockSpec(wuq_pad.shape, const)]
        args += [gq[None], wuq_pad]
    in_specs += [pl.BlockSpec((1, KV_RANK), const), pl.BlockSpec(wukv.shape, const)]
    args += [gkv[None], wukv]
    if rope_tabs is not None:
        in_specs += [pl.BlockSpec((tm, LANES), lambda i: (i % bps, 0))] * 3
        args += list(rope_tabs)
    out_specs = [pl.BlockSpec((tm, MLA_H * HEAD_QK), row), pl.BlockSpec((tm, LANES), row)]
    out_shape = [jax.ShapeDtypeStruct((m, MLA_H * HEAD_QK), BF16), jax.ShapeDtypeStruct((m, LANES), BF16)]
    if with_q:
        out_specs.insert(0, pl.BlockSpec((tm, MLA_H * HEAD_QK), row))
        out_shape.insert(0, jax.ShapeDtypeStruct((m, MLA_H * HEAD_QK), BF16))
    outs = pl.pallas_call(
        functools.partial(_proj_qkv_kernel, rope=rope_tabs is not None, with_q=with_q),
        grid=(m // tm,),
        in_specs=in_specs, out_specs=out_specs, out_shape=out_shape,
        compiler_params=pltpu.CompilerParams(dimension_semantics=("parallel",), vmem_limit_bytes=VMEM_LIMIT),
        name="proj_qkv",
    )(*args)
    return outs if with_q else [None] + list(outs)


def _mix_kernel(yhy_ref, hs_ref, gate_ref, ymla_ref, hg_ref, w_ref, x_ref, ng_ref, mod_ref, o_ref):
    y = jnp.concatenate([yhy_ref[...], jax.nn.gelu(gate_ref[...], approximate=True) * hs_ref[...],
                         ymla_ref[...]], axis=1)
    heads = [_rms(y[:, hd * HEAD_W:(hd + 1) * HEAD_W], hg_ref[:, hd * HEAD_W:(hd + 1) * HEAD_W]).astype(BF16)
             for hd in range(N_MIX_HEADS)]
    acc = jnp.dot(jnp.concatenate(heads, axis=1), w_ref[...], preferred_element_type=F32)
    o_ref[...] = x_ref[...] + mod_ref[0] * _rms(acc, ng_ref[...])


def _mix_out(x, y_hy, h_sum, u, y_mla, head_g, w_out, ng_post, gate):
    bsz, length, d = x.shape
    m = bsz * length
    tm = _pick(length, (256, 128))
    bps = length // tm
    row = lambda i: (i, 0)
    const = lambda i: (0, 0)
    mod_idx = (lambda i: (i // bps, 0, 0)) if gate.shape[0] > 1 else (lambda i: (0, 0, 0))
    out = pl.pallas_call(
        _mix_kernel,
        grid=(m // tm,),
        in_specs=[pl.BlockSpec((tm, HY_C), row), pl.BlockSpec((tm, LRU_C), row),
                  pl.BlockSpec((tm, LRU_C), lambda i: (i, OFF_LRU_G // LRU_C)),
                  pl.BlockSpec((tm, MLA_H * V_D), row),
                  pl.BlockSpec((1, D_MIX), const),
                  pl.BlockSpec(w_out.shape, const, pipeline_mode=pl.Buffered(1)),
                  pl.BlockSpec((tm, d), row), pl.BlockSpec((1, d), const), pl.BlockSpec((1, 1, d), mod_idx)],
        out_specs=pl.BlockSpec((tm, d), row),
        out_shape=jax.ShapeDtypeStruct((m, d), F32),
        compiler_params=pltpu.CompilerParams(dimension_semantics=("parallel",), vmem_limit_bytes=VMEM_LIMIT),
        name="mix_out",
    )(y_hy.reshape(m, HY_C), h_sum.reshape(m, LRU_C), u, y_mla.reshape(m, MLA_H * V_D), head_g[None], w_out,
      x.reshape(m, d), ng_post[None], gate[:, None])
    return out.reshape(bsz, length, d)


HALO = 16


def _ffn_kernel(x_ref, xprev_ref, xnext_ref, ng_ref, mod_ref, wg_ref, wv_ref, cg_ref, cv_ref, wd_ref,
                o_ref, hbuf, acc, *, blocks_per_seq):
    i = pl.program_id(0)
    j = pl.program_id(1)
    tm = x_ref.shape[0]

    def norm_mod(xv):
        y = xv * lax.rsqrt(jnp.mean(xv * xv, axis=-1, keepdims=True) + EPS) * ng_ref[0:1, :]
        return (y * (1.0 + mod_ref[0, 1:2, :]) + mod_ref[0, 0:1, :]).astype(BF16)

    @pl.when(j == 0)
    def _():
        first = (i % blocks_per_seq) == 0
        last = (i % blocks_per_seq) == blocks_per_seq - 1
        hp = norm_mod(xprev_ref[...])
        hn = norm_mod(xnext_ref[...])
        hbuf[0:HALO] = jnp.where(first, jnp.zeros_like(hp), hp)
        hbuf[HALO:HALO + tm] = norm_mod(x_ref[...])
        hbuf[HALO + tm:] = jnp.where(last, jnp.zeros_like(hn), hn)
        acc[...] = jnp.zeros_like(acc)

    hb = hbuf[...]

    def conv(up, c_ref):
        prev = pltpu.roll(up, 1, axis=0)[HALO:HALO + tm]
        nxt = pltpu.roll(up, tm + 2 * HALO - 1, axis=0)[HALO:HALO + tm]
        return c_ref[0:1, :] * prev + c_ref[1:2, :] * up[HALO:HALO + tm] + c_ref[2:3, :] * nxt

    gate = conv(jnp.dot(hb, wg_ref[...], preferred_element_type=F32), cg_ref)
    val = conv(jnp.dot(hb, wv_ref[...], preferred_element_type=F32), cv_ref)
    act = (jax.nn.gelu(gate, approximate=True) * val).astype(BF16)
    acc[...] += jnp.dot(act, wd_ref[...], preferred_element_type=F32)

    @pl.when(j == pl.num_programs(1) - 1)
    def _():
        y = acc[...]
        y = y * lax.rsqrt(jnp.mean(y * y, axis=-1, keepdims=True) + EPS) * ng_ref[1:2, :]
        o_ref[...] = x_ref[...] + mod_ref[0, 2:3, :] * y


def _ffn_sublayer(x, ng_pre, ng_post, shift, scale, gate, w_up, w_conv, w_down):
    bsz, length, d = x.shape
    m = bsz * length
    tm = _pick(length, (512, 256, 128))
    tf = _pick(D_FF, (512, 256, 128))
    nf = D_FF // tf
    bps = length // tm
    hb = tm // HALO
    nrow = m // HALO
    per_batch = shift.shape[0] > 1
    ng = jnp.stack([ng_pre, ng_post])
    mod = jnp.stack([shift, scale, gate], axis=1)
    out = pl.pallas_call(
        functools.partial(_ffn_kernel, blocks_per_seq=bps),
        grid=(m // tm, nf),
        in_specs=[pl.BlockSpec((tm, d), lambda i, j: (i, 0)),
                  pl.BlockSpec((HALO, d), lambda i, j: (jnp.maximum(i * hb - 1, 0), 0)),
                  pl.BlockSpec((HALO, d), lambda i, j: (jnp.minimum((i + 1) * hb, nrow - 1), 0)),
                  pl.BlockSpec((2, d), lambda i, j: (0, 0)),
                  pl.BlockSpec((1, 3, d), (lambda i, j: (i // bps, 0, 0)) if per_batch else (lambda i, j: (0, 0, 0))),
                  pl.BlockSpec((d, tf), lambda i, j: (0, j)),
                  pl.BlockSpec((d, tf), lambda i, j: (0, j + nf)),
                  pl.BlockSpec((FFN_CONV, tf), lambda i, j: (0, j)),
                  pl.BlockSpec((FFN_CONV, tf), lambda i, j: (0, j + nf)),
                  pl.BlockSpec((tf, d), lambda i, j: (j, 0))],
        out_specs=pl.BlockSpec((tm, d), lambda i, j: (i, 0)),
        out_shape=jax.ShapeDtypeStruct((m, d), F32),
        scratch_shapes=[pltpu.VMEM((tm + 2 * HALO, d), BF16), pltpu.VMEM((tm, d), F32)],
        compiler_params=pltpu.CompilerParams(
            dimension_semantics=("parallel", "arbitrary"), vmem_limit_bytes=VMEM_LIMIT),
        name="ffn",
    )(x.reshape(m, d), x.reshape(m, d), x.reshape(m, d), ng, mod, w_up, w_up, w_conv, w_conv, w_down)
    return out.reshape(bsz, length, d)


def kernel(x, c, ctx, c_ctx, ada_w, ada_b, norm_g, w_in, hy_conv, hy_w1, hy_b1, hy_w2, hy_b2, hy_w3, hy_bias, lru_conv, lru_wa, lru_ba, lru_wx, lru_bx, lru_lam, mla_gq, mla_gkv, mla_wuq, mla_wukv, head_g, w_out, ffn_up, ffn_conv, ffn_down):
    depth = ada_w.shape[0]
    bsz, seq, _ = x.shape
    ctx_len = ctx.shape[1]
    rope_tabs = _rope_lane_tables(_rope_tables(seq))
    dft_lat, dft_ctx = _dft_tables(seq), _dft_tables(ctx_len)
    xc = ctx
    for l in range(depth):
        need_ctx = l < depth - 1
        ng = norm_g[l]
        cc = jnp.concatenate([jax.nn.silu(c), jax.nn.silu(c_ctx)[None]], axis=0)
        mod_all = _mm(cc, ada_w[l]) + ada_b[l]
        sh1, sc1, g1, sh2, sc2, g2 = jnp.split(mod_all[:bsz, None, :], N_MOD, axis=-1)
        csh1, csc1, cg1, csh2, csc2, cg2 = jnp.split(mod_all[bsz], N_MOD, axis=-1)
        hy_f = (hy_w1[l], hy_b1[l], hy_w2[l], hy_b2[l], hy_w3[l])
        lru_p = (lru_conv[l], lru_wa[l], lru_ba[l], lru_wx[l], lru_bx[l], lru_lam[l])
        w_in_b = jnp.pad(w_in[l], ((0, 0), (0, NOPE_D - ROPE_D))).astype(BF16)
        wuq_pad = _pad_heads_q(mla_wuq[l]).astype(BF16)
        wukv_b = mla_wukv[l].astype(BF16)
        w_out_b = w_out[l].astype(BF16)
        up_b = ffn_up[l].astype(BF16)
        down_b = ffn_down[l].astype(BF16)

        def seq3(t, length):
            return None if t is None else t.reshape(bsz, length, t.shape[-1])

        u = _proj_in(x, ng[0], sh1[:, 0], sc1[:, 0], w_in_b)
        uc = _proj_in(xc, ng[0], csh1[None], csc1[None], w_in_b)
        u3, uc3 = seq3(u, seq), seq3(uc, ctx_len)

        hc_sum, hc_end = _rglru(uc3[..., OFF_LRU_X:OFF_MLA_KV], *lru_p, jnp.zeros((bsz, 2, LRU_C), F32))
        q_c, kv_c, kr_c = [seq3(t, ctx_len) for t in
                           _proj_qkv(uc, ctx_len, mla_gq[l], wuq_pad, mla_gkv[l], wukv_b, None, with_q=need_ctx)]

        y_hy = _hyena_mix(u3, hy_conv[l], _filter_spectra(seq, *hy_f, dft_lat), hy_bias[l], dft_lat)
        h_sum, _ = _rglru(u3[..., OFF_LRU_X:OFF_MLA_KV], *lru_p, hc_end)
        q, kv, kr = [seq3(t, seq) for t in _proj_qkv(u, seq, mla_gq[l], wuq_pad, mla_gkv[l], wukv_b, rope_tabs)]
        y_mla = _attention(q, [(kv_c, kr_c), (kv, kr)])
        x = _mix_out(x, y_hy, h_sum, u, y_mla, head_g[l], w_out_b, ng[1], g1[:, 0])

        if need_ctx:
            yc_hy = _hyena_mix(uc3, hy_conv[l], _filter_spectra(ctx_len, *hy_f, dft_ctx), hy_bias[l], dft_ctx)
            yc_mla = _attention(q_c, [(kv_c, kr_c)])
            xc = _mix_out(xc, yc_hy, hc_sum, uc, yc_mla, head_g[l], w_out_b, ng[1], cg1[None])

        x = _ffn_sublayer(x, ng[2], ng[3], sh2[:, 0], sc2[:, 0], g2[:, 0], up_b, ffn_conv[l], down_b)
        if need_ctx:
            xc = _ffn_sublayer(xc, ng[2], ng[3], csh2[None], csc2[None], cg2[None], up_b, ffn_conv[l], down_b)
    return x
```

```python
import functools
import math

import jax
import jax.numpy as jnp
from jax import lax
from jax.experimental import pallas as pl
from jax.experimental.pallas import tpu as pltpu

F32 = jnp.float32
BF16 = jnp.bfloat16

D_MODEL = 2048
GRID_W = 64
HEAD_W = 128
HY_C = 512
HY_ORDER = 2
HY_SHORT = 3
HY_BANDS = 16
HY_FAST_DECAY = 0.3
HY_SLOW_DECAY = 1.5
HY_DECAY_TARGET = 1e-2
LRU_C = 512
LRU_BLOCKS = 4
LRU_BW = LRU_C // LRU_BLOCKS
LRU_CONV = 4
RG_C = 8.0
MLA_H = 8
Q_RANK = 512
KV_RANK = 256
NOPE_D = 128
ROPE_D = 64
V_D = 128
ROPE_BASE = 10000.0
MLA_SCALE = (NOPE_D + ROPE_D) ** -0.5
D_MIX = HY_C + LRU_C + MLA_H * V_D
N_MIX_HEADS = D_MIX // HEAD_W
D_FF = 5632
FFN_CONV = 3
N_MOD = 6
EPS = 1e-6

OFF_HY = 0
OFF_LRU_G = OFF_HY + (HY_ORDER + 1) * HY_C
OFF_MLA_Q = OFF_LRU_G + LRU_C
OFF_LRU_X = OFF_MLA_Q + Q_RANK
OFF_MLA_KV = OFF_LRU_X + LRU_C
OFF_MLA_KR = OFF_MLA_KV + KV_RANK
D_IN = OFF_MLA_KR + ROPE_D

HEAD_QK = 2 * NOPE_D
VMEM_LIMIT = 48 * 1024 * 1024


def _pick(n, prefs):
    for p in prefs:
        if n % p == 0:
            return p
    return n


def _adaln_kernel(c_ref, w_ref, b_ref, o_ref):
    c = c_ref[...]
    a = (c * jax.nn.sigmoid(c)).astype(BF16)
    o_ref[...] = jnp.dot(a, w_ref[...].astype(BF16), preferred_element_type=F32) + b_ref[...]


def _adaln(cond, ada_w, ada_b):
    depth, d, n = ada_w.shape
    rows = cond.shape[0]
    rp = -(-rows // SUBLANES) * SUBLANES
    cond = jnp.pad(cond, ((0, rp - rows), (0, 0)))
    tn = 1024
    out = pl.pallas_call(
        _adaln_kernel,
        grid=(depth, n // tn),
        in_specs=[pl.BlockSpec((rp, d), lambda l, j: (0, 0)),
                  pl.BlockSpec((None, d, tn), lambda l, j: (l, 0, j)),
                  pl.BlockSpec((None, 1, tn), lambda l, j: (l, 0, j))],
        out_specs=pl.BlockSpec((None, rp, tn), lambda l, j: (l, 0, j)),
        out_shape=jax.ShapeDtypeStruct((depth, rp, n), F32),
        compiler_params=pltpu.CompilerParams(
            dimension_semantics=("parallel", "parallel"), vmem_limit_bytes=VMEM_LIMIT),
        name="adaln",
    )(cond, ada_w, ada_b[:, None])
    return out[:, :rows]


Q_SPLIT = 2

def _attn_kernel(*refs, n_seg):
    q_ref = refs[0]
    seg_refs = refs[1:1 + 3 * n_seg]
    o_ref = refs[1 + 3 * n_seg]
    k_scr, v_scr = refs[2 + 3 * n_seg:]

    @pl.when(pl.program_id(2) == 0)
    def _():
        off = 0
        for s in range(n_seg):
            kn_ref, v_ref, kr_ref = seg_refs[3 * s:3 * s + 3]
            n = kn_ref.shape[1]
            k_scr[off:off + n, 0:NOPE_D] = kn_ref[0]
            k_scr[off:off + n, NOPE_D:HEAD_QK] = kr_ref[0]
            v_scr[off:off + n, :] = v_ref[0]
            off += n

    sub = q_ref.shape[1] // Q_SPLIT
    for part in range(Q_SPLIT):
        rows = slice(part * sub, (part + 1) * sub)
        s = lax.dot_general(q_ref[0, rows], k_scr[...], (((1,), (1,)), ((), ())), preferred_element_type=F32)
        m = jnp.max(s, axis=-1, keepdims=True)
        p = jnp.exp(s - m)
        l = jnp.sum(p, axis=-1, keepdims=True)
        o = jnp.dot(p.astype(BF16), v_scr[...], preferred_element_type=F32)
        o_ref[0, rows] = o / l


def _attention(q, segs):
    bsz, lq, _ = q.shape
    tq = _pick(lq, (512, 256, 128))
    lk = sum(kv.shape[1] for kv, _ in segs)
    in_specs = [pl.BlockSpec((1, tq, HEAD_QK), lambda b, h, i: (b, i, h))]
    args = [q]
    for kv, kr in segs:
        n = kv.shape[1]
        in_specs += [pl.BlockSpec((1, n, NOPE_D), lambda b, h, i: (b, 0, 2 * h)),
                     pl.BlockSpec((1, n, V_D), lambda b, h, i: (b, 0, 2 * h + 1)),
                     pl.BlockSpec((1, n, NOPE_D), lambda b, h, i: (b, 0, 0))]
        args += [kv, kv, kr]
    return pl.pallas_call(
        functools.partial(_attn_kernel, n_seg=len(segs)),
        grid=(bsz, MLA_H, lq // tq),
        in_specs=in_specs,
        out_specs=pl.BlockSpec((1, tq, V_D), lambda b, h, i: (b, i, h)),
        out_shape=jax.ShapeDtypeStruct((bsz, lq, MLA_H * V_D), F32),
        scratch_shapes=[pltpu.VMEM((lk, HEAD_QK), BF16), pltpu.VMEM((lk, V_D), BF16)],
        compiler_params=pltpu.CompilerParams(
            dimension_semantics=("parallel", "parallel", "arbitrary"), vmem_limit_bytes=VMEM_LIMIT),
        name="attn",
    )(*args)


LANES = 128
PGROUP = 2
STAGE_UNROLL = 8


def _fft_dims(length):
    n = 2 * length
    bn = 128 if length >= 1024 else 32
    return n // bn, bn, bn + SUBLANES


def _dft_tables(length):
    a_n, bn, _ = _fft_dims(length)
    n, half = a_n * bn, a_n // 2
    p = jnp.arange(a_n, dtype=jnp.int32)
    b = jnp.arange(bn, dtype=jnp.int32)
    nn = bn * p[None, None, :] + b[:, None, None]
    ang = (2.0 * math.pi / n) * ((p[None, :, None] * nn) % n).astype(F32)
    cr, ci = jnp.cos(ang), -jnp.sin(ang)
    m1 = jnp.concatenate([jnp.concatenate([cr[:, :, :half], -ci[:, :, :half]], axis=2),
                          jnp.concatenate([ci[:, :, :half], cr[:, :, :half]], axis=2)], axis=1)
    m1f = jnp.concatenate([cr, ci], axis=1)
    crt, cit = jnp.swapaxes(cr, 1, 2)[:, :half], jnp.swapaxes(ci, 1, 2)[:, :half]
    m4 = jnp.concatenate([jnp.concatenate([crt, cit], axis=2),
                          jnp.concatenate([-cit, crt], axis=2)], axis=1) / n
    ang2 = (2.0 * math.pi / bn) * ((b[:, None] * b[None, :]) % bn).astype(F32)
    wr, wi = jnp.cos(ang2), -jnp.sin(ang2)
    m2 = jnp.concatenate([jnp.concatenate([wr, -wi], axis=1), jnp.concatenate([wi, wr], axis=1)], axis=0)
    m3 = jnp.concatenate([jnp.concatenate([wr, wi], axis=1), jnp.concatenate([-wi, wr], axis=1)], axis=0)
    return tuple(t.astype(BF16) for t in (m1, m1f, m2, m3, m4))


def _stage1(src, rows_per_part, parts, m_ref, t2, a_n, bn, pitch):
    def body(b, carry):
        rhs = jnp.concatenate([src[pl.ds(part * rows_per_part * pitch + b, rows_per_part, stride=pitch), :]
                               for part in range(parts)], axis=0)
        t = jnp.dot(m_ref[b], rhs.astype(BF16), preferred_element_type=F32)
        t2[pl.ds(b, a_n, stride=pitch), :] = t[:a_n]
        t2[pl.ds(a_n * pitch + b, a_n, stride=pitch), :] = t[a_n:]
        return carry
    lax.fori_loop(0, bn, body, 0, unroll=STAGE_UNROLL)


def _stage2_rhs(t2, g, a_n, bn, pitch):
    cols = []
    for k in range(PGROUP):
        r0 = pl.multiple_of((g * PGROUP + k) * pitch, SUBLANES)
        cols.append(jnp.concatenate([t2[pl.ds(r0, bn), :], t2[pl.ds(a_n * pitch + r0, bn), :]], axis=0))
    return jnp.concatenate(cols, axis=1).astype(BF16)


def _fftconv_kernel(z_ref, spec_ref, m1_ref, m2_ref, m3_ref, m4_ref, y_ref, zb, t2, *, length):
    a_n, bn, pitch = _fft_dims(length)
    half = a_n // 2
    for bi in range(2):
        for a in range(half):
            zb[(bi * half + a) * pitch:(bi * half + a) * pitch + bn, :] = z_ref[bi, a * bn:(a + 1) * bn, :]
    _stage1(zb, half, 2, m1_ref, t2, a_n, bn, pitch)

    def mid(g, carry):
        x = jnp.dot(m2_ref[...], _stage2_rhs(t2, g, a_n, bn, pitch), preferred_element_type=F32)
        k = spec_ref[0, 0, g]
        xr, xi, kr, ki = x[:bn], x[bn:], k[:bn], k[bn:]
        y = jnp.concatenate([xr * kr - xi * ki, xr * ki + xi * kr], axis=0).astype(BF16)
        u = jnp.dot(m3_ref[...], y, preferred_element_type=F32)
        for kk in range(PGROUP):
            r0 = pl.multiple_of((g * PGROUP + kk) * pitch, SUBLANES)
            t2[pl.ds(r0, bn), :] = u[:bn, kk * LANES:(kk + 1) * LANES]
            t2[pl.ds(a_n * pitch + r0, bn), :] = u[bn:, kk * LANES:(kk + 1) * LANES]
        return carry
    lax.fori_loop(0, a_n // PGROUP, mid, 0, unroll=2)

    def last(b, carry):
        rhs = jnp.concatenate([t2[pl.ds(b, a_n, stride=pitch), :],
                               t2[pl.ds(a_n * pitch + b, a_n, stride=pitch), :]], axis=0)
        y = jnp.dot(m4_ref[b], rhs.astype(BF16), preferred_element_type=F32)
        zb[pl.ds(b, half, stride=pitch), :] = y[:half]
        zb[pl.ds(half * pitch + b, half, stride=pitch), :] = y[half:]
        return carry
    lax.fori_loop(0, bn, last, 0, unroll=STAGE_UNROLL)
    for bi in range(2):
        for a in range(half):
            y_ref[bi, a * bn:(a + 1) * bn, :] = zb[(bi * half + a) * pitch:(bi * half + a) * pitch + bn, :]


def _fftconv(z, spec, tables):
    bsz, length, chans = z.shape
    assert bsz == 2, "the two batch rows are packed as one complex signal"
    a_n, bn, pitch = _fft_dims(length)
    m1, _, m2, m3, m4 = tables
    const3 = lambda c: (0, 0, 0)
    const2 = lambda c: (0, 0)
    once = pl.Buffered(1)
    return pl.pallas_call(
        functools.partial(_fftconv_kernel, length=length),
        grid=(chans // LANES,),
        in_specs=[pl.BlockSpec((2, length, LANES), lambda c: (0, 0, c)),
                  pl.BlockSpec((1, 1, a_n // PGROUP, 2 * bn, PGROUP * LANES), lambda c: (0, c, 0, 0, 0),
                               pipeline_mode=once),
                  pl.BlockSpec(m1.shape, const3, pipeline_mode=once),
                  pl.BlockSpec(m2.shape, const2, pipeline_mode=once),
                  pl.BlockSpec(m3.shape, const2, pipeline_mode=once),
                  pl.BlockSpec(m4.shape, const3, pipeline_mode=once)],
        out_specs=pl.BlockSpec((2, length, LANES), lambda c: (0, 0, c)),
        out_shape=jax.ShapeDtypeStruct(z.shape, F32),
        scratch_shapes=[pltpu.VMEM((a_n * pitch, LANES), F32), pltpu.VMEM((2 * a_n * pitch, LANES), F32)],
        compiler_params=pltpu.CompilerParams(dimension_semantics=("parallel",), vmem_limit_bytes=VMEM_LIMIT),
        name="fftconv",
    )(z, spec[None], m1, m2, m3, m4)


def _split_bf16(x):
    hi = x.astype(BF16)
    return hi, (x - hi.astype(F32)).astype(BF16)


def _dot_split(a, b):
    ah, al = _split_bf16(a)
    bh, bl = _split_bf16(b)
    d = functools.partial(jnp.dot, preferred_element_type=F32)
    return d(ah, bh) + (d(al, bh) + d(ah, bl))


def _filtspec_kernel(hf_ref, hb_ref, w3f_ref, w3b_ref, dec_ref, m1_ref, m2_ref, spec_ref, kb, t2, *, length):
    a_n, bn, pitch = _fft_dims(length)
    half = a_n // 2
    row = lax.broadcasted_iota(jnp.int32, (length, 1), 0).astype(F32)
    delta = dec_ref[...]
    kf = _dot_split(hf_ref[...], w3f_ref[...]) * jnp.exp(-(row / length) * delta)
    kbk = _dot_split(hb_ref[...], w3b_ref[...]) * jnp.exp(-((length - row) / length) * delta)
    kbk = jnp.where(row == 0.0, 0.0, kbk)
    norm = jnp.sum(jnp.abs(kf), axis=0, keepdims=True) + jnp.sum(jnp.abs(kbk), axis=0, keepdims=True)
    kf, kbk = kf / norm, kbk / norm
    for a in range(half):
        kb[a * pitch:a * pitch + bn, :] = kf[a * bn:(a + 1) * bn]
        kb[(half + a) * pitch:(half + a) * pitch + bn, :] = kbk[a * bn:(a + 1) * bn]
    _stage1(kb, a_n, 1, m1_ref, t2, a_n, bn, pitch)

    def mid(g, carry):
        spec_ref[0, 0, g] = jnp.dot(m2_ref[...], _stage2_rhs(t2, g, a_n, bn, pitch), preferred_element_type=F32)
        return carry
    lax.fori_loop(0, a_n // PGROUP, mid, 0, unroll=2)


def _filter_spectra(length, w1, b1, w2, b2, w3, tables):
    a_n, bn, pitch = _fft_dims(length)
    hp = lax.Precision.HIGHEST
    t_f = jnp.arange(length, dtype=F32) / length
    t_b = (length - jnp.arange(length, dtype=F32)) / length
    band = jnp.arange(1, HY_BANDS + 1, dtype=F32)

    def hidden(t):
        ang = 2.0 * math.pi * t[:, None] * band
        feats = jnp.concatenate([t[:, None], jnp.sin(ang), jnp.cos(ang)], axis=-1)
        hid = jnp.sin(jnp.dot(feats, w1, precision=hp) + b1)
        return jnp.sin(jnp.dot(hid, w2, precision=hp) + b2)

    deltas = jnp.abs(jnp.linspace(math.log(HY_DECAY_TARGET) / HY_SLOW_DECAY,
                                  math.log(HY_DECAY_TARGET) / HY_FAST_DECAY, HY_C, dtype=F32))[None]
    _, m1f, m2, _, _ = tables
    nct = HY_C // LANES
    hid_f, hid_b = hidden(t_f), hidden(t_b)
    hidden_w = hid_f.shape[1]
    once = pl.Buffered(1)
    return pl.pallas_call(
        functools.partial(_filtspec_kernel, length=length),
        grid=(HY_ORDER, nct),
        in_specs=[pl.BlockSpec((length, hidden_w), lambda o, c: (0, 0), pipeline_mode=once),
                  pl.BlockSpec((length, hidden_w), lambda o, c: (0, 0), pipeline_mode=once),
                  pl.BlockSpec((hidden_w, LANES), lambda o, c: (0, o * 2 * nct + c)),
                  pl.BlockSpec((hidden_w, LANES), lambda o, c: (0, o * 2 * nct + nct + c)),
                  pl.BlockSpec((1, LANES), lambda o, c: (0, c)),
                  pl.BlockSpec(m1f.shape, lambda o, c: (0, 0, 0), pipeline_mode=once),
                  pl.BlockSpec(m2.shape, lambda o, c: (0, 0), pipeline_mode=once)],
        out_specs=pl.BlockSpec((1, 1, a_n // PGROUP, 2 * bn, PGROUP * LANES), lambda o, c: (o, c, 0, 0, 0)),
        out_shape=jax.ShapeDtypeStruct((HY_ORDER, nct, a_n // PGROUP, 2 * bn, PGROUP * LANES), F32),
        scratch_shapes=[pltpu.VMEM((a_n * pitch, LANES), F32), pltpu.VMEM((2 * a_n * pitch, LANES), F32)],
        compiler_params=pltpu.CompilerParams(
            dimension_semantics=("parallel", "parallel"), vmem_limit_bytes=VMEM_LIMIT),
        name="filtspec",
    )(hid_f, hid_b, w3, w3, deltas, m1f, m2)


def _hygate_kernel(p_ref, w_ref, *rest, first):
    x = p_ref[0]
    n = x.shape[0]
    row = lax.broadcasted_iota(jnp.int32, x.shape, 0)
    prev = jnp.where(row == 0, 0.0, pltpu.roll(x, 1, axis=0))
    nxt = jnp.where(row == n - 1, 0.0, pltpu.roll(x, n - 1, axis=0))
    c = w_ref[0:1, :] * prev + w_ref[1:2, :] * x + w_ref[2:3, :] * nxt
    if first:
        (o_ref,) = rest
        o_ref[0] = c
    else:
        y_ref, z_ref, b_ref, o_ref = rest
        o_ref[0] = c * (y_ref[0] + z_ref[0] * b_ref[...])


def _hygate(u, part, conv_w, y=None, z=None, bias=None):
    bsz, length, _ = u.shape
    nct = HY_C // LANES
    blk = lambda b, c: (b, 0, c)
    in_specs = [pl.BlockSpec((1, length, LANES), lambda b, c: (b, 0, part * nct + c)),
                pl.BlockSpec((HY_SHORT, LANES), lambda b, c: (0, part * nct + c))]
    args = [u, conv_w]
    if y is not None:
        in_specs += [pl.BlockSpec((1, length, LANES), blk), pl.BlockSpec((1, length, LANES), blk),
                     pl.BlockSpec((1, LANES), lambda b, c: (0, c))]
        args += [y, z, bias]
    return pl.pallas_call(
        functools.partial(_hygate_kernel, first=y is None),
        grid=(bsz, nct),
        in_specs=in_specs,
        out_specs=pl.BlockSpec((1, length, LANES), blk),
        out_shape=jax.ShapeDtypeStruct((bsz, length, HY_C), F32),
        compiler_params=pltpu.CompilerParams(
            dimension_semantics=("parallel", "parallel"), vmem_limit_bytes=VMEM_LIMIT),
        name="hygate",
    )(*args)


def _hyena_mix(u, conv_w, spec, bias, tables):
    z = _hygate(u, 0, conv_w)
    for o in range(HY_ORDER):
        y = _fftconv(z, spec[o], tables)
        z = _hygate(u, o + 1, conv_w, y, z, bias[o:o + 1])
    return z


SUBLANES = 8
PAD = LRU_CONV - 1


def _lru_kernel(x_ref, wg_ref, pv_ref, h0_ref, o_ref, end_ref, xp, a_scr, b_scr, *, steps, chunk):
    sub = lax.broadcasted_iota(jnp.int32, (SUBLANES, LRU_BW), 0)
    xp[PAD:PAD + steps] = x_ref[0]
    for k in range(PAD):
        tail = pltpu.roll(x_ref[0, steps - PAD + k], 1, axis=0)
        xp[k] = jnp.where(sub == 0, 0.0, tail)
        head = pltpu.roll(x_ref[0, k], SUBLANES - 1, axis=0)
        xp[PAD + steps + k] = jnp.where(sub == SUBLANES - 1, 0.0, head)

    n_chunks = steps // chunk
    for d in range(2):
        ba = pv_ref[d, 0:1, :]
        bx = pv_ref[d, 1:2, :]
        lam = pv_ref[d, 2:3, :]
        neg_c_sp = -RG_C * jax.nn.softplus(-lam)
        taps = [pv_ref[d, 3 + k:4 + k, :].reshape(1, 1, LRU_BW) for k in range(LRU_CONV)]
        w_gate = wg_ref[d, 0]
        base = 0 if d == 0 else PAD

        def gates(c, carry):
            j0 = pl.multiple_of(c * chunk, chunk)
            xc = taps[0] * xp[pl.ds(j0 + base, chunk)]
            for k in range(1, LRU_CONV):
                xc = xc + taps[k] * xp[pl.ds(j0 + base + k, chunk)]
            xc2 = xc.reshape(chunk * SUBLANES, LRU_BW)
            g = jnp.dot(xc2.astype(BF16), w_gate, preferred_element_type=F32)
            r = jax.nn.sigmoid(g[:, :LRU_BW] + ba)
            i = jax.nn.sigmoid(g[:, LRU_BW:] + bx)
            log_a = neg_c_sp * r
            a = jnp.exp(log_a)
            t = jnp.tanh(log_a)
            b = jnp.sqrt(-2.0 * t / (1.0 - t)) * (i * xc2)
            a_scr[pl.ds(j0, chunk)] = a.reshape(chunk, SUBLANES, LRU_BW)
            b_scr[pl.ds(j0, chunk)] = b.reshape(chunk, SUBLANES, LRU_BW)
            return carry

        lax.fori_loop(0, n_chunks, gates, 0)

        def step(t, carry):
            h, p = carry
            j = t if d == 0 else steps - 1 - t
            a = a_scr[j]
            h = a * h + b_scr[j]
            p = a * p
            b_scr[j] = h
            a_scr[j] = p
            return h, p

        h_loc, p_loc = lax.fori_loop(
            0, steps, step,
            (jnp.zeros((SUBLANES, LRU_BW), F32), jnp.ones((SUBLANES, LRU_BW), F32)), unroll=8)

        carry = h0_ref[0, d:d + 1, :]
        cin = jnp.zeros((SUBLANES, LRU_BW), F32)
        order = range(SUBLANES) if d == 0 else range(SUBLANES - 1, -1, -1)
        for s in order:
            cin = jnp.where(sub == s, jnp.broadcast_to(carry, (SUBLANES, LRU_BW)), cin)
            carry = p_loc[s:s + 1, :] * carry + h_loc[s:s + 1, :]
        end_ref[0, d:d + 1, :] = carry

        def fix(c, carry_):
            j0 = pl.multiple_of(c * chunk, chunk)
            h = b_scr[pl.ds(j0, chunk)] + a_scr[pl.ds(j0, chunk)] * cin[None]
            if d == 0:
                o_ref[0, pl.ds(j0, chunk)] = h
            else:
                o_ref[0, pl.ds(j0, chunk)] += h
            return carry_

        lax.fori_loop(0, n_chunks, fix, 0)


def _rglru(xr, lconv, lwa, lba, lwx, lbx, llam, h0):
    bsz, length, _ = xr.shape
    steps = length // SUBLANES
    chunk = min(32, steps)
    xs = xr.reshape(bsz, SUBLANES, steps, LRU_C).transpose(0, 2, 1, 3)
    wg = jnp.concatenate([lwa, lwx], axis=-1).astype(BF16)
    pv = jnp.concatenate([lba[:, None], lbx[:, None], llam[:, None], lconv,
                          jnp.zeros((2, 1, LRU_C), F32)], axis=1)
    o, ends = pl.pallas_call(
        functools.partial(_lru_kernel, steps=steps, chunk=chunk),
        grid=(bsz, LRU_BLOCKS),
        in_specs=[pl.BlockSpec((1, steps, SUBLANES, LRU_BW), lambda b, n: (b, 0, 0, n)),
                  pl.BlockSpec((2, 1, LRU_BW, 2 * LRU_BW), lambda b, n: (0, n, 0, 0)),
                  pl.BlockSpec((2, SUBLANES, LRU_BW), lambda b, n: (0, 0, n)),
                  pl.BlockSpec((1, 2, LRU_BW), lambda b, n: (b, 0, n))],
        out_specs=[pl.BlockSpec((1, steps, SUBLANES, LRU_BW), lambda b, n: (b, 0, 0, n)),
                   pl.BlockSpec((1, 2, LRU_BW), lambda b, n: (b, 0, n))],
        out_shape=[jax.ShapeDtypeStruct((bsz, steps, SUBLANES, LRU_C), F32),
                   jax.ShapeDtypeStruct((bsz, 2, LRU_C), F32)],
        scratch_shapes=[pltpu.VMEM((steps + 2 * PAD, SUBLANES, LRU_BW), F32),
                        pltpu.VMEM((steps, SUBLANES, LRU_BW), F32),
                        pltpu.VMEM((steps, SUBLANES, LRU_BW), F32)],
        compiler_params=pltpu.CompilerParams(
            dimension_semantics=("parallel", "parallel"), vmem_limit_bytes=VMEM_LIMIT),
        name="rglru",
    )(xs, wg, pv, h0)
    return o.transpose(0, 2, 1, 3).reshape(bsz, length, LRU_C), ends


def _rope_tables(length):
    rows = length // GRID_W
    row = jnp.repeat(jnp.arange(rows, dtype=F32), GRID_W)
    col = jnp.tile(jnp.arange(GRID_W, dtype=F32), rows)
    n_freq = ROPE_D // 4
    inv = ROPE_BASE ** (-jnp.arange(n_freq, dtype=F32) / n_freq)
    ang = jnp.concatenate([row[:, None] * inv, col[:, None] * inv], axis=-1)
    return jnp.cos(ang), jnp.sin(ang)


def _rope_lane_tables(rope):
    cos, sin = rope
    z = jnp.zeros_like(cos)
    return (jnp.concatenate([cos, cos, z, z], axis=1), jnp.concatenate([-sin, z, z, z], axis=1),
            jnp.concatenate([z, sin, z, z], axis=1))


def _pad_heads_q(wuq):
    w = wuq.reshape(Q_RANK, MLA_H, NOPE_D + ROPE_D)
    w = jnp.pad(w, ((0, 0), (0, 0), (0, HEAD_QK - NOPE_D - ROPE_D)))
    return w.reshape(Q_RANK, MLA_H * HEAD_QK)


def _rms(xv, g):
    return xv * lax.rsqrt(jnp.mean(xv * xv, axis=-1, keepdims=True) + EPS) * g


def _proj_in_kernel(x_ref, g_ref, mod_ref, w_ref, o_ref, hbuf):
    @pl.when(pl.program_id(1) == 0)
    def _():
        y = _rms(x_ref[...], g_ref[...])
        hbuf[...] = (y * (1.0 + mod_ref[0, 1:2, :]) + mod_ref[0, 0:1, :]).astype(BF16)

    o_ref[...] = jnp.dot(hbuf[...], w_ref[...], preferred_element_type=F32)


def _proj_in(x, g, shift, scale, w, layer):
    bsz, length, d = x.shape
    m, n = bsz * length, w.shape[2]
    tm = _pick(length, (512, 256, 128))
    tn = max(t for t in range(128, 1153, 128) if n % t == 0)
    bps = length // tm
    mod = jnp.stack([shift, scale], axis=1)
    mod_idx = (lambda i, j: (i // bps, 0, 0)) if shift.shape[0] > 1 else (lambda i, j: (0, 0, 0))
    return pl.pallas_call(
        _proj_in_kernel,
        grid=(m // tm, n // tn),
        in_specs=[pl.BlockSpec((tm, d), lambda i, j: (i, 0)),
                  pl.BlockSpec((1, d), lambda i, j: (0, 0)),
                  pl.BlockSpec((1, 2, d), mod_idx),
                  pl.BlockSpec((None, d, tn), lambda i, j: (layer, 0, j))],
        out_specs=pl.BlockSpec((tm, tn), lambda i, j: (i, j)),
        out_shape=jax.ShapeDtypeStruct((m, n), F32),
        scratch_shapes=[pltpu.VMEM((tm, d), BF16)],
        compiler_params=pltpu.CompilerParams(
            dimension_semantics=("parallel", "arbitrary"), vmem_limit_bytes=VMEM_LIMIT),
        name="proj_in",
    )(x.reshape(m, d), g[None], mod, w)


def _proj_qkv_kernel(*refs, rope, with_q):
    refs = list(refs)
    uq_ref = refs.pop(0) if with_q else None
    ukv_ref, ukr_ref = refs.pop(0), refs.pop(0)
    gq_ref, wq_ref = (refs.pop(0), refs.pop(0)) if with_q else (None, None)
    gkv_ref, wkv_ref = refs.pop(0), refs.pop(0)
    cos_ref, sa_ref, sb_ref = (refs.pop(0), refs.pop(0), refs.pop(0)) if rope else (None, None, None)
    q_ref = refs.pop(0) if with_q else None
    kv_ref, kr_ref = refs

    def rot(blk):
        if not rope:
            return blk
        return (blk * cos_ref[...] + pltpu.roll(blk, LANES - 32, axis=1) * sa_ref[...]
                + pltpu.roll(blk, 32, axis=1) * sb_ref[...])

    if with_q:
        nq = _rms(uq_ref[...], gq_ref[...]).astype(BF16)
        q = jnp.dot(nq, wq_ref[...], preferred_element_type=F32)
        for hd in range(MLA_H):
            lo = hd * HEAD_QK
            q_ref[:, lo:lo + NOPE_D] = (q[:, lo:lo + NOPE_D] * MLA_SCALE).astype(BF16)
            q_ref[:, lo + NOPE_D:lo + HEAD_QK] = (rot(q[:, lo + NOPE_D:lo + HEAD_QK]) * MLA_SCALE).astype(BF16)
    nkv = _rms(ukv_ref[...], gkv_ref[...]).astype(BF16)
    kv_ref[...] = jnp.dot(nkv, wkv_ref[...], preferred_element_type=F32).astype(BF16)
    kr_ref[...] = rot(ukr_ref[...]).astype(BF16)


def _proj_qkv(u, length, gq, wuq_pad, gkv, wukv, layer, rope_tabs, with_q=True):
    m = u.shape[0]
    tm = _pick(length, (512, 256, 128))
    bps = length // tm
    row = lambda i: (i, 0)
    const = lambda i: (0, 0)
    wsel = lambda i: (layer, 0, 0)
    in_specs, args = [], []
    if with_q:
        in_specs.append(pl.BlockSpec((tm, Q_RANK), lambda i: (i, OFF_MLA_Q // Q_RANK)))
        args.append(u)
    in_specs += [pl.BlockSpec((tm, KV_RANK), lambda i: (i, OFF_MLA_KV // KV_RANK)),
                 pl.BlockSpec((tm, LANES), lambda i: (i, OFF_MLA_KR // LANES))]
    args += [u, u]
    if with_q:
        in_specs += [pl.BlockSpec((1, Q_RANK), const), pl.BlockSpec((None,) + wuq_pad.shape[1:], wsel)]
        args += [gq[None], wuq_pad]
    in_specs += [pl.BlockSpec((1, KV_RANK), const), pl.BlockSpec((None,) + wukv.shape[1:], wsel)]
    args += [gkv[None], wukv]
    if rope_tabs is not None:
        in_specs += [pl.BlockSpec((tm, LANES), lambda i: (i % bps, 0))] * 3
        args += list(rope_tabs)
    out_specs = [pl.BlockSpec((tm, MLA_H * HEAD_QK), row), pl.BlockSpec((tm, LANES), row)]
    out_shape = [jax.ShapeDtypeStruct((m, MLA_H * HEAD_QK), BF16), jax.ShapeDtypeStruct((m, LANES), BF16)]
    if with_q:
        out_specs.insert(0, pl.BlockSpec((tm, MLA_H * HEAD_QK), row))
        out_shape.insert(0, jax.ShapeDtypeStruct((m, MLA_H * HEAD_QK), BF16))
    outs = pl.pallas_call(
        functools.partial(_proj_qkv_kernel, rope=rope_tabs is not None, with_q=with_q),
        grid=(m // tm,),
        in_specs=in_specs, out_specs=out_specs, out_shape=out_shape,
        compiler_params=pltpu.CompilerParams(dimension_semantics=("parallel",), vmem_limit_bytes=VMEM_LIMIT),
        name="proj_qkv",
    )(*args)
    return outs if with_q else [None] + list(outs)


def _mix_kernel(yhy_ref, hs_ref, gate_ref, ymla_ref, hg_ref, w_ref, x_ref, ng_ref, mod_ref, o_ref):
    y = jnp.concatenate([yhy_ref[...], jax.nn.gelu(gate_ref[...], approximate=True) * hs_ref[...],
                         ymla_ref[...]], axis=1)
    heads = [_rms(y[:, hd * HEAD_W:(hd + 1) * HEAD_W], hg_ref[:, hd * HEAD_W:(hd + 1) * HEAD_W]).astype(BF16)
             for hd in range(N_MIX_HEADS)]
    acc = jnp.dot(jnp.concatenate(heads, axis=1), w_ref[...], preferred_element_type=F32)
    o_ref[...] = x_ref[...] + mod_ref[0] * _rms(acc, ng_ref[...])


def _mix_out(x, y_hy, h_sum, u, y_mla, head_g, w_out, layer, ng_post, gate):
    bsz, length, d = x.shape
    m = bsz * length
    tm = _pick(length, (256, 128))
    bps = length // tm
    row = lambda i: (i, 0)
    const = lambda i: (0, 0)
    mod_idx = (lambda i: (i // bps, 0, 0)) if gate.shape[0] > 1 else (lambda i: (0, 0, 0))
    out = pl.pallas_call(
        _mix_kernel,
        grid=(m // tm,),
        in_specs=[pl.BlockSpec((tm, HY_C), row), pl.BlockSpec((tm, LRU_C), row),
                  pl.BlockSpec((tm, LRU_C), lambda i: (i, OFF_LRU_G // LRU_C)),
                  pl.BlockSpec((tm, MLA_H * V_D), row),
                  pl.BlockSpec((1, D_MIX), const),
                  pl.BlockSpec((None,) + w_out.shape[1:], lambda i: (layer, 0, 0), pipeline_mode=pl.Buffered(1)),
                  pl.BlockSpec((tm, d), row), pl.BlockSpec((1, d), const), pl.BlockSpec((1, 1, d), mod_idx)],
        out_specs=pl.BlockSpec((tm, d), row),
        out_shape=jax.ShapeDtypeStruct((m, d), F32),
        compiler_params=pltpu.CompilerParams(dimension_semantics=("parallel",), vmem_limit_bytes=VMEM_LIMIT),
        name="mix_out",
    )(y_hy.reshape(m, HY_C), h_sum.reshape(m, LRU_C), u, y_mla.reshape(m, MLA_H * V_D), head_g[None], w_out,
      x.reshape(m, d), ng_post[None], gate[:, None])
    return out.reshape(bsz, length, d)


HALO = 16


def _ffn_kernel(x_ref, xprev_ref, xnext_ref, ng_ref, mod_ref, wg_ref, wv_ref, cg_ref, cv_ref, wd_ref,
                o_ref, hbuf, acc, *, blocks_per_seq):
    i = pl.program_id(0)
    j = pl.program_id(1)
    tm = x_ref.shape[0]

    def norm_mod(xv):
        y = xv * lax.rsqrt(jnp.mean(xv * xv, axis=-1, keepdims=True) + EPS) * ng_ref[0:1, :]
        return (y * (1.0 + mod_ref[0, 1:2, :]) + mod_ref[0, 0:1, :]).astype(BF16)

    @pl.when(j == 0)
    def _():
        first = (i % blocks_per_seq) == 0
        last = (i % blocks_per_seq) == blocks_per_seq - 1
        hp = norm_mod(xprev_ref[...])
        hn = norm_mod(xnext_ref[...])
        hbuf[0:HALO] = jnp.where(first, jnp.zeros_like(hp), hp)
        hbuf[HALO:HALO + tm] = norm_mod(x_ref[...])
        hbuf[HALO + tm:] = jnp.where(last, jnp.zeros_like(hn), hn)
        acc[...] = jnp.zeros_like(acc)

    hb = hbuf[...]

    def conv(up, c_ref):
        prev = pltpu.roll(up, 1, axis=0)[HALO:HALO + tm]
        nxt = pltpu.roll(up, tm + 2 * HALO - 1, axis=0)[HALO:HALO + tm]
        return c_ref[0:1, :] * prev + c_ref[1:2, :] * up[HALO:HALO + tm] + c_ref[2:3, :] * nxt

    gate = conv(jnp.dot(hb, wg_ref[...], preferred_element_type=F32), cg_ref)
    val = conv(jnp.dot(hb, wv_ref[...], preferred_element_type=F32), cv_ref)
    act = (jax.nn.gelu(gate, approximate=True) * val).astype(BF16)
    acc[...] += jnp.dot(act, wd_ref[...], preferred_element_type=F32)

    @pl.when(j == pl.num_programs(1) - 1)
    def _():
        y = acc[...]
        y = y * lax.rsqrt(jnp.mean(y * y, axis=-1, keepdims=True) + EPS) * ng_ref[1:2, :]
        o_ref[...] = x_ref[...] + mod_ref[0, 2:3, :] * y


def _ffn_sublayer(x, ng_pre, ng_post, shift, scale, gate, w_up, w_conv, w_down, layer):
    bsz, length, d = x.shape
    m = bsz * length
    tm = _pick(length, (512, 256, 128))
    tf = _pick(D_FF, (512, 256, 128))
    nf = D_FF // tf
    bps = length // tm
    hb = tm // HALO
    nrow = m // HALO
    per_batch = shift.shape[0] > 1
    ng = jnp.stack([ng_pre, ng_post])
    mod = jnp.stack([shift, scale, gate], axis=1)
    out = pl.pallas_call(
        functools.partial(_ffn_kernel, blocks_per_seq=bps),
        grid=(m // tm, nf),
        in_specs=[pl.BlockSpec((tm, d), lambda i, j: (i, 0)),
                  pl.BlockSpec((HALO, d), lambda i, j: (jnp.maximum(i * hb - 1, 0), 0)),
                  pl.BlockSpec((HALO, d), lambda i, j: (jnp.minimum((i + 1) * hb, nrow - 1), 0)),
                  pl.BlockSpec((2, d), lambda i, j: (0, 0)),
                  pl.BlockSpec((1, 3, d), (lambda i, j: (i // bps, 0, 0)) if per_batch else (lambda i, j: (0, 0, 0))),
                  pl.BlockSpec((None, d, tf), lambda i, j: (layer, 0, j)),
                  pl.BlockSpec((None, d, tf), lambda i, j: (layer, 0, j + nf)),
                  pl.BlockSpec((None, FFN_CONV, tf), lambda i, j: (layer, 0, j)),
                  pl.BlockSpec((None, FFN_CONV, tf), lambda i, j: (layer, 0, j + nf)),
                  pl.BlockSpec((None, tf, d), lambda i, j: (layer, j, 0))],
        out_specs=pl.BlockSpec((tm, d), lambda i, j: (i, 0)),
        out_shape=jax.ShapeDtypeStruct((m, d), F32),
        scratch_shapes=[pltpu.VMEM((tm + 2 * HALO, d), BF16), pltpu.VMEM((tm, d), F32)],
        compiler_params=pltpu.CompilerParams(
            dimension_semantics=("parallel", "arbitrary"), vmem_limit_bytes=VMEM_LIMIT),
        name="ffn",
    )(x.reshape(m, d), x.reshape(m, d), x.reshape(m, d), ng, mod, w_up, w_up, w_conv, w_conv, w_down)
    return out.reshape(bsz, length, d)


def kernel(x, c, ctx, c_ctx, ada_w, ada_b, norm_g, w_in, hy_conv, hy_w1, hy_b1, hy_w2, hy_b2, hy_w3, hy_bias, lru_conv, lru_wa, lru_ba, lru_wx, lru_bx, lru_lam, mla_gq, mla_gkv, mla_wuq, mla_wukv, head_g, w_out, ffn_up, ffn_conv, ffn_down):
    depth = ada_w.shape[0]
    bsz, seq, _ = x.shape
    ctx_len = ctx.shape[1]
    rope_tabs = _rope_lane_tables(_rope_tables(seq))
    dft_lat, dft_ctx = _dft_tables(seq), _dft_tables(ctx_len)
    xc = ctx
    w_in_b = jnp.pad(w_in, ((0, 0), (0, 0), (0, NOPE_D - ROPE_D))).astype(BF16)
    wuq_pad = jax.vmap(_pad_heads_q)(mla_wuq).astype(BF16)
    wukv_b, w_out_b = mla_wukv.astype(BF16), w_out.astype(BF16)
    up_b, down_b = ffn_up.astype(BF16), ffn_down.astype(BF16)
    mods = _adaln(jnp.concatenate([c, c_ctx[None]], axis=0), ada_w, ada_b)

    def seq3(t, length):
        return None if t is None else t.reshape(bsz, length, t.shape[-1])

    for l in range(depth):
        need_ctx = l < depth - 1
        ng = norm_g[l]
        sh1, sc1, g1, sh2, sc2, g2 = jnp.split(mods[l, :bsz], N_MOD, axis=-1)
        csh1, csc1, cg1, csh2, csc2, cg2 = jnp.split(mods[l, bsz:], N_MOD, axis=-1)
        hy_f = (hy_w1[l], hy_b1[l], hy_w2[l], hy_b2[l], hy_w3[l])
        lru_p = (lru_conv[l], lru_wa[l], lru_ba[l], lru_wx[l], lru_bx[l], lru_lam[l])

        u = _proj_in(x, ng[0], sh1, sc1, w_in_b, l)
        uc = _proj_in(xc, ng[0], csh1, csc1, w_in_b, l)
        u3, uc3 = seq3(u, seq), seq3(uc, ctx_len)

        hc_sum, hc_end = _rglru(uc3[..., OFF_LRU_X:OFF_MLA_KV], *lru_p, jnp.zeros((bsz, 2, LRU_C), F32))
        q_c, kv_c, kr_c = [seq3(t, ctx_len) for t in
                           _proj_qkv(uc, ctx_len, mla_gq[l], wuq_pad, mla_gkv[l], wukv_b, l, None, with_q=need_ctx)]

        y_hy = _hyena_mix(u3, hy_conv[l], _filter_spectra(seq, *hy_f, dft_lat), hy_bias[l], dft_lat)
        h_sum, _ = _rglru(u3[..., OFF_LRU_X:OFF_MLA_KV], *lru_p, hc_end)
        q, kv, kr = [seq3(t, seq) for t in
                     _proj_qkv(u, seq, mla_gq[l], wuq_pad, mla_gkv[l], wukv_b, l, rope_tabs)]
        y_mla = _attention(q, [(kv_c, kr_c), (kv, kr)])
        x = _mix_out(x, y_hy, h_sum, u, y_mla, head_g[l], w_out_b, l, ng[1], g1)

        if need_ctx:
            yc_hy = _hyena_mix(uc3, hy_conv[l], _filter_spectra(ctx_len, *hy_f, dft_ctx), hy_bias[l], dft_ctx)
            yc_mla = _attention(q_c, [(kv_c, kr_c)])
            xc = _mix_out(xc, yc_hy, hc_sum, uc, yc_mla, head_g[l], w_out_b, l, ng[1], cg1)

        x = _ffn_sublayer(x, ng[2], ng[3], sh2, sc2, g2, up_b, ffn_conv, down_b, l)
        if need_ctx:
            xc = _ffn_sublayer(xc, ng[2], ng[3], csh2, csc2, cg2, up_b, ffn_conv, down_b, l)
    return x
```

```python
import functools
import math

import jax
import jax.numpy as jnp
from jax import lax
from jax.experimental import pallas as pl
from jax.experimental.pallas import tpu as pltpu

F32 = jnp.float32
BF16 = jnp.bfloat16

D_MODEL = 2048
GRID_W = 64
HEAD_W = 128
HY_C = 512
HY_ORDER = 2
HY_SHORT = 3
HY_BANDS = 16
HY_FAST_DECAY = 0.3
HY_SLOW_DECAY = 1.5
HY_DECAY_TARGET = 1e-2
LRU_C = 512
LRU_BLOCKS = 4
LRU_BW = LRU_C // LRU_BLOCKS
LRU_CONV = 4
RG_C = 8.0
MLA_H = 8
Q_RANK = 512
KV_RANK = 256
NOPE_D = 128
ROPE_D = 64
V_D = 128
ROPE_BASE = 10000.0
MLA_SCALE = (NOPE_D + ROPE_D) ** -0.5
D_MIX = HY_C + LRU_C + MLA_H * V_D
N_MIX_HEADS = D_MIX // HEAD_W
D_FF = 5632
FFN_CONV = 3
N_MOD = 6
EPS = 1e-6

OFF_HY = 0
OFF_LRU_G = OFF_HY + (HY_ORDER + 1) * HY_C
OFF_MLA_Q = OFF_LRU_G + LRU_C
OFF_LRU_X = OFF_MLA_Q + Q_RANK
OFF_MLA_KV = OFF_LRU_X + LRU_C
OFF_MLA_KR = OFF_MLA_KV + KV_RANK
D_IN = OFF_MLA_KR + ROPE_D

HEAD_QK = 2 * NOPE_D
VMEM_LIMIT = 48 * 1024 * 1024
FFN_VMEM_LIMIT = 56 * 1024 * 1024


def _pick(n, prefs):
    for p in prefs:
        if n % p == 0:
            return p
    return n


def _adaln_kernel(c_ref, w_ref, b_ref, o_ref):
    c = c_ref[...]
    a = (c * jax.nn.sigmoid(c)).astype(BF16)
    o_ref[...] = jnp.dot(a, w_ref[...].astype(BF16), preferred_element_type=F32) + b_ref[...]


def _adaln(cond, ada_w, ada_b):
    depth, d, n = ada_w.shape
    rows = cond.shape[0]
    rp = -(-rows // SUBLANES) * SUBLANES
    cond = jnp.pad(cond, ((0, rp - rows), (0, 0)))
    tn = 1024
    out = pl.pallas_call(
        _adaln_kernel,
        grid=(depth, n // tn),
        in_specs=[pl.BlockSpec((rp, d), lambda l, j: (0, 0)),
                  pl.BlockSpec((None, d, tn), lambda l, j: (l, 0, j)),
                  pl.BlockSpec((None, 1, tn), lambda l, j: (l, 0, j))],
        out_specs=pl.BlockSpec((None, rp, tn), lambda l, j: (l, 0, j)),
        out_shape=jax.ShapeDtypeStruct((depth, rp, n), F32),
        compiler_params=pltpu.CompilerParams(
            dimension_semantics=("parallel", "parallel"), vmem_limit_bytes=VMEM_LIMIT),
        name="adaln",
    )(cond, ada_w, ada_b[:, None])
    return out[:, :rows]


Q_SPLIT = 2

def _attn_kernel(*refs, n_seg):
    q_ref = refs[0]
    seg_refs = refs[1:1 + 3 * n_seg]
    o_ref = refs[1 + 3 * n_seg]
    k_scr, v_scr = refs[2 + 3 * n_seg:]

    @pl.when(pl.program_id(2) == 0)
    def _():
        off = 0
        for s in range(n_seg):
            kn_ref, v_ref, kr_ref = seg_refs[3 * s:3 * s + 3]
            n = kn_ref.shape[1]
            k_scr[off:off + n, 0:NOPE_D] = kn_ref[0]
            k_scr[off:off + n, NOPE_D:HEAD_QK] = kr_ref[0]
            v_scr[off:off + n, :] = v_ref[0]
            off += n

    sub = q_ref.shape[1] // Q_SPLIT
    for part in range(Q_SPLIT):
        rows = slice(part * sub, (part + 1) * sub)
        s = lax.dot_general(q_ref[0, rows], k_scr[...], (((1,), (1,)), ((), ())), preferred_element_type=F32)
        m = jnp.max(s, axis=-1, keepdims=True)
        p = jnp.exp(s - m)
        l = jnp.sum(p, axis=-1, keepdims=True)
        o = jnp.dot(p.astype(BF16), v_scr[...], preferred_element_type=F32)
        o_ref[0, rows] = o / l


def _attention(q, segs):
    bsz, lq, _ = q.shape
    tq = _pick(lq, (512, 256, 128))
    lk = sum(kv.shape[1] for kv, _ in segs)
    in_specs = [pl.BlockSpec((1, tq, HEAD_QK), lambda b, h, i: (b, i, h))]
    args = [q]
    for kv, kr in segs:
        n = kv.shape[1]
        in_specs += [pl.BlockSpec((1, n, NOPE_D), lambda b, h, i: (b, 0, 2 * h)),
                     pl.BlockSpec((1, n, V_D), lambda b, h, i: (b, 0, 2 * h + 1)),
                     pl.BlockSpec((1, n, NOPE_D), lambda b, h, i: (b, 0, 0))]
        args += [kv, kv, kr]
    return pl.pallas_call(
        functools.partial(_attn_kernel, n_seg=len(segs)),
        grid=(bsz, MLA_H, lq // tq),
        in_specs=in_specs,
        out_specs=pl.BlockSpec((1, tq, V_D), lambda b, h, i: (b, i, h)),
        out_shape=jax.ShapeDtypeStruct((bsz, lq, MLA_H * V_D), F32),
        scratch_shapes=[pltpu.VMEM((lk, HEAD_QK), BF16), pltpu.VMEM((lk, V_D), BF16)],
        compiler_params=pltpu.CompilerParams(
            dimension_semantics=("parallel", "parallel", "arbitrary"), vmem_limit_bytes=VMEM_LIMIT),
        name="attn",
    )(*args)


LANES = 128
PGROUP = 2
STAGE_UNROLL = 8


def _fft_dims(length):
    n = 2 * length
    bn = 128 if length >= 1024 else 32
    return n // bn, bn, bn + SUBLANES


def _dft_tables(length):
    a_n, bn, _ = _fft_dims(length)
    n, half = a_n * bn, a_n // 2
    p = jnp.arange(a_n, dtype=jnp.int32)
    b = jnp.arange(bn, dtype=jnp.int32)
    nn = bn * p[None, None, :] + b[:, None, None]
    ang = (2.0 * math.pi / n) * ((p[None, :, None] * nn) % n).astype(F32)
    cr, ci = jnp.cos(ang), -jnp.sin(ang)
    m1 = jnp.concatenate([jnp.concatenate([cr[:, :, :half], -ci[:, :, :half]], axis=2),
                          jnp.concatenate([ci[:, :, :half], cr[:, :, :half]], axis=2)], axis=1)
    m1f = jnp.concatenate([cr, ci], axis=1)
    crt, cit = jnp.swapaxes(cr, 1, 2)[:, :half], jnp.swapaxes(ci, 1, 2)[:, :half]
    m4 = jnp.concatenate([jnp.concatenate([crt, cit], axis=2),
                          jnp.concatenate([-cit, crt], axis=2)], axis=1) / n
    ang2 = (2.0 * math.pi / bn) * ((b[:, None] * b[None, :]) % bn).astype(F32)
    wr, wi = jnp.cos(ang2), -jnp.sin(ang2)
    m2 = jnp.concatenate([jnp.concatenate([wr, -wi], axis=1), jnp.concatenate([wi, wr], axis=1)], axis=0)
    m3 = jnp.concatenate([jnp.concatenate([wr, wi], axis=1), jnp.concatenate([-wi, wr], axis=1)], axis=0)
    return tuple(t.astype(BF16) for t in (m1, m1f, m2, m3, m4))


def _stage1(src, rows_per_part, parts, m_ref, t2, a_n, bn, pitch):
    def body(b, carry):
        rhs = jnp.concatenate([src[pl.ds(part * rows_per_part * pitch + b, rows_per_part, stride=pitch), :]
                               for part in range(parts)], axis=0)
        t = jnp.dot(m_ref[b], rhs.astype(BF16), preferred_element_type=F32)
        t2[pl.ds(b, a_n, stride=pitch), :] = t[:a_n]
        t2[pl.ds(a_n * pitch + b, a_n, stride=pitch), :] = t[a_n:]
        return carry
    lax.fori_loop(0, bn, body, 0, unroll=STAGE_UNROLL)


def _stage2_rhs(t2, g, a_n, bn, pitch):
    cols = []
    for k in range(PGROUP):
        r0 = pl.multiple_of((g * PGROUP + k) * pitch, SUBLANES)
        cols.append(jnp.concatenate([t2[pl.ds(r0, bn), :], t2[pl.ds(a_n * pitch + r0, bn), :]], axis=0))
    return jnp.concatenate(cols, axis=1).astype(BF16)


def _fftconv_kernel(z_ref, spec_ref, m1_ref, m2_ref, m3_ref, m4_ref, y_ref, zb, t2, *, length):
    a_n, bn, pitch = _fft_dims(length)
    half = a_n // 2
    for bi in range(2):
        for a in range(half):
            zb[(bi * half + a) * pitch:(bi * half + a) * pitch + bn, :] = z_ref[bi, a * bn:(a + 1) * bn, :]
    _stage1(zb, half, 2, m1_ref, t2, a_n, bn, pitch)

    def mid(g, carry):
        x = jnp.dot(m2_ref[...], _stage2_rhs(t2, g, a_n, bn, pitch), preferred_element_type=F32)
        k = spec_ref[0, 0, g]
        xr, xi, kr, ki = x[:bn], x[bn:], k[:bn], k[bn:]
        y = jnp.concatenate([xr * kr - xi * ki, xr * ki + xi * kr], axis=0).astype(BF16)
        u = jnp.dot(m3_ref[...], y, preferred_element_type=F32)
        for kk in range(PGROUP):
            r0 = pl.multiple_of((g * PGROUP + kk) * pitch, SUBLANES)
            t2[pl.ds(r0, bn), :] = u[:bn, kk * LANES:(kk + 1) * LANES]
            t2[pl.ds(a_n * pitch + r0, bn), :] = u[bn:, kk * LANES:(kk + 1) * LANES]
        return carry
    lax.fori_loop(0, a_n // PGROUP, mid, 0, unroll=2)

    def last(b, carry):
        rhs = jnp.concatenate([t2[pl.ds(b, a_n, stride=pitch), :],
                               t2[pl.ds(a_n * pitch + b, a_n, stride=pitch), :]], axis=0)
        y = jnp.dot(m4_ref[b], rhs.astype(BF16), preferred_element_type=F32)
        zb[pl.ds(b, half, stride=pitch), :] = y[:half]
        zb[pl.ds(half * pitch + b, half, stride=pitch), :] = y[half:]
        return carry
    lax.fori_loop(0, bn, last, 0, unroll=STAGE_UNROLL)
    for bi in range(2):
        for a in range(half):
            y_ref[bi, a * bn:(a + 1) * bn, :] = zb[(bi * half + a) * pitch:(bi * half + a) * pitch + bn, :]


def _fftconv(z, spec, tables):
    bsz, length, chans = z.shape
    assert bsz == 2, "the two batch rows are packed as one complex signal"
    a_n, bn, pitch = _fft_dims(length)
    m1, _, m2, m3, m4 = tables
    const3 = lambda c: (0, 0, 0)
    const2 = lambda c: (0, 0)
    once = pl.Buffered(1)
    return pl.pallas_call(
        functools.partial(_fftconv_kernel, length=length),
        grid=(chans // LANES,),
        in_specs=[pl.BlockSpec((2, length, LANES), lambda c: (0, 0, c)),
                  pl.BlockSpec((1, 1, a_n // PGROUP, 2 * bn, PGROUP * LANES), lambda c: (0, c, 0, 0, 0),
                               pipeline_mode=once),
                  pl.BlockSpec(m1.shape, const3, pipeline_mode=once),
                  pl.BlockSpec(m2.shape, const2, pipeline_mode=once),
                  pl.BlockSpec(m3.shape, const2, pipeline_mode=once),
                  pl.BlockSpec(m4.shape, const3, pipeline_mode=once)],
        out_specs=pl.BlockSpec((2, length, LANES), lambda c: (0, 0, c)),
        out_shape=jax.ShapeDtypeStruct(z.shape, F32),
        scratch_shapes=[pltpu.VMEM((a_n * pitch, LANES), F32), pltpu.VMEM((2 * a_n * pitch, LANES), F32)],
        compiler_params=pltpu.CompilerParams(dimension_semantics=("parallel",), vmem_limit_bytes=VMEM_LIMIT),
        name="fftconv",
    )(z, spec[None], m1, m2, m3, m4)


def _split_bf16(x):
    hi = x.astype(BF16)
    return hi, (x - hi.astype(F32)).astype(BF16)


def _dot_split(a, b):
    ah, al = _split_bf16(a)
    bh, bl = _split_bf16(b)
    d = functools.partial(jnp.dot, preferred_element_type=F32)
    return d(ah, bh) + (d(al, bh) + d(ah, bl))


def _filtspec_kernel(hf_ref, hb_ref, w3f_ref, w3b_ref, dec_ref, m1_ref, m2_ref, spec_ref, kb, t2, *, length):
    a_n, bn, pitch = _fft_dims(length)
    half = a_n // 2
    row = lax.broadcasted_iota(jnp.int32, (length, 1), 0).astype(F32)
    delta = dec_ref[...]
    kf = _dot_split(hf_ref[...], w3f_ref[...]) * jnp.exp(-(row / length) * delta)
    kbk = _dot_split(hb_ref[...], w3b_ref[...]) * jnp.exp(-((length - row) / length) * delta)
    kbk = jnp.where(row == 0.0, 0.0, kbk)
    norm = jnp.sum(jnp.abs(kf), axis=0, keepdims=True) + jnp.sum(jnp.abs(kbk), axis=0, keepdims=True)
    kf, kbk = kf / norm, kbk / norm
    for a in range(half):
        kb[a * pitch:a * pitch + bn, :] = kf[a * bn:(a + 1) * bn]
        kb[(half + a) * pitch:(half + a) * pitch + bn, :] = kbk[a * bn:(a + 1) * bn]
    _stage1(kb, a_n, 1, m1_ref, t2, a_n, bn, pitch)

    def mid(g, carry):
        spec_ref[0, 0, g] = jnp.dot(m2_ref[...], _stage2_rhs(t2, g, a_n, bn, pitch), preferred_element_type=F32)
        return carry
    lax.fori_loop(0, a_n // PGROUP, mid, 0, unroll=2)


def _filter_spectra(length, w1, b1, w2, b2, w3, tables):
    a_n, bn, pitch = _fft_dims(length)
    hp = lax.Precision.HIGHEST
    t_f = jnp.arange(length, dtype=F32) / length
    t_b = (length - jnp.arange(length, dtype=F32)) / length
    band = jnp.arange(1, HY_BANDS + 1, dtype=F32)

    def hidden(t):
        ang = 2.0 * math.pi * t[:, None] * band
        feats = jnp.concatenate([t[:, None], jnp.sin(ang), jnp.cos(ang)], axis=-1)
        hid = jnp.sin(jnp.dot(feats, w1, precision=hp) + b1)
        return jnp.sin(jnp.dot(hid, w2, precision=hp) + b2)

    deltas = jnp.abs(jnp.linspace(math.log(HY_DECAY_TARGET) / HY_SLOW_DECAY,
                                  math.log(HY_DECAY_TARGET) / HY_FAST_DECAY, HY_C, dtype=F32))[None]
    _, m1f, m2, _, _ = tables
    nct = HY_C // LANES
    hid_f, hid_b = hidden(t_f), hidden(t_b)
    hidden_w = hid_f.shape[1]
    once = pl.Buffered(1)
    return pl.pallas_call(
        functools.partial(_filtspec_kernel, length=length),
        grid=(HY_ORDER, nct),
        in_specs=[pl.BlockSpec((length, hidden_w), lambda o, c: (0, 0), pipeline_mode=once),
                  pl.BlockSpec((length, hidden_w), lambda o, c: (0, 0), pipeline_mode=once),
                  pl.BlockSpec((hidden_w, LANES), lambda o, c: (0, o * 2 * nct + c)),
                  pl.BlockSpec((hidden_w, LANES), lambda o, c: (0, o * 2 * nct + nct + c)),
                  pl.BlockSpec((1, LANES), lambda o, c: (0, c)),
                  pl.BlockSpec(m1f.shape, lambda o, c: (0, 0, 0), pipeline_mode=once),
                  pl.BlockSpec(m2.shape, lambda o, c: (0, 0), pipeline_mode=once)],
        out_specs=pl.BlockSpec((1, 1, a_n // PGROUP, 2 * bn, PGROUP * LANES), lambda o, c: (o, c, 0, 0, 0)),
        out_shape=jax.ShapeDtypeStruct((HY_ORDER, nct, a_n // PGROUP, 2 * bn, PGROUP * LANES), F32),
        scratch_shapes=[pltpu.VMEM((a_n * pitch, LANES), F32), pltpu.VMEM((2 * a_n * pitch, LANES), F32)],
        compiler_params=pltpu.CompilerParams(
            dimension_semantics=("parallel", "parallel"), vmem_limit_bytes=VMEM_LIMIT),
        name="filtspec",
    )(hid_f, hid_b, w3, w3, deltas, m1f, m2)


def _hygate_kernel(p_ref, w_ref, *rest, first):
    x = p_ref[0]
    n = x.shape[0]
    row = lax.broadcasted_iota(jnp.int32, x.shape, 0)
    prev = jnp.where(row == 0, 0.0, pltpu.roll(x, 1, axis=0))
    nxt = jnp.where(row == n - 1, 0.0, pltpu.roll(x, n - 1, axis=0))
    c = w_ref[0:1, :] * prev + w_ref[1:2, :] * x + w_ref[2:3, :] * nxt
    if first:
        (o_ref,) = rest
        o_ref[0] = c
    else:
        y_ref, z_ref, b_ref, o_ref = rest
        o_ref[0] = c * (y_ref[0] + z_ref[0] * b_ref[...])


def _hygate(u, part, conv_w, y=None, z=None, bias=None):
    bsz, length, _ = u.shape
    nct = HY_C // LANES
    blk = lambda b, c: (b, 0, c)
    in_specs = [pl.BlockSpec((1, length, LANES), lambda b, c: (b, 0, part * nct + c)),
                pl.BlockSpec((HY_SHORT, LANES), lambda b, c: (0, part * nct + c))]
    args = [u, conv_w]
    if y is not None:
        in_specs += [pl.BlockSpec((1, length, LANES), blk), pl.BlockSpec((1, length, LANES), blk),
                     pl.BlockSpec((1, LANES), lambda b, c: (0, c))]
        args += [y, z, bias]
    return pl.pallas_call(
        functools.partial(_hygate_kernel, first=y is None),
        grid=(bsz, nct),
        in_specs=in_specs,
        out_specs=pl.BlockSpec((1, length, LANES), blk),
        out_shape=jax.ShapeDtypeStruct((bsz, length, HY_C), F32),
        compiler_params=pltpu.CompilerParams(
            dimension_semantics=("parallel", "parallel"), vmem_limit_bytes=VMEM_LIMIT),
        name="hygate",
    )(*args)


def _hyena_mix(u, conv_w, spec, bias, tables):
    z = _hygate(u, 0, conv_w)
    for o in range(HY_ORDER):
        y = _fftconv(z, spec[o], tables)
        z = _hygate(u, o + 1, conv_w, y, z, bias[o:o + 1])
    return z


SUBLANES = 8
PAD = LRU_CONV - 1


def _lru_kernel(x_ref, wg_ref, pv_ref, h0_ref, o_ref, end_ref, xp, a_scr, b_scr, o_scr, *, steps, chunk):
    sub = lax.broadcasted_iota(jnp.int32, (SUBLANES, LRU_BW), 0)

    def gather(j, carry):
        xp[PAD + j] = x_ref[pl.ds(j, SUBLANES, stride=steps), :]
        return carry

    lax.fori_loop(0, steps, gather, 0, unroll=8)
    for k in range(PAD):
        tail = pltpu.roll(xp[steps + k], 1, axis=0)
        xp[k] = jnp.where(sub == 0, 0.0, tail)
        head = pltpu.roll(xp[PAD + k], SUBLANES - 1, axis=0)
        xp[PAD + steps + k] = jnp.where(sub == SUBLANES - 1, 0.0, head)

    n_chunks = steps // chunk
    for d in range(2):
        ba = pv_ref[d, 0:1, :]
        bx = pv_ref[d, 1:2, :]
        lam = pv_ref[d, 2:3, :]
        neg_c_sp = -RG_C * jax.nn.softplus(-lam)
        taps = [pv_ref[d, 3 + k:4 + k, :].reshape(1, 1, LRU_BW) for k in range(LRU_CONV)]
        w_gate = wg_ref[d, 0]
        base = 0 if d == 0 else PAD

        def gates(c, carry):
            j0 = pl.multiple_of(c * chunk, chunk)
            xc = taps[0] * xp[pl.ds(j0 + base, chunk)]
            for k in range(1, LRU_CONV):
                xc = xc + taps[k] * xp[pl.ds(j0 + base + k, chunk)]
            xc2 = xc.reshape(chunk * SUBLANES, LRU_BW)
            g = jnp.dot(xc2.astype(BF16), w_gate, preferred_element_type=F32)
            r = jax.nn.sigmoid(g[:, :LRU_BW] + ba)
            i = jax.nn.sigmoid(g[:, LRU_BW:] + bx)
            log_a = neg_c_sp * r
            a = jnp.exp(log_a)
            t = jnp.tanh(log_a)
            b = jnp.sqrt(-2.0 * t / (1.0 - t)) * (i * xc2)
            a_scr[pl.ds(j0, chunk)] = a.reshape(chunk, SUBLANES, LRU_BW)
            b_scr[pl.ds(j0, chunk)] = b.reshape(chunk, SUBLANES, LRU_BW)
            return carry

        lax.fori_loop(0, n_chunks, gates, 0)

        def step(t, carry):
            h, p = carry
            j = t if d == 0 else steps - 1 - t
            a = a_scr[j]
            h = a * h + b_scr[j]
            p = a * p
            b_scr[j] = h
            a_scr[j] = p
            return h, p

        h_loc, p_loc = lax.fori_loop(
            0, steps, step,
            (jnp.zeros((SUBLANES, LRU_BW), F32), jnp.ones((SUBLANES, LRU_BW), F32)), unroll=8)

        carry = h0_ref[0, d:d + 1, :]
        cin = jnp.zeros((SUBLANES, LRU_BW), F32)
        order = range(SUBLANES) if d == 0 else range(SUBLANES - 1, -1, -1)
        for s in order:
            cin = jnp.where(sub == s, jnp.broadcast_to(carry, (SUBLANES, LRU_BW)), cin)
            carry = p_loc[s:s + 1, :] * carry + h_loc[s:s + 1, :]
        end_ref[0, d:d + 1, :] = carry

        def fix(c, carry_):
            j0 = pl.multiple_of(c * chunk, chunk)
            h = b_scr[pl.ds(j0, chunk)] + a_scr[pl.ds(j0, chunk)] * cin[None]
            if d == 0:
                o_scr[pl.ds(j0, chunk)] = h
            else:
                h = h + o_scr[pl.ds(j0, chunk)]
                for jj in range(chunk):
                    o_ref[pl.ds(j0 + jj, SUBLANES, stride=steps), :] = h[jj]
            return carry_

        lax.fori_loop(0, n_chunks, fix, 0)


def _rglru(u, length, lconv, lwa, lba, lwx, lbx, llam, h0):
    m = u.shape[0]
    bsz = m // length
    steps = length // SUBLANES
    chunk = min(32, steps)
    wg = jnp.concatenate([lwa, lwx], axis=-1).astype(BF16)
    pv = jnp.concatenate([lba[:, None], lbx[:, None], llam[:, None], lconv,
                          jnp.zeros((2, 1, LRU_C), F32)], axis=1)
    seg = pltpu.VMEM((steps, SUBLANES, LRU_BW), F32)
    return pl.pallas_call(
        functools.partial(_lru_kernel, steps=steps, chunk=chunk),
        grid=(bsz, LRU_BLOCKS),
        in_specs=[pl.BlockSpec((length, LRU_BW), lambda b, n: (b, OFF_LRU_X // LRU_BW + n)),
                  pl.BlockSpec((2, 1, LRU_BW, 2 * LRU_BW), lambda b, n: (0, n, 0, 0)),
                  pl.BlockSpec((2, SUBLANES, LRU_BW), lambda b, n: (0, 0, n)),
                  pl.BlockSpec((1, 2, LRU_BW), lambda b, n: (b, 0, n))],
        out_specs=[pl.BlockSpec((length, LRU_BW), lambda b, n: (b, n)),
                   pl.BlockSpec((1, 2, LRU_BW), lambda b, n: (b, 0, n))],
        out_shape=[jax.ShapeDtypeStruct((m, LRU_C), F32),
                   jax.ShapeDtypeStruct((bsz, 2, LRU_C), F32)],
        scratch_shapes=[pltpu.VMEM((steps + 2 * PAD, SUBLANES, LRU_BW), F32), seg, seg, seg],
        compiler_params=pltpu.CompilerParams(
            dimension_semantics=("parallel", "parallel"), vmem_limit_bytes=VMEM_LIMIT),
        name="rglru",
    )(u, wg, pv, h0)


def _rope_tables(length):
    rows = length // GRID_W
    row = jnp.repeat(jnp.arange(rows, dtype=F32), GRID_W)
    col = jnp.tile(jnp.arange(GRID_W, dtype=F32), rows)
    n_freq = ROPE_D // 4
    inv = ROPE_BASE ** (-jnp.arange(n_freq, dtype=F32) / n_freq)
    ang = jnp.concatenate([row[:, None] * inv, col[:, None] * inv], axis=-1)
    return jnp.cos(ang), jnp.sin(ang)


def _rope_lane_tables(rope):
    cos, sin = rope
    z = jnp.zeros_like(cos)
    return (jnp.concatenate([cos, cos, z, z], axis=1), jnp.concatenate([-sin, z, z, z], axis=1),
            jnp.concatenate([z, sin, z, z], axis=1))


def _pad_heads_q(wuq):
    w = wuq.reshape(Q_RANK, MLA_H, NOPE_D + ROPE_D)
    w = jnp.pad(w, ((0, 0), (0, 0), (0, HEAD_QK - NOPE_D - ROPE_D)))
    return w.reshape(Q_RANK, MLA_H * HEAD_QK)


def _rms(xv, g):
    return xv * lax.rsqrt(jnp.mean(xv * xv, axis=-1, keepdims=True) + EPS) * g


ROW_SPLIT = 2


def _proj_in_kernel(x_ref, g_ref, mod_ref, w_ref, o_ref):
    sub = x_ref.shape[0] // ROW_SPLIT
    for part in range(ROW_SPLIT):
        rows = slice(part * sub, (part + 1) * sub)
        y = _rms(x_ref[rows, :], g_ref[...])
        h = (y * (1.0 + mod_ref[0, 1:2, :]) + mod_ref[0, 0:1, :]).astype(BF16)
        o_ref[rows, :] = jnp.dot(h, w_ref[...], preferred_element_type=F32)


def _proj_in(x, g, shift, scale, w, layer):
    bsz, length, d = x.shape
    m, n = bsz * length, w.shape[2]
    tm = _pick(length, (512, 256, 128))
    bps = length // tm
    mod = jnp.stack([shift, scale], axis=1)
    mod_idx = (lambda i: (i // bps, 0, 0)) if shift.shape[0] > 1 else (lambda i: (0, 0, 0))
    return pl.pallas_call(
        _proj_in_kernel,
        grid=(m // tm,),
        in_specs=[pl.BlockSpec((tm, d), lambda i: (i, 0)),
                  pl.BlockSpec((1, d), lambda i: (0, 0)),
                  pl.BlockSpec((1, 2, d), mod_idx),
                  pl.BlockSpec((None, d, n), lambda i: (layer, 0, 0), pipeline_mode=pl.Buffered(1))],
        out_specs=pl.BlockSpec((tm, n), lambda i: (i, 0)),
        out_shape=jax.ShapeDtypeStruct((m, n), F32),
        compiler_params=pltpu.CompilerParams(dimension_semantics=("parallel",), vmem_limit_bytes=VMEM_LIMIT),
        name="proj_in",
    )(x.reshape(m, d), g[None], mod, w)


def _proj_qkv_kernel(*refs, rope, with_q):
    refs = list(refs)
    uq_ref = refs.pop(0) if with_q else None
    ukv_ref, ukr_ref = refs.pop(0), refs.pop(0)
    gq_ref, wq_ref = (refs.pop(0), refs.pop(0)) if with_q else (None, None)
    gkv_ref, wkv_ref = refs.pop(0), refs.pop(0)
    cos_ref, sa_ref, sb_ref = (refs.pop(0), refs.pop(0), refs.pop(0)) if rope else (None, None, None)
    q_ref = refs.pop(0) if with_q else None
    kv_ref, kr_ref = refs

    def rot(blk):
        if not rope:
            return blk
        return (blk * cos_ref[...] + pltpu.roll(blk, LANES - 32, axis=1) * sa_ref[...]
                + pltpu.roll(blk, 32, axis=1) * sb_ref[...])

    if with_q:
        nq = _rms(uq_ref[...], gq_ref[...]).astype(BF16)
        q = jnp.dot(nq, wq_ref[...], preferred_element_type=F32)
        for hd in range(MLA_H):
            lo = hd * HEAD_QK
            q_ref[:, lo:lo + NOPE_D] = (q[:, lo:lo + NOPE_D] * MLA_SCALE).astype(BF16)
            q_ref[:, lo + NOPE_D:lo + HEAD_QK] = (rot(q[:, lo + NOPE_D:lo + HEAD_QK]) * MLA_SCALE).astype(BF16)
    nkv = _rms(ukv_ref[...], gkv_ref[...]).astype(BF16)
    kv_ref[...] = jnp.dot(nkv, wkv_ref[...], preferred_element_type=F32).astype(BF16)
    kr_ref[...] = rot(ukr_ref[...]).astype(BF16)


def _proj_qkv(u, length, gq, wuq_pad, gkv, wukv, layer, rope_tabs, with_q=True):
    m = u.shape[0]
    tm = _pick(length, (512, 256, 128))
    bps = length // tm
    row = lambda i: (i, 0)
    const = lambda i: (0, 0)
    wsel = lambda i: (layer, 0, 0)
    in_specs, args = [], []
    if with_q:
        in_specs.append(pl.BlockSpec((tm, Q_RANK), lambda i: (i, OFF_MLA_Q // Q_RANK)))
        args.append(u)
    in_specs += [pl.BlockSpec((tm, KV_RANK), lambda i: (i, OFF_MLA_KV // KV_RANK)),
                 pl.BlockSpec((tm, LANES), lambda i: (i, OFF_MLA_KR // LANES))]
    args += [u, u]
    if with_q:
        in_specs += [pl.BlockSpec((1, Q_RANK), const), pl.BlockSpec((None,) + wuq_pad.shape[1:], wsel)]
        args += [gq[None], wuq_pad]
    in_specs += [pl.BlockSpec((1, KV_RANK), const), pl.BlockSpec((None,) + wukv.shape[1:], wsel)]
    args += [gkv[None], wukv]
    if rope_tabs is not None:
        in_specs += [pl.BlockSpec((tm, LANES), lambda i: (i % bps, 0))] * 3
        args += list(rope_tabs)
    out_specs = [pl.BlockSpec((tm, MLA_H * HEAD_QK), row), pl.BlockSpec((tm, LANES), row)]
    out_shape = [jax.ShapeDtypeStruct((m, MLA_H * HEAD_QK), BF16), jax.ShapeDtypeStruct((m, LANES), BF16)]
    if with_q:
        out_specs.insert(0, pl.BlockSpec((tm, MLA_H * HEAD_QK), row))
        out_shape.insert(0, jax.ShapeDtypeStruct((m, MLA_H * HEAD_QK), BF16))
    outs = pl.pallas_call(
        functools.partial(_proj_qkv_kernel, rope=rope_tabs is not None, with_q=with_q),
        grid=(m // tm,),
        in_specs=in_specs, out_specs=out_specs, out_shape=out_shape,
        compiler_params=pltpu.CompilerParams(dimension_semantics=("parallel",), vmem_limit_bytes=VMEM_LIMIT),
        name="proj_qkv",
    )(*args)
    return outs if with_q else [None] + list(outs)


def _mix_kernel(yhy_ref, hs_ref, gate_ref, ymla_ref, hg_ref, w_ref, x_ref, ng_ref, mod_ref, o_ref):
    sub = x_ref.shape[0] // ROW_SPLIT
    for part in range(ROW_SPLIT):
        rows = slice(part * sub, (part + 1) * sub)
        y = jnp.concatenate([yhy_ref[rows, :], jax.nn.gelu(gate_ref[rows, :], approximate=True) * hs_ref[rows, :],
                             ymla_ref[rows, :]], axis=1)
        heads = [_rms(y[:, hd * HEAD_W:(hd + 1) * HEAD_W], hg_ref[:, hd * HEAD_W:(hd + 1) * HEAD_W]).astype(BF16)
                 for hd in range(N_MIX_HEADS)]
        acc = jnp.dot(jnp.concatenate(heads, axis=1), w_ref[...], preferred_element_type=F32)
        o_ref[rows, :] = x_ref[rows, :] + mod_ref[0] * _rms(acc, ng_ref[...])


def _mix_out(x, y_hy, h_sum, u, y_mla, head_g, w_out, layer, ng_post, gate):
    bsz, length, d = x.shape
    m = bsz * length
    tm = _pick(length, (512, 256, 128))
    bps = length // tm
    row = lambda i: (i, 0)
    const = lambda i: (0, 0)
    mod_idx = (lambda i: (i // bps, 0, 0)) if gate.shape[0] > 1 else (lambda i: (0, 0, 0))
    out = pl.pallas_call(
        _mix_kernel,
        grid=(m // tm,),
        in_specs=[pl.BlockSpec((tm, HY_C), row), pl.BlockSpec((tm, LRU_C), row),
                  pl.BlockSpec((tm, LRU_C), lambda i: (i, OFF_LRU_G // LRU_C)),
                  pl.BlockSpec((tm, MLA_H * V_D), row),
                  pl.BlockSpec((1, D_MIX), const),
                  pl.BlockSpec((None,) + w_out.shape[1:], lambda i: (layer, 0, 0), pipeline_mode=pl.Buffered(1)),
                  pl.BlockSpec((tm, d), row), pl.BlockSpec((1, d), const), pl.BlockSpec((1, 1, d), mod_idx)],
        out_specs=pl.BlockSpec((tm, d), row),
        out_shape=jax.ShapeDtypeStruct((m, d), F32),
        compiler_params=pltpu.CompilerParams(dimension_semantics=("parallel",), vmem_limit_bytes=VMEM_LIMIT),
        name="mix_out",
    )(y_hy.reshape(m, HY_C), h_sum.reshape(m, LRU_C), u, y_mla.reshape(m, MLA_H * V_D), head_g[None], w_out,
      x.reshape(m, d), ng_post[None], gate[:, None])
    return out.reshape(bsz, length, d)


HALO = 16


def _ffn_kernel(x_ref, xprev_ref, xnext_ref, ng_ref, mod_ref, wg_ref, wv_ref, cg_ref, cv_ref, wd_ref,
                o_ref, hbuf, *, blocks_per_seq):
    i = pl.program_id(0)
    j = pl.program_id(1)
    tm = x_ref.shape[0]

    def norm_mod(xv):
        y = xv * lax.rsqrt(jnp.mean(xv * xv, axis=-1, keepdims=True) + EPS) * ng_ref[0:1, :]
        return (y * (1.0 + mod_ref[0, 1:2, :]) + mod_ref[0, 0:1, :]).astype(BF16)

    @pl.when(j == 0)
    def _():
        first = (i % blocks_per_seq) == 0
        last = (i % blocks_per_seq) == blocks_per_seq - 1
        hp = norm_mod(xprev_ref[...])
        hn = norm_mod(xnext_ref[...])
        hbuf[0:HALO] = jnp.where(first, jnp.zeros_like(hp), hp)
        hbuf[HALO:HALO + tm] = norm_mod(x_ref[...])
        hbuf[HALO + tm:] = jnp.where(last, jnp.zeros_like(hn), hn)
        o_ref[...] = jnp.zeros_like(o_ref)

    hb = hbuf[...]

    def conv(up, c_ref):
        prev = pltpu.roll(up, 1, axis=0)[HALO:HALO + tm]
        nxt = pltpu.roll(up, tm + 2 * HALO - 1, axis=0)[HALO:HALO + tm]
        return c_ref[0:1, :] * prev + c_ref[1:2, :] * up[HALO:HALO + tm] + c_ref[2:3, :] * nxt

    gate = conv(jnp.dot(hb, wg_ref[...], preferred_element_type=F32), cg_ref)
    val = conv(jnp.dot(hb, wv_ref[...], preferred_element_type=F32), cv_ref)
    act = (jax.nn.gelu(gate, approximate=True) * val).astype(BF16)
    o_ref[...] += jnp.dot(act, wd_ref[...], preferred_element_type=F32)

    @pl.when(j == pl.num_programs(1) - 1)
    def _():
        y = o_ref[...]
        y = y * lax.rsqrt(jnp.mean(y * y, axis=-1, keepdims=True) + EPS) * ng_ref[1:2, :]
        o_ref[...] = x_ref[...] + mod_ref[0, 2:3, :] * y


def _ffn_sublayer(x, ng_pre, ng_post, shift, scale, gate, w_up, w_conv, w_down, layer):
    bsz, length, d = x.shape
    m = bsz * length
    tm = _pick(length, (1024, 512, 256, 128))
    tf = _pick(D_FF, (512, 256, 128))
    nf = D_FF // tf
    bps = length // tm
    hb = tm // HALO
    nrow = m // HALO
    per_batch = shift.shape[0] > 1
    ng = jnp.stack([ng_pre, ng_post])
    mod = jnp.stack([shift, scale, gate], axis=1)
    out = pl.pallas_call(
        functools.partial(_ffn_kernel, blocks_per_seq=bps),
        grid=(m // tm, nf),
        in_specs=[pl.BlockSpec((tm, d), lambda i, j: (i, 0), pipeline_mode=pl.Buffered(1)),
                  pl.BlockSpec((HALO, d), lambda i, j: (jnp.maximum(i * hb - 1, 0), 0)),
                  pl.BlockSpec((HALO, d), lambda i, j: (jnp.minimum((i + 1) * hb, nrow - 1), 0)),
                  pl.BlockSpec((2, d), lambda i, j: (0, 0)),
                  pl.BlockSpec((1, 3, d), (lambda i, j: (i // bps, 0, 0)) if per_batch else (lambda i, j: (0, 0, 0))),
                  pl.BlockSpec((None, d, tf), lambda i, j: (layer, 0, j)),
                  pl.BlockSpec((None, d, tf), lambda i, j: (layer, 0, j + nf)),
                  pl.BlockSpec((None, FFN_CONV, tf), lambda i, j: (layer, 0, j)),
                  pl.BlockSpec((None, FFN_CONV, tf), lambda i, j: (layer, 0, j + nf)),
                  pl.BlockSpec((None, tf, d), lambda i, j: (layer, j, 0))],
        out_specs=pl.BlockSpec((tm, d), lambda i, j: (i, 0)),
        out_shape=jax.ShapeDtypeStruct((m, d), F32),
        scratch_shapes=[pltpu.VMEM((tm + 2 * HALO, d), BF16)],
        compiler_params=pltpu.CompilerParams(
            dimension_semantics=("parallel", "arbitrary"), vmem_limit_bytes=FFN_VMEM_LIMIT),
        name="ffn",
    )(x.reshape(m, d), x.reshape(m, d), x.reshape(m, d), ng, mod, w_up, w_up, w_conv, w_conv, w_down)
    return out.reshape(bsz, length, d)


def kernel(x, c, ctx, c_ctx, ada_w, ada_b, norm_g, w_in, hy_conv, hy_w1, hy_b1, hy_w2, hy_b2, hy_w3, hy_bias, lru_conv, lru_wa, lru_ba, lru_wx, lru_bx, lru_lam, mla_gq, mla_gkv, mla_wuq, mla_wukv, head_g, w_out, ffn_up, ffn_conv, ffn_down):
    depth = ada_w.shape[0]
    bsz, seq, _ = x.shape
    ctx_len = ctx.shape[1]
    rope_tabs = _rope_lane_tables(_rope_tables(seq))
    dft_lat, dft_ctx = _dft_tables(seq), _dft_tables(ctx_len)
    xc = ctx
    w_in_b = jnp.pad(w_in, ((0, 0), (0, 0), (0, NOPE_D - ROPE_D))).astype(BF16)
    wuq_pad = jax.vmap(_pad_heads_q)(mla_wuq).astype(BF16)
    wukv_b, w_out_b = mla_wukv.astype(BF16), w_out.astype(BF16)
    up_b, down_b = ffn_up.astype(BF16), ffn_down.astype(BF16)
    mods = _adaln(jnp.concatenate([c, c_ctx[None]], axis=0), ada_w, ada_b)

    def seq3(t, length):
        return None if t is None else t.reshape(bsz, length, t.shape[-1])

    for l in range(depth):
        need_ctx = l < depth - 1
        ng = norm_g[l]
        sh1, sc1, g1, sh2, sc2, g2 = jnp.split(mods[l, :bsz], N_MOD, axis=-1)
        csh1, csc1, cg1, csh2, csc2, cg2 = jnp.split(mods[l, bsz:], N_MOD, axis=-1)
        hy_f = (hy_w1[l], hy_b1[l], hy_w2[l], hy_b2[l], hy_w3[l])
        lru_p = (lru_conv[l], lru_wa[l], lru_ba[l], lru_wx[l], lru_bx[l], lru_lam[l])

        u = _proj_in(x, ng[0], sh1, sc1, w_in_b, l)
        uc = _proj_in(xc, ng[0], csh1, csc1, w_in_b, l)
        u3, uc3 = seq3(u, seq), seq3(uc, ctx_len)

        hc_sum, hc_end = _rglru(uc, ctx_len, *lru_p, jnp.zeros((bsz, 2, LRU_C), F32))
        q_c, kv_c, kr_c = [seq3(t, ctx_len) for t in
                           _proj_qkv(uc, ctx_len, mla_gq[l], wuq_pad, mla_gkv[l], wukv_b, l, None, with_q=need_ctx)]

        y_hy = _hyena_mix(u3, hy_conv[l], _filter_spectra(seq, *hy_f, dft_lat), hy_bias[l], dft_lat)
        h_sum, _ = _rglru(u, seq, *lru_p, hc_end)
        q, kv, kr = [seq3(t, seq) for t in
                     _proj_qkv(u, seq, mla_gq[l], wuq_pad, mla_gkv[l], wukv_b, l, rope_tabs)]
        y_mla = _attention(q, [(kv_c, kr_c), (kv, kr)])
        x = _mix_out(x, y_hy, h_sum, u, y_mla, head_g[l], w_out_b, l, ng[1], g1)

        if need_ctx:
            yc_hy = _hyena_mix(uc3, hy_conv[l], _filter_spectra(ctx_len, *hy_f, dft_ctx), hy_bias[l], dft_ctx)
            yc_mla = _attention(q_c, [(kv_c, kr_c)])
            xc = _mix_out(xc, yc_hy, hc_sum, uc, yc_mla, head_g[l], w_out_b, l, ng[1], cg1)

        x = _ffn_sublayer(x, ng[2], ng[3], sh2, sc2, g2, up_b, ffn_conv, down_b, l)
        if need_ctx:
            xc = _ffn_sublayer(xc, ng[2], ng[3], csh2, csc2, cg2, up_b, ffn_conv, down_b, l)
    return x
```

```python
import functools
import math

import jax
import jax.numpy as jnp
from jax import lax
from jax.experimental import pallas as pl
from jax.experimental.pallas import tpu as pltpu

F32 = jnp.float32
BF16 = jnp.bfloat16

D_MODEL = 2048
GRID_W = 64
HEAD_W = 128
HY_C = 512
HY_ORDER = 2
HY_SHORT = 3
HY_BANDS = 16
HY_FAST_DECAY = 0.3
HY_SLOW_DECAY = 1.5
HY_DECAY_TARGET = 1e-2
LRU_C = 512
LRU_BLOCKS = 4
LRU_BW = LRU_C // LRU_BLOCKS
LRU_CONV = 4
RG_C = 8.0
MLA_H = 8
Q_RANK = 512
KV_RANK = 256
NOPE_D = 128
ROPE_D = 64
V_D = 128
ROPE_BASE = 10000.0
MLA_SCALE = (NOPE_D + ROPE_D) ** -0.5
Q_SCALE = MLA_SCALE * math.log2(math.e)
D_MIX = HY_C + LRU_C + MLA_H * V_D
N_MIX_HEADS = D_MIX // HEAD_W
D_FF = 5632
FFN_CONV = 3
N_MOD = 6
EPS = 1e-6

OFF_HY = 0
OFF_LRU_G = OFF_HY + (HY_ORDER + 1) * HY_C
OFF_MLA_Q = OFF_LRU_G + LRU_C
OFF_LRU_X = OFF_MLA_Q + Q_RANK
OFF_MLA_KV = OFF_LRU_X + LRU_C
OFF_MLA_KR = OFF_MLA_KV + KV_RANK
D_IN = OFF_MLA_KR + ROPE_D

HEAD_QK = 2 * NOPE_D
VMEM_LIMIT = 48 * 1024 * 1024
FFN_VMEM_LIMIT = 56 * 1024 * 1024


def _pick(n, prefs):
    for p in prefs:
        if n % p == 0:
            return p
    return n


def _adaln_kernel(c_ref, w_ref, b_ref, o_ref):
    c = c_ref[...]
    a = (c * jax.nn.sigmoid(c)).astype(BF16)
    o_ref[...] = jnp.dot(a, w_ref[...].astype(BF16), preferred_element_type=F32) + b_ref[...]


def _adaln(cond, ada_w, ada_b):
    depth, d, n = ada_w.shape
    rows = cond.shape[0]
    rp = -(-rows // SUBLANES) * SUBLANES
    cond = jnp.pad(cond, ((0, rp - rows), (0, 0)))
    tn = 1024
    out = pl.pallas_call(
        _adaln_kernel,
        grid=(depth, n // tn),
        in_specs=[pl.BlockSpec((rp, d), lambda l, j: (0, 0)),
                  pl.BlockSpec((None, d, tn), lambda l, j: (l, 0, j)),
                  pl.BlockSpec((None, 1, tn), lambda l, j: (l, 0, j))],
        out_specs=pl.BlockSpec((None, rp, tn), lambda l, j: (l, 0, j)),
        out_shape=jax.ShapeDtypeStruct((depth, rp, n), F32),
        compiler_params=pltpu.CompilerParams(
            dimension_semantics=("parallel", "parallel"), vmem_limit_bytes=VMEM_LIMIT),
        name="adaln",
    )(cond, ada_w, ada_b[:, None])
    return out[:, :rows]


Q_SPLIT = 2

def _attn_kernel(*refs, n_seg):
    q_ref = refs[0]
    seg_refs = refs[1:1 + 3 * n_seg]
    o_ref = refs[1 + 3 * n_seg]
    k_scr, v_scr = refs[2 + 3 * n_seg:]

    @pl.when(pl.program_id(2) == 0)
    def _():
        off = 0
        for s in range(n_seg):
            kn_ref, v_ref, kr_ref = seg_refs[3 * s:3 * s + 3]
            n = kn_ref.shape[1]
            k_scr[off:off + n, 0:NOPE_D] = kn_ref[0]
            k_scr[off:off + n, NOPE_D:HEAD_QK] = kr_ref[0]
            v_scr[off:off + n, :] = v_ref[0]
            off += n

    sub = q_ref.shape[1] // Q_SPLIT
    for part in range(Q_SPLIT):
        rows = slice(part * sub, (part + 1) * sub)
        s = lax.dot_general(q_ref[0, rows], k_scr[...], (((1,), (1,)), ((), ())), preferred_element_type=F32)
        m = jnp.max(s, axis=-1, keepdims=True)
        p = jnp.exp2(s - m)
        l = jnp.sum(p, axis=-1, keepdims=True)
        o = jnp.dot(p.astype(BF16), v_scr[...], preferred_element_type=F32)
        o_ref[0, rows] = o / l


def _attention(q, segs):
    bsz, lq, _ = q.shape
    tq = _pick(lq, (512, 256, 128))
    lk = sum(kv.shape[1] for kv, _ in segs)
    in_specs = [pl.BlockSpec((1, tq, HEAD_QK), lambda b, h, i: (b, i, h))]
    args = [q]
    for kv, kr in segs:
        n = kv.shape[1]
        in_specs += [pl.BlockSpec((1, n, NOPE_D), lambda b, h, i: (b, 0, 2 * h)),
                     pl.BlockSpec((1, n, V_D), lambda b, h, i: (b, 0, 2 * h + 1)),
                     pl.BlockSpec((1, n, NOPE_D), lambda b, h, i: (b, 0, 0))]
        args += [kv, kv, kr]
    return pl.pallas_call(
        functools.partial(_attn_kernel, n_seg=len(segs)),
        grid=(bsz, MLA_H, lq // tq),
        in_specs=in_specs,
        out_specs=pl.BlockSpec((1, tq, V_D), lambda b, h, i: (b, i, h)),
        out_shape=jax.ShapeDtypeStruct((bsz, lq, MLA_H * V_D), F32),
        scratch_shapes=[pltpu.VMEM((lk, HEAD_QK), BF16), pltpu.VMEM((lk, V_D), BF16)],
        compiler_params=pltpu.CompilerParams(
            dimension_semantics=("parallel", "parallel", "arbitrary"), vmem_limit_bytes=VMEM_LIMIT),
        name="attn",
    )(*args)


LANES = 128
PGROUP = 2
STAGE_UNROLL = 8


def _fft_dims(length):
    n = 2 * length
    bn = 128 if length >= 1024 else 32
    return n // bn, bn, bn + SUBLANES


def _dft_tables(length):
    a_n, bn, _ = _fft_dims(length)
    n, half = a_n * bn, a_n // 2
    p = jnp.arange(a_n, dtype=jnp.int32)
    b = jnp.arange(bn, dtype=jnp.int32)
    nn = bn * p[None, None, :] + b[:, None, None]
    ang = (2.0 * math.pi / n) * ((p[None, :, None] * nn) % n).astype(F32)
    cr, ci = jnp.cos(ang), -jnp.sin(ang)
    m1 = jnp.concatenate([jnp.concatenate([cr[:, :, :half], -ci[:, :, :half]], axis=2),
                          jnp.concatenate([ci[:, :, :half], cr[:, :, :half]], axis=2)], axis=1)
    m1f = jnp.concatenate([cr, ci], axis=1)
    crt, cit = jnp.swapaxes(cr, 1, 2)[:, :half], jnp.swapaxes(ci, 1, 2)[:, :half]
    m4 = jnp.concatenate([jnp.concatenate([crt, cit], axis=2),
                          jnp.concatenate([-cit, crt], axis=2)], axis=1) / n
    ang2 = (2.0 * math.pi / bn) * ((b[:, None] * b[None, :]) % bn).astype(F32)
    wr, wi = jnp.cos(ang2), -jnp.sin(ang2)
    m2 = jnp.concatenate([jnp.concatenate([wr, -wi], axis=1), jnp.concatenate([wi, wr], axis=1)], axis=0)
    m3 = jnp.concatenate([jnp.concatenate([wr, wi], axis=1), jnp.concatenate([-wi, wr], axis=1)], axis=0)
    return tuple(t.astype(BF16) for t in (m1, m1f, m2, m3, m4))


def _stage1(src, rows_per_part, parts, m_ref, t2, a_n, bn, pitch):
    def body(b, carry):
        rhs = jnp.concatenate([src[pl.ds(part * rows_per_part * pitch + b, rows_per_part, stride=pitch), :]
                               for part in range(parts)], axis=0)
        t = jnp.dot(m_ref[b], rhs.astype(BF16), preferred_element_type=F32)
        t2[pl.ds(b, a_n, stride=pitch), :] = t[:a_n]
        t2[pl.ds(a_n * pitch + b, a_n, stride=pitch), :] = t[a_n:]
        return carry
    lax.fori_loop(0, bn, body, 0, unroll=STAGE_UNROLL)


def _stage2_rhs(t2, g, a_n, bn, pitch):
    cols = []
    for k in range(PGROUP):
        r0 = pl.multiple_of((g * PGROUP + k) * pitch, SUBLANES)
        cols.append(jnp.concatenate([t2[pl.ds(r0, bn), :], t2[pl.ds(a_n * pitch + r0, bn), :]], axis=0))
    return jnp.concatenate(cols, axis=1).astype(BF16)


def _fftconv_kernel(z_ref, spec_ref, m1_ref, m2_ref, m3_ref, m4_ref, y_ref, zb, t2, *, length):
    a_n, bn, pitch = _fft_dims(length)
    half = a_n // 2
    for bi in range(2):
        for a in range(half):
            zb[(bi * half + a) * pitch:(bi * half + a) * pitch + bn, :] = z_ref[bi, a * bn:(a + 1) * bn, :]
    _stage1(zb, half, 2, m1_ref, t2, a_n, bn, pitch)

    def mid(g, carry):
        x = jnp.dot(m2_ref[...], _stage2_rhs(t2, g, a_n, bn, pitch), preferred_element_type=F32)
        k = spec_ref[0, 0, g]
        xr, xi, kr, ki = x[:bn], x[bn:], k[:bn], k[bn:]
        y = jnp.concatenate([xr * kr - xi * ki, xr * ki + xi * kr], axis=0).astype(BF16)
        u = jnp.dot(m3_ref[...], y, preferred_element_type=F32)
        for kk in range(PGROUP):
            r0 = pl.multiple_of((g * PGROUP + kk) * pitch, SUBLANES)
            t2[pl.ds(r0, bn), :] = u[:bn, kk * LANES:(kk + 1) * LANES]
            t2[pl.ds(a_n * pitch + r0, bn), :] = u[bn:, kk * LANES:(kk + 1) * LANES]
        return carry
    lax.fori_loop(0, a_n // PGROUP, mid, 0, unroll=2)

    def last(b, carry):
        rhs = jnp.concatenate([t2[pl.ds(b, a_n, stride=pitch), :],
                               t2[pl.ds(a_n * pitch + b, a_n, stride=pitch), :]], axis=0)
        y = jnp.dot(m4_ref[b], rhs.astype(BF16), preferred_element_type=F32)
        zb[pl.ds(b, half, stride=pitch), :] = y[:half]
        zb[pl.ds(half * pitch + b, half, stride=pitch), :] = y[half:]
        return carry
    lax.fori_loop(0, bn, last, 0, unroll=STAGE_UNROLL)
    for bi in range(2):
        for a in range(half):
            y_ref[bi, a * bn:(a + 1) * bn, :] = zb[(bi * half + a) * pitch:(bi * half + a) * pitch + bn, :]


def _fftconv(z, spec, tables):
    bsz, length, chans = z.shape
    assert bsz == 2, "the two batch rows are packed as one complex signal"
    a_n, bn, pitch = _fft_dims(length)
    m1, _, m2, m3, m4 = tables
    const3 = lambda c: (0, 0, 0)
    const2 = lambda c: (0, 0)
    once = pl.Buffered(1)
    return pl.pallas_call(
        functools.partial(_fftconv_kernel, length=length),
        grid=(chans // LANES,),
        in_specs=[pl.BlockSpec((2, length, LANES), lambda c: (0, 0, c)),
                  pl.BlockSpec((1, 1, a_n // PGROUP, 2 * bn, PGROUP * LANES), lambda c: (0, c, 0, 0, 0),
                               pipeline_mode=once),
                  pl.BlockSpec(m1.shape, const3, pipeline_mode=once),
                  pl.BlockSpec(m2.shape, const2, pipeline_mode=once),
                  pl.BlockSpec(m3.shape, const2, pipeline_mode=once),
                  pl.BlockSpec(m4.shape, const3, pipeline_mode=once)],
        out_specs=pl.BlockSpec((2, length, LANES), lambda c: (0, 0, c)),
        out_shape=jax.ShapeDtypeStruct(z.shape, F32),
        scratch_shapes=[pltpu.VMEM((a_n * pitch, LANES), F32), pltpu.VMEM((2 * a_n * pitch, LANES), F32)],
        compiler_params=pltpu.CompilerParams(dimension_semantics=("parallel",), vmem_limit_bytes=VMEM_LIMIT),
        name="fftconv",
    )(z, spec[None], m1, m2, m3, m4)


def _split_bf16(x):
    hi = x.astype(BF16)
    return hi, (x - hi.astype(F32)).astype(BF16)


def _dot_split(a, b):
    ah, al = _split_bf16(a)
    bh, bl = _split_bf16(b)
    d = functools.partial(jnp.dot, preferred_element_type=F32)
    return d(ah, bh) + (d(al, bh) + d(ah, bl))


def _filtspec_kernel(hf_ref, hb_ref, w3f_ref, w3b_ref, dec_ref, m1_ref, m2_ref, spec_ref, kb, t2, *, length):
    a_n, bn, pitch = _fft_dims(length)
    half = a_n // 2
    row = lax.broadcasted_iota(jnp.int32, (length, 1), 0).astype(F32)
    delta = dec_ref[...]
    kf = _dot_split(hf_ref[...], w3f_ref[...]) * jnp.exp(-(row / length) * delta)
    kbk = _dot_split(hb_ref[...], w3b_ref[...]) * jnp.exp(-((length - row) / length) * delta)
    kbk = jnp.where(row == 0.0, 0.0, kbk)
    norm = jnp.sum(jnp.abs(kf), axis=0, keepdims=True) + jnp.sum(jnp.abs(kbk), axis=0, keepdims=True)
    kf, kbk = kf / norm, kbk / norm
    for a in range(half):
        kb[a * pitch:a * pitch + bn, :] = kf[a * bn:(a + 1) * bn]
        kb[(half + a) * pitch:(half + a) * pitch + bn, :] = kbk[a * bn:(a + 1) * bn]
    _stage1(kb, a_n, 1, m1_ref, t2, a_n, bn, pitch)

    def mid(g, carry):
        spec_ref[0, 0, g] = jnp.dot(m2_ref[...], _stage2_rhs(t2, g, a_n, bn, pitch), preferred_element_type=F32)
        return carry
    lax.fori_loop(0, a_n // PGROUP, mid, 0, unroll=2)


def _filter_spectra(length, w1, b1, w2, b2, w3, tables):
    a_n, bn, pitch = _fft_dims(length)
    hp = lax.Precision.HIGHEST
    t_f = jnp.arange(length, dtype=F32) / length
    t_b = (length - jnp.arange(length, dtype=F32)) / length
    band = jnp.arange(1, HY_BANDS + 1, dtype=F32)

    def hidden(t):
        ang = 2.0 * math.pi * t[:, None] * band
        feats = jnp.concatenate([t[:, None], jnp.sin(ang), jnp.cos(ang)], axis=-1)
        hid = jnp.sin(jnp.dot(feats, w1, precision=hp) + b1)
        return jnp.sin(jnp.dot(hid, w2, precision=hp) + b2)

    deltas = jnp.abs(jnp.linspace(math.log(HY_DECAY_TARGET) / HY_SLOW_DECAY,
                                  math.log(HY_DECAY_TARGET) / HY_FAST_DECAY, HY_C, dtype=F32))[None]
    _, m1f, m2, _, _ = tables
    nct = HY_C // LANES
    hid_f, hid_b = hidden(t_f), hidden(t_b)
    hidden_w = hid_f.shape[1]
    once = pl.Buffered(1)
    return pl.pallas_call(
        functools.partial(_filtspec_kernel, length=length),
        grid=(HY_ORDER, nct),
        in_specs=[pl.BlockSpec((length, hidden_w), lambda o, c: (0, 0), pipeline_mode=once),
                  pl.BlockSpec((length, hidden_w), lambda o, c: (0, 0), pipeline_mode=once),
                  pl.BlockSpec((hidden_w, LANES), lambda o, c: (0, o * 2 * nct + c)),
                  pl.BlockSpec((hidden_w, LANES), lambda o, c: (0, o * 2 * nct + nct + c)),
                  pl.BlockSpec((1, LANES), lambda o, c: (0, c)),
                  pl.BlockSpec(m1f.shape, lambda o, c: (0, 0, 0), pipeline_mode=once),
                  pl.BlockSpec(m2.shape, lambda o, c: (0, 0), pipeline_mode=once)],
        out_specs=pl.BlockSpec((1, 1, a_n // PGROUP, 2 * bn, PGROUP * LANES), lambda o, c: (o, c, 0, 0, 0)),
        out_shape=jax.ShapeDtypeStruct((HY_ORDER, nct, a_n // PGROUP, 2 * bn, PGROUP * LANES), F32),
        scratch_shapes=[pltpu.VMEM((a_n * pitch, LANES), F32), pltpu.VMEM((2 * a_n * pitch, LANES), F32)],
        compiler_params=pltpu.CompilerParams(
            dimension_semantics=("parallel", "parallel"), vmem_limit_bytes=VMEM_LIMIT),
        name="filtspec",
    )(hid_f, hid_b, w3, w3, deltas, m1f, m2)


def _hygate_kernel(p_ref, w_ref, *rest, first):
    x = p_ref[0]
    n = x.shape[0]
    row = lax.broadcasted_iota(jnp.int32, x.shape, 0)
    prev = jnp.where(row == 0, 0.0, pltpu.roll(x, 1, axis=0))
    nxt = jnp.where(row == n - 1, 0.0, pltpu.roll(x, n - 1, axis=0))
    c = w_ref[0:1, :] * prev + w_ref[1:2, :] * x + w_ref[2:3, :] * nxt
    if first:
        (o_ref,) = rest
        o_ref[0] = c
    else:
        y_ref, z_ref, b_ref, o_ref = rest
        o_ref[0] = c * (y_ref[0] + z_ref[0] * b_ref[...])


def _hygate(u, part, conv_w, y=None, z=None, bias=None):
    bsz, length, _ = u.shape
    nct = HY_C // LANES
    blk = lambda b, c: (b, 0, c)
    in_specs = [pl.BlockSpec((1, length, LANES), lambda b, c: (b, 0, part * nct + c)),
                pl.BlockSpec((HY_SHORT, LANES), lambda b, c: (0, part * nct + c))]
    args = [u, conv_w]
    if y is not None:
        in_specs += [pl.BlockSpec((1, length, LANES), blk), pl.BlockSpec((1, length, LANES), blk),
                     pl.BlockSpec((1, LANES), lambda b, c: (0, c))]
        args += [y, z, bias]
    return pl.pallas_call(
        functools.partial(_hygate_kernel, first=y is None),
        grid=(bsz, nct),
        in_specs=in_specs,
        out_specs=pl.BlockSpec((1, length, LANES), blk),
        out_shape=jax.ShapeDtypeStruct((bsz, length, HY_C), F32),
        compiler_params=pltpu.CompilerParams(
            dimension_semantics=("parallel", "parallel"), vmem_limit_bytes=VMEM_LIMIT),
        name="hygate",
    )(*args)


def _hyena_mix(u, conv_w, spec, bias, tables):
    z = _hygate(u, 0, conv_w)
    for o in range(HY_ORDER):
        y = _fftconv(z, spec[o], tables)
        z = _hygate(u, o + 1, conv_w, y, z, bias[o:o + 1])
    return z


SUBLANES = 8
PAD = LRU_CONV - 1
SCAN_UNROLL = 8


def _lru_kernel(x_ref, wg_ref, pv_ref, h0_ref, o_ref, end_ref, xp, a_scr, b_scr, o_scr, *, steps, chunk):
    sub = lax.broadcasted_iota(jnp.int32, (SUBLANES, LRU_BW), 0)

    def gather(j, carry):
        xp[PAD + j] = x_ref[pl.ds(j, SUBLANES, stride=steps), :]
        return carry

    lax.fori_loop(0, steps, gather, 0, unroll=8)
    for k in range(PAD):
        tail = pltpu.roll(xp[steps + k], 1, axis=0)
        xp[k] = jnp.where(sub == 0, 0.0, tail)
        head = pltpu.roll(xp[PAD + k], SUBLANES - 1, axis=0)
        xp[PAD + steps + k] = jnp.where(sub == SUBLANES - 1, 0.0, head)

    n_chunks = steps // chunk
    for d in range(2):
        ba = pv_ref[d, 0:1, :]
        bx = pv_ref[d, 1:2, :]
        lam = pv_ref[d, 2:3, :]
        neg_c_sp = -RG_C * jax.nn.softplus(-lam)
        taps = [pv_ref[d, 3 + k:4 + k, :].reshape(1, 1, LRU_BW) for k in range(LRU_CONV)]
        w_gate = wg_ref[d, 0]
        base = 0 if d == 0 else PAD

        def gates(c, carry):
            j0 = pl.multiple_of(c * chunk, chunk)
            xc = taps[0] * xp[pl.ds(j0 + base, chunk)]
            for k in range(1, LRU_CONV):
                xc = xc + taps[k] * xp[pl.ds(j0 + base + k, chunk)]
            xc2 = xc.reshape(chunk * SUBLANES, LRU_BW)
            g = jnp.dot(xc2.astype(BF16), w_gate, preferred_element_type=F32)
            r = 0.5 * jnp.tanh(0.5 * (g[:, :LRU_BW] + ba)) + 0.5
            i = 0.5 * jnp.tanh(0.5 * (g[:, LRU_BW:] + bx)) + 0.5
            log_a = neg_c_sp * r
            a = jnp.exp(log_a)
            t = jnp.tanh(log_a)
            gap = -2.0 * t / (1.0 - t)
            root = jnp.where(gap > 0.0, gap * lax.rsqrt(gap), 0.0)
            b = root * (i * xc2)
            a_scr[pl.ds(j0, chunk)] = a.reshape(chunk, SUBLANES, LRU_BW)
            b_scr[pl.ds(j0, chunk)] = b.reshape(chunk, SUBLANES, LRU_BW)
            return carry

        lax.fori_loop(0, n_chunks, gates, 0)

        def steps8(k, carry):
            h, p = carry
            base = pl.multiple_of(k * SCAN_UNROLL if d == 0 else steps - SCAN_UNROLL * (k + 1), SCAN_UNROLL)
            for u in (range(SCAN_UNROLL) if d == 0 else range(SCAN_UNROLL - 1, -1, -1)):
                a = a_scr[base + u]
                h = a * h + b_scr[base + u]
                p = a * p
                b_scr[base + u] = h
                a_scr[base + u] = p
            return h, p

        h_loc, p_loc = lax.fori_loop(
            0, steps // SCAN_UNROLL, steps8,
            (jnp.zeros((SUBLANES, LRU_BW), F32), jnp.ones((SUBLANES, LRU_BW), F32)))

        carry = h0_ref[0, d:d + 1, :]
        cin = jnp.zeros((SUBLANES, LRU_BW), F32)
        order = range(SUBLANES) if d == 0 else range(SUBLANES - 1, -1, -1)
        for s in order:
            cin = jnp.where(sub == s, jnp.broadcast_to(carry, (SUBLANES, LRU_BW)), cin)
            carry = p_loc[s:s + 1, :] * carry + h_loc[s:s + 1, :]
        end_ref[0, d:d + 1, :] = carry

        def fix(c, carry_):
            j0 = pl.multiple_of(c * chunk, chunk)
            h = b_scr[pl.ds(j0, chunk)] + a_scr[pl.ds(j0, chunk)] * cin[None]
            if d == 0:
                o_scr[pl.ds(j0, chunk)] = h
            else:
                h = h + o_scr[pl.ds(j0, chunk)]
                for jj in range(chunk):
                    o_ref[pl.ds(j0 + jj, SUBLANES, stride=steps), :] = h[jj]
            return carry_

        lax.fori_loop(0, n_chunks, fix, 0)


def _rglru(u, length, lconv, lwa, lba, lwx, lbx, llam, h0):
    m = u.shape[0]
    bsz = m // length
    steps = length // SUBLANES
    chunk = min(32, steps)
    wg = jnp.concatenate([lwa, lwx], axis=-1).astype(BF16)
    pv = jnp.concatenate([lba[:, None], lbx[:, None], llam[:, None], lconv,
                          jnp.zeros((2, 1, LRU_C), F32)], axis=1)
    seg = pltpu.VMEM((steps, SUBLANES, LRU_BW), F32)
    return pl.pallas_call(
        functools.partial(_lru_kernel, steps=steps, chunk=chunk),
        grid=(bsz, LRU_BLOCKS),
        in_specs=[pl.BlockSpec((length, LRU_BW), lambda b, n: (b, OFF_LRU_X // LRU_BW + n)),
                  pl.BlockSpec((2, 1, LRU_BW, 2 * LRU_BW), lambda b, n: (0, n, 0, 0)),
                  pl.BlockSpec((2, SUBLANES, LRU_BW), lambda b, n: (0, 0, n)),
                  pl.BlockSpec((1, 2, LRU_BW), lambda b, n: (b, 0, n))],
        out_specs=[pl.BlockSpec((length, LRU_BW), lambda b, n: (b, n)),
                   pl.BlockSpec((1, 2, LRU_BW), lambda b, n: (b, 0, n))],
        out_shape=[jax.ShapeDtypeStruct((m, LRU_C), F32),
                   jax.ShapeDtypeStruct((bsz, 2, LRU_C), F32)],
        scratch_shapes=[pltpu.VMEM((steps + 2 * PAD, SUBLANES, LRU_BW), F32), seg, seg, seg],
        compiler_params=pltpu.CompilerParams(
            dimension_semantics=("parallel", "parallel"), vmem_limit_bytes=VMEM_LIMIT),
        name="rglru",
    )(u, wg, pv, h0)


def _rope_tables(length):
    rows = length // GRID_W
    row = jnp.repeat(jnp.arange(rows, dtype=F32), GRID_W)
    col = jnp.tile(jnp.arange(GRID_W, dtype=F32), rows)
    n_freq = ROPE_D // 4
    inv = ROPE_BASE ** (-jnp.arange(n_freq, dtype=F32) / n_freq)
    ang = jnp.concatenate([row[:, None] * inv, col[:, None] * inv], axis=-1)
    return jnp.cos(ang), jnp.sin(ang)


def _rope_lane_tables(rope):
    cos, sin = rope
    z = jnp.zeros_like(cos)
    return (jnp.concatenate([cos, cos, z, z], axis=1), jnp.concatenate([-sin, z, z, z], axis=1),
            jnp.concatenate([z, sin, z, z], axis=1))


def _pad_heads_q(wuq):
    w = wuq.reshape(Q_RANK, MLA_H, NOPE_D + ROPE_D)
    w = jnp.pad(w, ((0, 0), (0, 0), (0, HEAD_QK - NOPE_D - ROPE_D)))
    return w.reshape(Q_RANK, MLA_H * HEAD_QK)


def _rms(xv, g):
    return xv * lax.rsqrt(jnp.mean(xv * xv, axis=-1, keepdims=True) + EPS) * g


ROW_SPLIT = 2


def _proj_in_kernel(x_ref, g_ref, mod_ref, w_ref, o_ref):
    sub = x_ref.shape[0] // ROW_SPLIT
    for part in range(ROW_SPLIT):
        rows = slice(part * sub, (part + 1) * sub)
        y = _rms(x_ref[rows, :], g_ref[...])
        h = (y * (1.0 + mod_ref[0, 1:2, :]) + mod_ref[0, 0:1, :]).astype(BF16)
        o_ref[rows, :] = jnp.dot(h, w_ref[...], preferred_element_type=F32)


def _proj_in(x, g, shift, scale, w, layer):
    bsz, length, d = x.shape
    m, n = bsz * length, w.shape[2]
    tm = _pick(length, (512, 256, 128))
    bps = length // tm
    mod = jnp.stack([shift, scale], axis=1)
    mod_idx = (lambda i: (i // bps, 0, 0)) if shift.shape[0] > 1 else (lambda i: (0, 0, 0))
    return pl.pallas_call(
        _proj_in_kernel,
        grid=(m // tm,),
        in_specs=[pl.BlockSpec((tm, d), lambda i: (i, 0)),
                  pl.BlockSpec((1, d), lambda i: (0, 0)),
                  pl.BlockSpec((1, 2, d), mod_idx),
                  pl.BlockSpec((None, d, n), lambda i: (layer, 0, 0), pipeline_mode=pl.Buffered(1))],
        out_specs=pl.BlockSpec((tm, n), lambda i: (i, 0)),
        out_shape=jax.ShapeDtypeStruct((m, n), F32),
        compiler_params=pltpu.CompilerParams(dimension_semantics=("parallel",), vmem_limit_bytes=VMEM_LIMIT),
        name="proj_in",
    )(x.reshape(m, d), g[None], mod, w)


def _proj_qkv_kernel(*refs, rope, with_q):
    refs = list(refs)
    uq_ref = refs.pop(0) if with_q else None
    ukv_ref, ukr_ref = refs.pop(0), refs.pop(0)
    gq_ref, wq_ref = (refs.pop(0), refs.pop(0)) if with_q else (None, None)
    gkv_ref, wkv_ref = refs.pop(0), refs.pop(0)
    cos_ref, sa_ref, sb_ref = (refs.pop(0), refs.pop(0), refs.pop(0)) if rope else (None, None, None)
    q_ref = refs.pop(0) if with_q else None
    kv_ref, kr_ref = refs

    def rot(blk):
        if not rope:
            return blk
        return (blk * cos_ref[...] + pltpu.roll(blk, LANES - 32, axis=1) * sa_ref[...]
                + pltpu.roll(blk, 32, axis=1) * sb_ref[...])

    if with_q:
        nq = _rms(uq_ref[...], gq_ref[...]).astype(BF16)
        q = jnp.dot(nq, wq_ref[...], preferred_element_type=F32)
        for hd in range(MLA_H):
            lo = hd * HEAD_QK
            q_ref[:, lo:lo + NOPE_D] = (q[:, lo:lo + NOPE_D] * Q_SCALE).astype(BF16)
            q_ref[:, lo + NOPE_D:lo + HEAD_QK] = (rot(q[:, lo + NOPE_D:lo + HEAD_QK]) * Q_SCALE).astype(BF16)
    nkv = _rms(ukv_ref[...], gkv_ref[...]).astype(BF16)
    kv_ref[...] = jnp.dot(nkv, wkv_ref[...], preferred_element_type=F32).astype(BF16)
    kr_ref[...] = rot(ukr_ref[...]).astype(BF16)


def _proj_qkv(u, length, gq, wuq_pad, gkv, wukv, layer, rope_tabs, with_q=True):
    m = u.shape[0]
    tm = _pick(length, (512, 256, 128))
    bps = length // tm
    row = lambda i: (i, 0)
    const = lambda i: (0, 0)
    wsel = lambda i: (layer, 0, 0)
    in_specs, args = [], []
    if with_q:
        in_specs.append(pl.BlockSpec((tm, Q_RANK), lambda i: (i, OFF_MLA_Q // Q_RANK)))
        args.append(u)
    in_specs += [pl.BlockSpec((tm, KV_RANK), lambda i: (i, OFF_MLA_KV // KV_RANK)),
                 pl.BlockSpec((tm, LANES), lambda i: (i, OFF_MLA_KR // LANES))]
    args += [u, u]
    if with_q:
        in_specs += [pl.BlockSpec((1, Q_RANK), const), pl.BlockSpec((None,) + wuq_pad.shape[1:], wsel)]
        args += [gq[None], wuq_pad]
    in_specs += [pl.BlockSpec((1, KV_RANK), const), pl.BlockSpec((None,) + wukv.shape[1:], wsel)]
    args += [gkv[None], wukv]
    if rope_tabs is not None:
        in_specs += [pl.BlockSpec((tm, LANES), lambda i: (i % bps, 0))] * 3
        args += list(rope_tabs)
    out_specs = [pl.BlockSpec((tm, MLA_H * HEAD_QK), row), pl.BlockSpec((tm, LANES), row)]
    out_shape = [jax.ShapeDtypeStruct((m, MLA_H * HEAD_QK), BF16), jax.ShapeDtypeStruct((m, LANES), BF16)]
    if with_q:
        out_specs.insert(0, pl.BlockSpec((tm, MLA_H * HEAD_QK), row))
        out_shape.insert(0, jax.ShapeDtypeStruct((m, MLA_H * HEAD_QK), BF16))
    outs = pl.pallas_call(
        functools.partial(_proj_qkv_kernel, rope=rope_tabs is not None, with_q=with_q),
        grid=(m // tm,),
        in_specs=in_specs, out_specs=out_specs, out_shape=out_shape,
        compiler_params=pltpu.CompilerParams(dimension_semantics=("parallel",), vmem_limit_bytes=VMEM_LIMIT),
        name="proj_qkv",
    )(*args)
    return outs if with_q else [None] + list(outs)


def _mix_kernel(yhy_ref, hs_ref, gate_ref, ymla_ref, hg_ref, w_ref, x_ref, ng_ref, mod_ref, o_ref):
    sub = x_ref.shape[0] // ROW_SPLIT
    for part in range(ROW_SPLIT):
        rows = slice(part * sub, (part + 1) * sub)
        y = jnp.concatenate([yhy_ref[rows, :], jax.nn.gelu(gate_ref[rows, :], approximate=True) * hs_ref[rows, :],
                             ymla_ref[rows, :]], axis=1)
        heads = [_rms(y[:, hd * HEAD_W:(hd + 1) * HEAD_W], hg_ref[:, hd * HEAD_W:(hd + 1) * HEAD_W]).astype(BF16)
                 for hd in range(N_MIX_HEADS)]
        acc = jnp.dot(jnp.concatenate(heads, axis=1), w_ref[...], preferred_element_type=F32)
        o_ref[rows, :] = x_ref[rows, :] + mod_ref[0] * _rms(acc, ng_ref[...])


def _mix_out(x, y_hy, h_sum, u, y_mla, head_g, w_out, layer, ng_post, gate):
    bsz, length, d = x.shape
    m = bsz * length
    tm = _pick(length, (512, 256, 128))
    bps = length // tm
    row = lambda i: (i, 0)
    const = lambda i: (0, 0)
    mod_idx = (lambda i: (i // bps, 0, 0)) if gate.shape[0] > 1 else (lambda i: (0, 0, 0))
    out = pl.pallas_call(
        _mix_kernel,
        grid=(m // tm,),
        in_specs=[pl.BlockSpec((tm, HY_C), row), pl.BlockSpec((tm, LRU_C), row),
                  pl.BlockSpec((tm, LRU_C), lambda i: (i, OFF_LRU_G // LRU_C)),
                  pl.BlockSpec((tm, MLA_H * V_D), row),
                  pl.BlockSpec((1, D_MIX), const),
                  pl.BlockSpec((None,) + w_out.shape[1:], lambda i: (layer, 0, 0), pipeline_mode=pl.Buffered(1)),
                  pl.BlockSpec((tm, d), row), pl.BlockSpec((1, d), const), pl.BlockSpec((1, 1, d), mod_idx)],
        out_specs=pl.BlockSpec((tm, d), row),
        out_shape=jax.ShapeDtypeStruct((m, d), F32),
        compiler_params=pltpu.CompilerParams(dimension_semantics=("parallel",), vmem_limit_bytes=VMEM_LIMIT),
        name="mix_out",
    )(y_hy.reshape(m, HY_C), h_sum.reshape(m, LRU_C), u, y_mla.reshape(m, MLA_H * V_D), head_g[None], w_out,
      x.reshape(m, d), ng_post[None], gate[:, None])
    return out.reshape(bsz, length, d)


HALO = 16


def _ffn_kernel(x_ref, xprev_ref, xnext_ref, ng_ref, mod_ref, wg_ref, wv_ref, cg_ref, cv_ref, wd_ref,
                o_ref, hbuf, *, blocks_per_seq):
    i = pl.program_id(0)
    j = pl.program_id(1)
    tm = x_ref.shape[0]

    def norm_mod(xv):
        y = xv * lax.rsqrt(jnp.mean(xv * xv, axis=-1, keepdims=True) + EPS) * ng_ref[0:1, :]
        return (y * (1.0 + mod_ref[0, 1:2, :]) + mod_ref[0, 0:1, :]).astype(BF16)

    @pl.when(j == 0)
    def _():
        first = (i % blocks_per_seq) == 0
        last = (i % blocks_per_seq) == blocks_per_seq - 1
        hp = norm_mod(xprev_ref[...])
        hn = norm_mod(xnext_ref[...])
        hbuf[0:HALO] = jnp.where(first, jnp.zeros_like(hp), hp)
        hbuf[HALO:HALO + tm] = norm_mod(x_ref[...])
        hbuf[HALO + tm:] = jnp.where(last, jnp.zeros_like(hn), hn)
        o_ref[...] = jnp.zeros_like(o_ref)

    hb = hbuf[...]

    def conv(up, c_ref):
        prev = pltpu.roll(up, 1, axis=0)[HALO:HALO + tm]
        nxt = pltpu.roll(up, tm + 2 * HALO - 1, axis=0)[HALO:HALO + tm]
        return c_ref[0:1, :] * prev + c_ref[1:2, :] * up[HALO:HALO + tm] + c_ref[2:3, :] * nxt

    gate = conv(jnp.dot(hb, wg_ref[...], preferred_element_type=F32), cg_ref)
    val = conv(jnp.dot(hb, wv_ref[...], preferred_element_type=F32), cv_ref)
    act = (jax.nn.gelu(gate, approximate=True) * val).astype(BF16)
    o_ref[...] += jnp.dot(act, wd_ref[...], preferred_element_type=F32)

    @pl.when(j == pl.num_programs(1) - 1)
    def _():
        y = o_ref[...]
        y = y * lax.rsqrt(jnp.mean(y * y, axis=-1, keepdims=True) + EPS) * ng_ref[1:2, :]
        o_ref[...] = x_ref[...] + mod_ref[0, 2:3, :] * y


def _ffn_sublayer(x, ng_pre, ng_post, shift, scale, gate, w_up, w_conv, w_down, layer):
    bsz, length, d = x.shape
    m = bsz * length
    tm = _pick(length, (1024, 512, 256, 128))
    tf = _pick(D_FF, (512, 256, 128))
    nf = D_FF // tf
    bps = length // tm
    hb = tm // HALO
    nrow = m // HALO
    per_batch = shift.shape[0] > 1
    ng = jnp.stack([ng_pre, ng_post])
    mod = jnp.stack([shift, scale, gate], axis=1)
    out = pl.pallas_call(
        functools.partial(_ffn_kernel, blocks_per_seq=bps),
        grid=(m // tm, nf),
        in_specs=[pl.BlockSpec((tm, d), lambda i, j: (i, 0), pipeline_mode=pl.Buffered(1)),
                  pl.BlockSpec((HALO, d), lambda i, j: (jnp.maximum(i * hb - 1, 0), 0)),
                  pl.BlockSpec((HALO, d), lambda i, j: (jnp.minimum((i + 1) * hb, nrow - 1), 0)),
                  pl.BlockSpec((2, d), lambda i, j: (0, 0)),
                  pl.BlockSpec((1, 3, d), (lambda i, j: (i // bps, 0, 0)) if per_batch else (lambda i, j: (0, 0, 0))),
                  pl.BlockSpec((None, d, tf), lambda i, j: (layer, 0, j)),
                  pl.BlockSpec((None, d, tf), lambda i, j: (layer, 0, j + nf)),
                  pl.BlockSpec((None, FFN_CONV, tf), lambda i, j: (layer, 0, j)),
                  pl.BlockSpec((None, FFN_CONV, tf), lambda i, j: (layer, 0, j + nf)),
                  pl.BlockSpec((None, tf, d), lambda i, j: (layer, j, 0))],
        out_specs=pl.BlockSpec((tm, d), lambda i, j: (i, 0)),
        out_shape=jax.ShapeDtypeStruct((m, d), F32),
        scratch_shapes=[pltpu.VMEM((tm + 2 * HALO, d), BF16)],
        compiler_params=pltpu.CompilerParams(
            dimension_semantics=("parallel", "arbitrary"), vmem_limit_bytes=FFN_VMEM_LIMIT),
        name="ffn",
    )(x.reshape(m, d), x.reshape(m, d), x.reshape(m, d), ng, mod, w_up, w_up, w_conv, w_conv, w_down)
    return out.reshape(bsz, length, d)


def kernel(x, c, ctx, c_ctx, ada_w, ada_b, norm_g, w_in, hy_conv, hy_w1, hy_b1, hy_w2, hy_b2, hy_w3, hy_bias, lru_conv, lru_wa, lru_ba, lru_wx, lru_bx, lru_lam, mla_gq, mla_gkv, mla_wuq, mla_wukv, head_g, w_out, ffn_up, ffn_conv, ffn_down):
    depth = ada_w.shape[0]
    bsz, seq, _ = x.shape
    ctx_len = ctx.shape[1]
    rope_tabs = _rope_lane_tables(_rope_tables(seq))
    dft_lat, dft_ctx = _dft_tables(seq), _dft_tables(ctx_len)
    xc = ctx
    w_in_b = jnp.pad(w_in, ((0, 0), (0, 0), (0, NOPE_D - ROPE_D))).astype(BF16)
    wuq_pad = jax.vmap(_pad_heads_q)(mla_wuq).astype(BF16)
    wukv_b, w_out_b = mla_wukv.astype(BF16), w_out.astype(BF16)
    up_b, down_b = ffn_up.astype(BF16), ffn_down.astype(BF16)
    mods = _adaln(jnp.concatenate([c, c_ctx[None]], axis=0), ada_w, ada_b)

    def seq3(t, length):
        return None if t is None else t.reshape(bsz, length, t.shape[-1])

    for l in range(depth):
        need_ctx = l < depth - 1
        ng = norm_g[l]
        sh1, sc1, g1, sh2, sc2, g2 = jnp.split(mods[l, :bsz], N_MOD, axis=-1)
        csh1, csc1, cg1, csh2, csc2, cg2 = jnp.split(mods[l, bsz:], N_MOD, axis=-1)
        hy_f = (hy_w1[l], hy_b1[l], hy_w2[l], hy_b2[l], hy_w3[l])
        lru_p = (lru_conv[l], lru_wa[l], lru_ba[l], lru_wx[l], lru_bx[l], lru_lam[l])

        u = _proj_in(x, ng[0], sh1, sc1, w_in_b, l)
        uc = _proj_in(xc, ng[0], csh1, csc1, w_in_b, l)
        u3, uc3 = seq3(u, seq), seq3(uc, ctx_len)

        hc_sum, hc_end = _rglru(uc, ctx_len, *lru_p, jnp.zeros((bsz, 2, LRU_C), F32))
        q_c, kv_c, kr_c = [seq3(t, ctx_len) for t in
                           _proj_qkv(uc, ctx_len, mla_gq[l], wuq_pad, mla_gkv[l], wukv_b, l, None, with_q=need_ctx)]

        y_hy = _hyena_mix(u3, hy_conv[l], _filter_spectra(seq, *hy_f, dft_lat), hy_bias[l], dft_lat)
        h_sum, _ = _rglru(u, seq, *lru_p, hc_end)
        q, kv, kr = [seq3(t, seq) for t in
                     _proj_qkv(u, seq, mla_gq[l], wuq_pad, mla_gkv[l], wukv_b, l, rope_tabs)]
        y_mla = _attention(q, [(kv_c, kr_c), (kv, kr)])
        x = _mix_out(x, y_hy, h_sum, u, y_mla, head_g[l], w_out_b, l, ng[1], g1)

        if need_ctx:
            yc_hy = _hyena_mix(uc3, hy_conv[l], _filter_spectra(ctx_len, *hy_f, dft_ctx), hy_bias[l], dft_ctx)
            yc_mla = _attention(q_c, [(kv_c, kr_c)])
            xc = _mix_out(xc, yc_hy, hc_sum, uc, yc_mla, head_g[l], w_out_b, l, ng[1], cg1)

        x = _ffn_sublayer(x, ng[2], ng[3], sh2, sc2, g2, up_b, ffn_conv, down_b, l)
        if need_ctx:
            xc = _ffn_sublayer(xc, ng[2], ng[3], csh2, csc2, cg2, up_b, ffn_conv, down_b, l)
    return x
```

```python
import functools
import math

import jax
import jax.numpy as jnp
from jax import lax
from jax.experimental import pallas as pl
from jax.experimental.pallas import tpu as pltpu

F32 = jnp.float32
BF16 = jnp.bfloat16

D_MODEL = 2048
GRID_W = 64
HEAD_W = 128
HY_C = 512
HY_ORDER = 2
HY_SHORT = 3
HY_BANDS = 16
HY_FAST_DECAY = 0.3
HY_SLOW_DECAY = 1.5
HY_DECAY_TARGET = 1e-2
LRU_C = 512
LRU_BLOCKS = 4
LRU_BW = LRU_C // LRU_BLOCKS
LRU_CONV = 4
RG_C = 8.0
MLA_H = 8
Q_RANK = 512
KV_RANK = 256
NOPE_D = 128
ROPE_D = 64
V_D = 128
ROPE_BASE = 10000.0
MLA_SCALE = (NOPE_D + ROPE_D) ** -0.5
Q_SCALE = MLA_SCALE * math.log2(math.e)
D_MIX = HY_C + LRU_C + MLA_H * V_D
N_MIX_HEADS = D_MIX // HEAD_W
D_FF = 5632
FFN_CONV = 3
N_MOD = 6
EPS = 1e-6

OFF_HY = 0
OFF_LRU_G = OFF_HY + (HY_ORDER + 1) * HY_C
OFF_MLA_Q = OFF_LRU_G + LRU_C
OFF_LRU_X = OFF_MLA_Q + Q_RANK
OFF_MLA_KV = OFF_LRU_X + LRU_C
OFF_MLA_KR = OFF_MLA_KV + KV_RANK
D_IN = OFF_MLA_KR + ROPE_D

HEAD_QK = 2 * NOPE_D
VMEM_LIMIT = 48 * 1024 * 1024
FFN_VMEM_LIMIT = 56 * 1024 * 1024


def _pick(n, prefs):
    for p in prefs:
        if n % p == 0:
            return p
    return n


def _adaln_kernel(c_ref, w_ref, b_ref, o_ref):
    c = c_ref[...]
    a = (c * jax.nn.sigmoid(c)).astype(BF16)
    o_ref[...] = jnp.dot(a, w_ref[...].astype(BF16), preferred_element_type=F32) + b_ref[...]


def _adaln(cond, ada_w, ada_b):
    depth, d, n = ada_w.shape
    rows = cond.shape[0]
    rp = -(-rows // SUBLANES) * SUBLANES
    cond = jnp.pad(cond, ((0, rp - rows), (0, 0)))
    tn = 1024
    out = pl.pallas_call(
        _adaln_kernel,
        grid=(depth, n // tn),
        in_specs=[pl.BlockSpec((rp, d), lambda l, j: (0, 0)),
                  pl.BlockSpec((None, d, tn), lambda l, j: (l, 0, j)),
                  pl.BlockSpec((None, 1, tn), lambda l, j: (l, 0, j))],
        out_specs=pl.BlockSpec((None, rp, tn), lambda l, j: (l, 0, j)),
        out_shape=jax.ShapeDtypeStruct((depth, rp, n), F32),
        compiler_params=pltpu.CompilerParams(
            dimension_semantics=("parallel", "parallel"), vmem_limit_bytes=VMEM_LIMIT),
        name="adaln",
    )(cond, ada_w, ada_b[:, None])
    return out[:, :rows]


Q_SPLIT = 2

def _attn_kernel(*refs, n_seg):
    q_ref = refs[0]
    seg_refs = refs[1:1 + 3 * n_seg]
    o_ref = refs[1 + 3 * n_seg]
    k_scr, v_scr = refs[2 + 3 * n_seg:]

    @pl.when(pl.program_id(2) == 0)
    def _():
        off = 0
        for s in range(n_seg):
            kn_ref, v_ref, kr_ref = seg_refs[3 * s:3 * s + 3]
            n = kn_ref.shape[1]
            k_scr[off:off + n, 0:NOPE_D] = kn_ref[0]
            k_scr[off:off + n, NOPE_D:HEAD_QK] = kr_ref[0]
            v_scr[off:off + n, :] = v_ref[0]
            off += n

    sub = q_ref.shape[1] // Q_SPLIT
    for part in range(Q_SPLIT):
        rows = slice(part * sub, (part + 1) * sub)
        s = lax.dot_general(q_ref[0, rows], k_scr[...], (((1,), (1,)), ((), ())), preferred_element_type=F32)
        m = jnp.max(s, axis=-1, keepdims=True)
        p = jnp.exp2(s - m)
        l = jnp.sum(p, axis=-1, keepdims=True)
        o = jnp.dot(p.astype(BF16), v_scr[...], preferred_element_type=F32)
        o_ref[0, rows] = o / l


def _attention(q, segs):
    bsz, lq, _ = q.shape
    tq = _pick(lq, (512, 256, 128))
    lk = sum(kv.shape[1] for kv, _ in segs)
    in_specs = [pl.BlockSpec((1, tq, HEAD_QK), lambda b, h, i: (b, i, h))]
    args = [q]
    for kv, kr in segs:
        n = kv.shape[1]
        in_specs += [pl.BlockSpec((1, n, NOPE_D), lambda b, h, i: (b, 0, 2 * h)),
                     pl.BlockSpec((1, n, V_D), lambda b, h, i: (b, 0, 2 * h + 1)),
                     pl.BlockSpec((1, n, NOPE_D), lambda b, h, i: (b, 0, 0))]
        args += [kv, kv, kr]
    return pl.pallas_call(
        functools.partial(_attn_kernel, n_seg=len(segs)),
        grid=(bsz, MLA_H, lq // tq),
        in_specs=in_specs,
        out_specs=pl.BlockSpec((1, tq, V_D), lambda b, h, i: (b, i, h)),
        out_shape=jax.ShapeDtypeStruct((bsz, lq, MLA_H * V_D), F32),
        scratch_shapes=[pltpu.VMEM((lk, HEAD_QK), BF16), pltpu.VMEM((lk, V_D), BF16)],
        compiler_params=pltpu.CompilerParams(
            dimension_semantics=("parallel", "parallel", "arbitrary"), vmem_limit_bytes=VMEM_LIMIT),
        name="attn",
    )(*args)


LANES = 128
PGROUP = 2
STAGE_UNROLL = 8


def _fft_dims(length):
    n = 2 * length
    bn = 128 if length >= 1024 else 32
    return n // bn, bn, bn + SUBLANES


def _dft_tables(length):
    a_n, bn, _ = _fft_dims(length)
    n, half = a_n * bn, a_n // 2
    p = jnp.arange(a_n, dtype=jnp.int32)
    b = jnp.arange(bn, dtype=jnp.int32)
    nn = bn * p[None, None, :] + b[:, None, None]
    ang = (2.0 * math.pi / n) * ((p[None, :, None] * nn) % n).astype(F32)
    cr, ci = jnp.cos(ang), -jnp.sin(ang)
    m1 = jnp.concatenate([jnp.concatenate([cr[:, :, :half], -ci[:, :, :half]], axis=2),
                          jnp.concatenate([ci[:, :, :half], cr[:, :, :half]], axis=2)], axis=1)
    m1f = jnp.concatenate([cr, ci], axis=1)
    crt, cit = jnp.swapaxes(cr, 1, 2)[:, :half], jnp.swapaxes(ci, 1, 2)[:, :half]
    m4 = jnp.concatenate([jnp.concatenate([crt, cit], axis=2),
                          jnp.concatenate([-cit, crt], axis=2)], axis=1) / n
    ang2 = (2.0 * math.pi / bn) * ((b[:, None] * b[None, :]) % bn).astype(F32)
    wr, wi = jnp.cos(ang2), -jnp.sin(ang2)
    m2 = jnp.concatenate([jnp.concatenate([wr, -wi], axis=1), jnp.concatenate([wi, wr], axis=1)], axis=0)
    m3 = jnp.concatenate([jnp.concatenate([wr, wi], axis=1), jnp.concatenate([-wi, wr], axis=1)], axis=0)
    return tuple(t.astype(BF16) for t in (m1, m1f, m2, m3, m4))


def _stage1(src, rows_per_part, parts, m_ref, t2, a_n, bn, pitch):
    def body(b, carry):
        rhs = jnp.concatenate([src[pl.ds(part * rows_per_part * pitch + b, rows_per_part, stride=pitch), :]
                               for part in range(parts)], axis=0)
        t = jnp.dot(m_ref[b], rhs.astype(BF16), preferred_element_type=F32)
        t2[pl.ds(b, a_n, stride=pitch), :] = t[:a_n]
        t2[pl.ds(a_n * pitch + b, a_n, stride=pitch), :] = t[a_n:]
        return carry
    lax.fori_loop(0, bn, body, 0, unroll=STAGE_UNROLL)


def _stage2_rhs(t2, g, a_n, bn, pitch):
    cols = []
    for k in range(PGROUP):
        r0 = pl.multiple_of((g * PGROUP + k) * pitch, SUBLANES)
        cols.append(jnp.concatenate([t2[pl.ds(r0, bn), :], t2[pl.ds(a_n * pitch + r0, bn), :]], axis=0))
    return jnp.concatenate(cols, axis=1).astype(BF16)


def _fftconv_kernel(z_ref, spec_ref, m1_ref, m2_ref, m3_ref, m4_ref, y_ref, zb, t2, *, length):
    a_n, bn, pitch = _fft_dims(length)
    half = a_n // 2
    for bi in range(2):
        for a in range(half):
            zb[(bi * half + a) * pitch:(bi * half + a) * pitch + bn, :] = z_ref[bi, a * bn:(a + 1) * bn, :]
    _stage1(zb, half, 2, m1_ref, t2, a_n, bn, pitch)

    def mid(g, carry):
        x = jnp.dot(m2_ref[...], _stage2_rhs(t2, g, a_n, bn, pitch), preferred_element_type=F32)
        k = spec_ref[0, 0, g].astype(F32)
        xr, xi, kr, ki = x[:bn], x[bn:], k[:bn], k[bn:]
        y = jnp.concatenate([xr * kr - xi * ki, xr * ki + xi * kr], axis=0).astype(BF16)
        u = jnp.dot(m3_ref[...], y, preferred_element_type=F32)
        for kk in range(PGROUP):
            r0 = pl.multiple_of((g * PGROUP + kk) * pitch, SUBLANES)
            t2[pl.ds(r0, bn), :] = u[:bn, kk * LANES:(kk + 1) * LANES]
            t2[pl.ds(a_n * pitch + r0, bn), :] = u[bn:, kk * LANES:(kk + 1) * LANES]
        return carry
    lax.fori_loop(0, a_n // PGROUP, mid, 0, unroll=2)

    def last(b, carry):
        rhs = jnp.concatenate([t2[pl.ds(b, a_n, stride=pitch), :],
                               t2[pl.ds(a_n * pitch + b, a_n, stride=pitch), :]], axis=0)
        y = jnp.dot(m4_ref[b], rhs.astype(BF16), preferred_element_type=F32)
        zb[pl.ds(b, half, stride=pitch), :] = y[:half]
        zb[pl.ds(half * pitch + b, half, stride=pitch), :] = y[half:]
        return carry
    lax.fori_loop(0, bn, last, 0, unroll=STAGE_UNROLL)
    for bi in range(2):
        for a in range(half):
            y_ref[bi, a * bn:(a + 1) * bn, :] = zb[(bi * half + a) * pitch:(bi * half + a) * pitch + bn, :]


def _fftconv(z, spec, tables):
    bsz, length, chans = z.shape
    assert bsz == 2, "the two batch rows are packed as one complex signal"
    a_n, bn, pitch = _fft_dims(length)
    m1, _, m2, m3, m4 = tables
    const3 = lambda c: (0, 0, 0)
    const2 = lambda c: (0, 0)
    once = pl.Buffered(1)
    return pl.pallas_call(
        functools.partial(_fftconv_kernel, length=length),
        grid=(chans // LANES,),
        in_specs=[pl.BlockSpec((2, length, LANES), lambda c: (0, 0, c)),
                  pl.BlockSpec((1, 1, a_n // PGROUP, 2 * bn, PGROUP * LANES), lambda c: (0, c, 0, 0, 0)),
                  pl.BlockSpec(m1.shape, const3, pipeline_mode=once),
                  pl.BlockSpec(m2.shape, const2, pipeline_mode=once),
                  pl.BlockSpec(m3.shape, const2, pipeline_mode=once),
                  pl.BlockSpec(m4.shape, const3, pipeline_mode=once)],
        out_specs=pl.BlockSpec((2, length, LANES), lambda c: (0, 0, c)),
        out_shape=jax.ShapeDtypeStruct(z.shape, F32),
        scratch_shapes=[pltpu.VMEM((a_n * pitch, LANES), F32), pltpu.VMEM((2 * a_n * pitch, LANES), F32)],
        compiler_params=pltpu.CompilerParams(dimension_semantics=("parallel",), vmem_limit_bytes=VMEM_LIMIT),
        name="fftconv",
    )(z, spec[None], m1, m2, m3, m4)


def _split_bf16(x):
    hi = x.astype(BF16)
    return hi, (x - hi.astype(F32)).astype(BF16)


def _dot_split(a, b):
    ah, al = _split_bf16(a)
    bh, bl = _split_bf16(b)
    d = functools.partial(jnp.dot, preferred_element_type=F32)
    return d(ah, bh) + (d(al, bh) + d(ah, bl))


def _filtspec_kernel(hf_ref, hb_ref, w3f_ref, w3b_ref, dec_ref, m1_ref, m2_ref, spec_ref, kb, t2, *, length):
    a_n, bn, pitch = _fft_dims(length)
    half = a_n // 2
    row = lax.broadcasted_iota(jnp.int32, (length, 1), 0).astype(F32)
    delta = dec_ref[...]
    kf = _dot_split(hf_ref[...], w3f_ref[...]) * jnp.exp(-(row / length) * delta)
    kbk = _dot_split(hb_ref[...], w3b_ref[...]) * jnp.exp(-((length - row) / length) * delta)
    kbk = jnp.where(row == 0.0, 0.0, kbk)
    norm = jnp.sum(jnp.abs(kf), axis=0, keepdims=True) + jnp.sum(jnp.abs(kbk), axis=0, keepdims=True)
    kf, kbk = kf / norm, kbk / norm
    for a in range(half):
        kb[a * pitch:a * pitch + bn, :] = kf[a * bn:(a + 1) * bn]
        kb[(half + a) * pitch:(half + a) * pitch + bn, :] = kbk[a * bn:(a + 1) * bn]
    _stage1(kb, a_n, 1, m1_ref, t2, a_n, bn, pitch)

    def mid(g, carry):
        spec_ref[0, 0, g] = jnp.dot(m2_ref[...], _stage2_rhs(t2, g, a_n, bn, pitch),
                                    preferred_element_type=F32).astype(spec_ref.dtype)
        return carry
    lax.fori_loop(0, a_n // PGROUP, mid, 0, unroll=2)


def _filter_spectra(length, w1, b1, w2, b2, w3, tables):
    a_n, bn, pitch = _fft_dims(length)
    hp = lax.Precision.HIGHEST
    t_f = jnp.arange(length, dtype=F32) / length
    band = jnp.arange(1, HY_BANDS + 1, dtype=F32)

    def hidden(t):
        ang = 2.0 * math.pi * t[:, None] * band
        feats = jnp.concatenate([t[:, None], jnp.sin(ang), jnp.cos(ang)], axis=-1)
        hid = jnp.sin(jnp.dot(feats, w1, precision=hp) + b1)
        return jnp.sin(jnp.dot(hid, w2, precision=hp) + b2)

    deltas = jnp.abs(jnp.linspace(math.log(HY_DECAY_TARGET) / HY_SLOW_DECAY,
                                  math.log(HY_DECAY_TARGET) / HY_FAST_DECAY, HY_C, dtype=F32))[None]
    _, m1f, m2, _, _ = tables
    nct = HY_C // LANES
    hid_f = hidden(t_f)
    hid_b = jnp.concatenate([hid_f[:1], hid_f[:0:-1]], axis=0)
    hidden_w = hid_f.shape[1]
    once = pl.Buffered(1)
    return pl.pallas_call(
        functools.partial(_filtspec_kernel, length=length),
        grid=(HY_ORDER, nct),
        in_specs=[pl.BlockSpec((length, hidden_w), lambda o, c: (0, 0), pipeline_mode=once),
                  pl.BlockSpec((length, hidden_w), lambda o, c: (0, 0), pipeline_mode=once),
                  pl.BlockSpec((hidden_w, LANES), lambda o, c: (0, o * 2 * nct + c)),
                  pl.BlockSpec((hidden_w, LANES), lambda o, c: (0, o * 2 * nct + nct + c)),
                  pl.BlockSpec((1, LANES), lambda o, c: (0, c)),
                  pl.BlockSpec(m1f.shape, lambda o, c: (0, 0, 0), pipeline_mode=once),
                  pl.BlockSpec(m2.shape, lambda o, c: (0, 0), pipeline_mode=once)],
        out_specs=pl.BlockSpec((1, 1, a_n // PGROUP, 2 * bn, PGROUP * LANES), lambda o, c: (o, c, 0, 0, 0)),
        out_shape=jax.ShapeDtypeStruct((HY_ORDER, nct, a_n // PGROUP, 2 * bn, PGROUP * LANES), BF16),
        scratch_shapes=[pltpu.VMEM((a_n * pitch, LANES), F32), pltpu.VMEM((2 * a_n * pitch, LANES), F32)],
        compiler_params=pltpu.CompilerParams(
            dimension_semantics=("parallel", "parallel"), vmem_limit_bytes=VMEM_LIMIT),
        name="filtspec",
    )(hid_f, hid_b, w3, w3, deltas, m1f, m2)


def _hygate_kernel(p_ref, w_ref, *rest, first):
    x = p_ref[0]
    n = x.shape[0]
    row = lax.broadcasted_iota(jnp.int32, x.shape, 0)
    prev = jnp.where(row == 0, 0.0, pltpu.roll(x, 1, axis=0))
    nxt = jnp.where(row == n - 1, 0.0, pltpu.roll(x, n - 1, axis=0))
    c = w_ref[0:1, :] * prev + w_ref[1:2, :] * x + w_ref[2:3, :] * nxt
    if first:
        (o_ref,) = rest
        o_ref[0] = c
    else:
        y_ref, z_ref, b_ref, o_ref = rest
        o_ref[0] = c * (y_ref[0] + z_ref[0] * b_ref[...])


def _hygate(u, part, conv_w, y=None, z=None, bias=None):
    bsz, length, _ = u.shape
    nct = HY_C // LANES
    blk = lambda b, c: (b, 0, c)
    in_specs = [pl.BlockSpec((1, length, LANES), lambda b, c: (b, 0, part * nct + c)),
                pl.BlockSpec((HY_SHORT, LANES), lambda b, c: (0, part * nct + c))]
    args = [u, conv_w]
    if y is not None:
        in_specs += [pl.BlockSpec((1, length, LANES), blk), pl.BlockSpec((1, length, LANES), blk),
                     pl.BlockSpec((1, LANES), lambda b, c: (0, c))]
        args += [y, z, bias]
    return pl.pallas_call(
        functools.partial(_hygate_kernel, first=y is None),
        grid=(bsz, nct),
        in_specs=in_specs,
        out_specs=pl.BlockSpec((1, length, LANES), blk),
        out_shape=jax.ShapeDtypeStruct((bsz, length, HY_C), F32),
        compiler_params=pltpu.CompilerParams(
            dimension_semantics=("parallel", "parallel"), vmem_limit_bytes=VMEM_LIMIT),
        name="hygate",
    )(*args)


def _hyena_mix(u, conv_w, spec, bias, tables):
    z = _hygate(u, 0, conv_w)
    for o in range(HY_ORDER):
        y = _fftconv(z, spec[o], tables)
        z = _hygate(u, o + 1, conv_w, y, z, bias[o:o + 1])
    return z


SUBLANES = 8
PAD = LRU_CONV - 1
SCAN_UNROLL = 8


def _lru_kernel(x_ref, wg_ref, pv_ref, h0_ref, o_ref, end_ref, xp, a_scr, b_scr, o_scr, *, steps, chunk):
    sub = lax.broadcasted_iota(jnp.int32, (SUBLANES, LRU_BW), 0)

    def gather(j, carry):
        xp[PAD + j] = x_ref[pl.ds(j, SUBLANES, stride=steps), :]
        return carry

    lax.fori_loop(0, steps, gather, 0, unroll=8)
    for k in range(PAD):
        tail = pltpu.roll(xp[steps + k], 1, axis=0)
        xp[k] = jnp.where(sub == 0, 0.0, tail)
        head = pltpu.roll(xp[PAD + k], SUBLANES - 1, axis=0)
        xp[PAD + steps + k] = jnp.where(sub == SUBLANES - 1, 0.0, head)

    n_chunks = steps // chunk
    for d in range(2):
        ba = pv_ref[d, 0:1, :]
        bx = pv_ref[d, 1:2, :]
        lam = pv_ref[d, 2:3, :]
        neg_c_sp = -RG_C * jax.nn.softplus(-lam)
        taps = [pv_ref[d, 3 + k:4 + k, :].reshape(1, 1, LRU_BW) for k in range(LRU_CONV)]
        w_gate = wg_ref[d, 0]
        base = 0 if d == 0 else PAD

        def gates(c, carry):
            j0 = pl.multiple_of(c * chunk, chunk)
            xc = taps[0] * xp[pl.ds(j0 + base, chunk)]
            for k in range(1, LRU_CONV):
                xc = xc + taps[k] * xp[pl.ds(j0 + base + k, chunk)]
            xc2 = xc.reshape(chunk * SUBLANES, LRU_BW)
            g = jnp.dot(xc2.astype(BF16), w_gate, preferred_element_type=F32)
            r = 0.5 * jnp.tanh(0.5 * (g[:, :LRU_BW] + ba)) + 0.5
            i = 0.5 * jnp.tanh(0.5 * (g[:, LRU_BW:] + bx)) + 0.5
            log_a = neg_c_sp * r
            a = jnp.exp(log_a)
            t = jnp.tanh(log_a)
            gap = -2.0 * t / (1.0 - t)
            root = jnp.where(gap > 0.0, gap * lax.rsqrt(gap), 0.0)
            b = root * (i * xc2)
            a_scr[pl.ds(j0, chunk)] = a.reshape(chunk, SUBLANES, LRU_BW)
            b_scr[pl.ds(j0, chunk)] = b.reshape(chunk, SUBLANES, LRU_BW)
            return carry

        lax.fori_loop(0, n_chunks, gates, 0)

        def steps8(k, carry):
            h, p = carry
            base = pl.multiple_of(k * SCAN_UNROLL if d == 0 else steps - SCAN_UNROLL * (k + 1), SCAN_UNROLL)
            for u in (range(SCAN_UNROLL) if d == 0 else range(SCAN_UNROLL - 1, -1, -1)):
                a = a_scr[base + u]
                h = a * h + b_scr[base + u]
                p = a * p
                b_scr[base + u] = h
                a_scr[base + u] = p
            return h, p

        h_loc, p_loc = lax.fori_loop(
            0, steps // SCAN_UNROLL, steps8,
            (jnp.zeros((SUBLANES, LRU_BW), F32), jnp.ones((SUBLANES, LRU_BW), F32)))

        carry = h0_ref[0, d:d + 1, :]
        cin = jnp.zeros((SUBLANES, LRU_BW), F32)
        order = range(SUBLANES) if d == 0 else range(SUBLANES - 1, -1, -1)
        for s in order:
            cin = jnp.where(sub == s, jnp.broadcast_to(carry, (SUBLANES, LRU_BW)), cin)
            carry = p_loc[s:s + 1, :] * carry + h_loc[s:s + 1, :]
        end_ref[0, d:d + 1, :] = carry

        def fix(c, carry_):
            j0 = pl.multiple_of(c * chunk, chunk)
            h = b_scr[pl.ds(j0, chunk)] + a_scr[pl.ds(j0, chunk)] * cin[None]
            if d == 0:
                o_scr[pl.ds(j0, chunk)] = h
            else:
                h = h + o_scr[pl.ds(j0, chunk)]
                for jj in range(chunk):
                    o_ref[pl.ds(j0 + jj, SUBLANES, stride=steps), :] = h[jj]
            return carry_

        lax.fori_loop(0, n_chunks, fix, 0)


def _rglru(u, length, lconv, lwa, lba, lwx, lbx, llam, h0):
    m = u.shape[0]
    bsz = m // length
    steps = length // SUBLANES
    chunk = min(32, steps)
    wg = jnp.concatenate([lwa, lwx], axis=-1).astype(BF16)
    pv = jnp.concatenate([lba[:, None], lbx[:, None], llam[:, None], lconv,
                          jnp.zeros((2, 1, LRU_C), F32)], axis=1)
    seg = pltpu.VMEM((steps, SUBLANES, LRU_BW), F32)
    return pl.pallas_call(
        functools.partial(_lru_kernel, steps=steps, chunk=chunk),
        grid=(bsz, LRU_BLOCKS),
        in_specs=[pl.BlockSpec((length, LRU_BW), lambda b, n: (b, OFF_LRU_X // LRU_BW + n)),
                  pl.BlockSpec((2, 1, LRU_BW, 2 * LRU_BW), lambda b, n: (0, n, 0, 0)),
                  pl.BlockSpec((2, SUBLANES, LRU_BW), lambda b, n: (0, 0, n)),
                  pl.BlockSpec((1, 2, LRU_BW), lambda b, n: (b, 0, n))],
        out_specs=[pl.BlockSpec((length, LRU_BW), lambda b, n: (b, n)),
                   pl.BlockSpec((1, 2, LRU_BW), lambda b, n: (b, 0, n))],
        out_shape=[jax.ShapeDtypeStruct((m, LRU_C), F32),
                   jax.ShapeDtypeStruct((bsz, 2, LRU_C), F32)],
        scratch_shapes=[pltpu.VMEM((steps + 2 * PAD, SUBLANES, LRU_BW), F32), seg, seg, seg],
        compiler_params=pltpu.CompilerParams(
            dimension_semantics=("parallel", "parallel"), vmem_limit_bytes=VMEM_LIMIT),
        name="rglru",
    )(u, wg, pv, h0)


def _rope_tables(length):
    rows = length // GRID_W
    row = jnp.repeat(jnp.arange(rows, dtype=F32), GRID_W)
    col = jnp.tile(jnp.arange(GRID_W, dtype=F32), rows)
    n_freq = ROPE_D // 4
    inv = ROPE_BASE ** (-jnp.arange(n_freq, dtype=F32) / n_freq)
    ang = jnp.concatenate([row[:, None] * inv, col[:, None] * inv], axis=-1)
    return jnp.cos(ang), jnp.sin(ang)


def _rope_lane_tables(rope):
    cos, sin = rope
    z = jnp.zeros_like(cos)
    return (jnp.concatenate([cos, cos, z, z], axis=1), jnp.concatenate([-sin, z, z, z], axis=1),
            jnp.concatenate([z, sin, z, z], axis=1))


def _pad_heads_q(wuq):
    w = wuq.reshape(Q_RANK, MLA_H, NOPE_D + ROPE_D)
    w = jnp.pad(w, ((0, 0), (0, 0), (0, HEAD_QK - NOPE_D - ROPE_D)))
    return w.reshape(Q_RANK, MLA_H * HEAD_QK)


def _rms(xv, g):
    return xv * lax.rsqrt(jnp.mean(xv * xv, axis=-1, keepdims=True) + EPS) * g


ROW_SPLIT = 2


def _proj_in_kernel(x_ref, g_ref, mod_ref, w_ref, o_ref):
    sub = x_ref.shape[0] // ROW_SPLIT
    for part in range(ROW_SPLIT):
        rows = slice(part * sub, (part + 1) * sub)
        y = _rms(x_ref[rows, :], g_ref[...])
        h = (y * (1.0 + mod_ref[0, 1:2, :]) + mod_ref[0, 0:1, :]).astype(BF16)
        o_ref[rows, :] = jnp.dot(h, w_ref[...], preferred_element_type=F32)


def _proj_in(x, g, shift, scale, w, layer):
    bsz, length, d = x.shape
    m, n = bsz * length, w.shape[2]
    tm = _pick(length, (512, 256, 128))
    bps = length // tm
    mod = jnp.stack([shift, scale], axis=1)
    mod_idx = (lambda i: (i // bps, 0, 0)) if shift.shape[0] > 1 else (lambda i: (0, 0, 0))
    return pl.pallas_call(
        _proj_in_kernel,
        grid=(m // tm,),
        in_specs=[pl.BlockSpec((tm, d), lambda i: (i, 0)),
                  pl.BlockSpec((1, d), lambda i: (0, 0)),
                  pl.BlockSpec((1, 2, d), mod_idx),
                  pl.BlockSpec((None, d, n), lambda i: (layer, 0, 0), pipeline_mode=pl.Buffered(1))],
        out_specs=pl.BlockSpec((tm, n), lambda i: (i, 0)),
        out_shape=jax.ShapeDtypeStruct((m, n), F32),
        compiler_params=pltpu.CompilerParams(dimension_semantics=("parallel",), vmem_limit_bytes=VMEM_LIMIT),
        name="proj_in",
    )(x.reshape(m, d), g[None], mod, w)


def _proj_qkv_kernel(*refs, rope, with_q):
    refs = list(refs)
    uq_ref = refs.pop(0) if with_q else None
    ukv_ref, ukr_ref = refs.pop(0), refs.pop(0)
    gq_ref, wq_ref = (refs.pop(0), refs.pop(0)) if with_q else (None, None)
    gkv_ref, wkv_ref = refs.pop(0), refs.pop(0)
    cos_ref, sa_ref, sb_ref = (refs.pop(0), refs.pop(0), refs.pop(0)) if rope else (None, None, None)
    q_ref = refs.pop(0) if with_q else None
    kv_ref, kr_ref = refs

    def rot(blk):
        if not rope:
            return blk
        return (blk * cos_ref[...] + pltpu.roll(blk, LANES - 32, axis=1) * sa_ref[...]
                + pltpu.roll(blk, 32, axis=1) * sb_ref[...])

    if with_q:
        nq = _rms(uq_ref[...], gq_ref[...]).astype(BF16)
        q = jnp.dot(nq, wq_ref[...], preferred_element_type=F32)
        for hd in range(MLA_H):
            lo = hd * HEAD_QK
            q_ref[:, lo:lo + NOPE_D] = (q[:, lo:lo + NOPE_D] * Q_SCALE).astype(BF16)
            q_ref[:, lo + NOPE_D:lo + HEAD_QK] = (rot(q[:, lo + NOPE_D:lo + HEAD_QK]) * Q_SCALE).astype(BF16)
    nkv = _rms(ukv_ref[...], gkv_ref[...]).astype(BF16)
    kv_ref[...] = jnp.dot(nkv, wkv_ref[...], preferred_element_type=F32).astype(BF16)
    kr_ref[...] = rot(ukr_ref[...]).astype(BF16)


def _proj_qkv(u, length, gq, wuq_pad, gkv, wukv, layer, rope_tabs, with_q=True):
    m = u.shape[0]
    tm = _pick(length, (512, 256, 128))
    bps = length // tm
    row = lambda i: (i, 0)
    const = lambda i: (0, 0)
    wsel = lambda i: (layer, 0, 0)
    in_specs, args = [], []
    if with_q:
        in_specs.append(pl.BlockSpec((tm, Q_RANK), lambda i: (i, OFF_MLA_Q // Q_RANK)))
        args.append(u)
    in_specs += [pl.BlockSpec((tm, KV_RANK), lambda i: (i, OFF_MLA_KV // KV_RANK)),
                 pl.BlockSpec((tm, LANES), lambda i: (i, OFF_MLA_KR // LANES))]
    args += [u, u]
    if with_q:
        in_specs += [pl.BlockSpec((1, Q_RANK), const), pl.BlockSpec((None,) + wuq_pad.shape[1:], wsel)]
        args += [gq[None], wuq_pad]
    in_specs += [pl.BlockSpec((1, KV_RANK), const), pl.BlockSpec((None,) + wukv.shape[1:], wsel)]
    args += [gkv[None], wukv]
    if rope_tabs is not None:
        in_specs += [pl.BlockSpec((tm, LANES), lambda i: (i % bps, 0))] * 3
        args += list(rope_tabs)
    out_specs = [pl.BlockSpec((tm, MLA_H * HEAD_QK), row), pl.BlockSpec((tm, LANES), row)]
    out_shape = [jax.ShapeDtypeStruct((m, MLA_H * HEAD_QK), BF16), jax.ShapeDtypeStruct((m, LANES), BF16)]
    if with_q:
        out_specs.insert(0, pl.BlockSpec((tm, MLA_H * HEAD_QK), row))
        out_shape.insert(0, jax.ShapeDtypeStruct((m, MLA_H * HEAD_QK), BF16))
    outs = pl.pallas_call(
        functools.partial(_proj_qkv_kernel, rope=rope_tabs is not None, with_q=with_q),
        grid=(m // tm,),
        in_specs=in_specs, out_specs=out_specs, out_shape=out_shape,
        compiler_params=pltpu.CompilerParams(dimension_semantics=("parallel",), vmem_limit_bytes=VMEM_LIMIT),
        name="proj_qkv",
    )(*args)
    return outs if with_q else [None] + list(outs)


def _mix_kernel(yhy_ref, hs_ref, gate_ref, ymla_ref, hg_ref, w_ref, x_ref, ng_ref, mod_ref, o_ref):
    sub = x_ref.shape[0] // ROW_SPLIT
    for part in range(ROW_SPLIT):
        rows = slice(part * sub, (part + 1) * sub)
        y = jnp.concatenate([yhy_ref[rows, :], jax.nn.gelu(gate_ref[rows, :], approximate=True) * hs_ref[rows, :],
                             ymla_ref[rows, :]], axis=1)
        heads = [_rms(y[:, hd * HEAD_W:(hd + 1) * HEAD_W], hg_ref[:, hd * HEAD_W:(hd + 1) * HEAD_W]).astype(BF16)
                 for hd in range(N_MIX_HEADS)]
        acc = jnp.dot(jnp.concatenate(heads, axis=1), w_ref[...], preferred_element_type=F32)
        o_ref[rows, :] = x_ref[rows, :] + mod_ref[0] * _rms(acc, ng_ref[...])


def _mix_out(x, y_hy, h_sum, u, y_mla, head_g, w_out, layer, ng_post, gate):
    bsz, length, d = x.shape
    m = bsz * length
    tm = _pick(length, (512, 256, 128))
    bps = length // tm
    row = lambda i: (i, 0)
    const = lambda i: (0, 0)
    mod_idx = (lambda i: (i // bps, 0, 0)) if gate.shape[0] > 1 else (lambda i: (0, 0, 0))
    out = pl.pallas_call(
        _mix_kernel,
        grid=(m // tm,),
        in_specs=[pl.BlockSpec((tm, HY_C), row), pl.BlockSpec((tm, LRU_C), row),
                  pl.BlockSpec((tm, LRU_C), lambda i: (i, OFF_LRU_G // LRU_C)),
                  pl.BlockSpec((tm, MLA_H * V_D), row),
                  pl.BlockSpec((1, D_MIX), const),
                  pl.BlockSpec((None,) + w_out.shape[1:], lambda i: (layer, 0, 0), pipeline_mode=pl.Buffered(1)),
                  pl.BlockSpec((tm, d), row), pl.BlockSpec((1, d), const), pl.BlockSpec((1, 1, d), mod_idx)],
        out_specs=pl.BlockSpec((tm, d), row),
        out_shape=jax.ShapeDtypeStruct((m, d), F32),
        compiler_params=pltpu.CompilerParams(dimension_semantics=("parallel",), vmem_limit_bytes=VMEM_LIMIT),
        name="mix_out",
    )(y_hy.reshape(m, HY_C), h_sum.reshape(m, LRU_C), u, y_mla.reshape(m, MLA_H * V_D), head_g[None], w_out,
      x.reshape(m, d), ng_post[None], gate[:, None])
    return out.reshape(bsz, length, d)


HALO = 16


def _ffn_kernel(x_ref, xprev_ref, xnext_ref, ng_ref, mod_ref, wg_ref, wv_ref, cg_ref, cv_ref, wd_ref,
                o_ref, hbuf, *, blocks_per_seq):
    i = pl.program_id(0)
    j = pl.program_id(1)
    tm = x_ref.shape[0]

    def norm_mod(xv):
        y = xv * lax.rsqrt(jnp.mean(xv * xv, axis=-1, keepdims=True) + EPS) * ng_ref[0:1, :]
        return (y * (1.0 + mod_ref[0, 1:2, :]) + mod_ref[0, 0:1, :]).astype(BF16)

    @pl.when(j == 0)
    def _():
        first = (i % blocks_per_seq) == 0
        last = (i % blocks_per_seq) == blocks_per_seq - 1
        hp = norm_mod(xprev_ref[...])
        hn = norm_mod(xnext_ref[...])
        hbuf[0:HALO] = jnp.where(first, jnp.zeros_like(hp), hp)
        hbuf[HALO:HALO + tm] = norm_mod(x_ref[...])
        hbuf[HALO + tm:] = jnp.where(last, jnp.zeros_like(hn), hn)
        o_ref[...] = jnp.zeros_like(o_ref)

    hb = hbuf[...]

    def conv(up, c_ref):
        prev = pltpu.roll(up, 1, axis=0)[HALO:HALO + tm]
        nxt = pltpu.roll(up, tm + 2 * HALO - 1, axis=0)[HALO:HALO + tm]
        return c_ref[0:1, :] * prev + c_ref[1:2, :] * up[HALO:HALO + tm] + c_ref[2:3, :] * nxt

    gate = conv(jnp.dot(hb, wg_ref[...], preferred_element_type=F32), cg_ref)
    val = conv(jnp.dot(hb, wv_ref[...], preferred_element_type=F32), cv_ref)
    act = (jax.nn.gelu(gate, approximate=True) * val).astype(BF16)
    o_ref[...] += jnp.dot(act, wd_ref[...], preferred_element_type=F32)

    @pl.when(j == pl.num_programs(1) - 1)
    def _():
        y = o_ref[...]
        y = y * lax.rsqrt(jnp.mean(y * y, axis=-1, keepdims=True) + EPS) * ng_ref[1:2, :]
        o_ref[...] = x_ref[...] + mod_ref[0, 2:3, :] * y


def _ffn_sublayer(x, ng_pre, ng_post, shift, scale, gate, w_up, w_conv, w_down, layer):
    bsz, length, d = x.shape
    m = bsz * length
    tm = _pick(length, (1024, 512, 256, 128))
    tf = _pick(D_FF, (512, 256, 128))
    nf = D_FF // tf
    bps = length // tm
    hb = tm // HALO
    nrow = m // HALO
    per_batch = shift.shape[0] > 1
    ng = jnp.stack([ng_pre, ng_post])
    mod = jnp.stack([shift, scale, gate], axis=1)
    out = pl.pallas_call(
        functools.partial(_ffn_kernel, blocks_per_seq=bps),
        grid=(m // tm, nf),
        in_specs=[pl.BlockSpec((tm, d), lambda i, j: (i, 0), pipeline_mode=pl.Buffered(1)),
                  pl.BlockSpec((HALO, d), lambda i, j: (jnp.maximum(i * hb - 1, 0), 0)),
                  pl.BlockSpec((HALO, d), lambda i, j: (jnp.minimum((i + 1) * hb, nrow - 1), 0)),
                  pl.BlockSpec((2, d), lambda i, j: (0, 0)),
                  pl.BlockSpec((1, 3, d), (lambda i, j: (i // bps, 0, 0)) if per_batch else (lambda i, j: (0, 0, 0))),
                  pl.BlockSpec((None, d, tf), lambda i, j: (layer, 0, j)),
                  pl.BlockSpec((None, d, tf), lambda i, j: (layer, 0, j + nf)),
                  pl.BlockSpec((None, FFN_CONV, tf), lambda i, j: (layer, 0, j)),
                  pl.BlockSpec((None, FFN_CONV, tf), lambda i, j: (layer, 0, j + nf)),
                  pl.BlockSpec((None, tf, d), lambda i, j: (layer, j, 0))],
        out_specs=pl.BlockSpec((tm, d), lambda i, j: (i, 0)),
        out_shape=jax.ShapeDtypeStruct((m, d), F32),
        scratch_shapes=[pltpu.VMEM((tm + 2 * HALO, d), BF16)],
        compiler_params=pltpu.CompilerParams(
            dimension_semantics=("parallel", "arbitrary"), vmem_limit_bytes=FFN_VMEM_LIMIT),
        name="ffn",
    )(x.reshape(m, d), x.reshape(m, d), x.reshape(m, d), ng, mod, w_up, w_up, w_conv, w_conv, w_down)
    return out.reshape(bsz, length, d)


def kernel(x, c, ctx, c_ctx, ada_w, ada_b, norm_g, w_in, hy_conv, hy_w1, hy_b1, hy_w2, hy_b2, hy_w3, hy_bias, lru_conv, lru_wa, lru_ba, lru_wx, lru_bx, lru_lam, mla_gq, mla_gkv, mla_wuq, mla_wukv, head_g, w_out, ffn_up, ffn_conv, ffn_down):
    depth = ada_w.shape[0]
    bsz, seq, _ = x.shape
    ctx_len = ctx.shape[1]
    rope_tabs = _rope_lane_tables(_rope_tables(seq))
    dft_lat, dft_ctx = _dft_tables(seq), _dft_tables(ctx_len)
    xc = ctx
    w_in_b = jnp.pad(w_in, ((0, 0), (0, 0), (0, NOPE_D - ROPE_D))).astype(BF16)
    wuq_pad = jax.vmap(_pad_heads_q)(mla_wuq).astype(BF16)
    wukv_b, w_out_b = mla_wukv.astype(BF16), w_out.astype(BF16)
    up_b, down_b = ffn_up.astype(BF16), ffn_down.astype(BF16)
    mods = _adaln(jnp.concatenate([c, c_ctx[None]], axis=0), ada_w, ada_b)

    def seq3(t, length):
        return None if t is None else t.reshape(bsz, length, t.shape[-1])

    for l in range(depth):
        need_ctx = l < depth - 1
        ng = norm_g[l]
        sh1, sc1, g1, sh2, sc2, g2 = jnp.split(mods[l, :bsz], N_MOD, axis=-1)
        csh1, csc1, cg1, csh2, csc2, cg2 = jnp.split(mods[l, bsz:], N_MOD, axis=-1)
        hy_f = (hy_w1[l], hy_b1[l], hy_w2[l], hy_b2[l], hy_w3[l])
        lru_p = (lru_conv[l], lru_wa[l], lru_ba[l], lru_wx[l], lru_bx[l], lru_lam[l])

        u = _proj_in(x, ng[0], sh1, sc1, w_in_b, l)
        uc = _proj_in(xc, ng[0], csh1, csc1, w_in_b, l)
        u3, uc3 = seq3(u, seq), seq3(uc, ctx_len)

        hc_sum, hc_end = _rglru(uc, ctx_len, *lru_p, jnp.zeros((bsz, 2, LRU_C), F32))
        q_c, kv_c, kr_c = [seq3(t, ctx_len) for t in
                           _proj_qkv(uc, ctx_len, mla_gq[l], wuq_pad, mla_gkv[l], wukv_b, l, None, with_q=need_ctx)]

        y_hy = _hyena_mix(u3, hy_conv[l], _filter_spectra(seq, *hy_f, dft_lat), hy_bias[l], dft_lat)
        h_sum, _ = _rglru(u, seq, *lru_p, hc_end)
        q, kv, kr = [seq3(t, seq) for t in
                     _proj_qkv(u, seq, mla_gq[l], wuq_pad, mla_gkv[l], wukv_b, l, rope_tabs)]
        y_mla = _attention(q, [(kv_c, kr_c), (kv, kr)])
        x = _mix_out(x, y_hy, h_sum, u, y_mla, head_g[l], w_out_b, l, ng[1], g1)

        if need_ctx:
            yc_hy = _hyena_mix(uc3, hy_conv[l], _filter_spectra(ctx_len, *hy_f, dft_ctx), hy_bias[l], dft_ctx)
            yc_mla = _attention(q_c, [(kv_c, kr_c)])
            xc = _mix_out(xc, yc_hy, hc_sum, uc, yc_mla, head_g[l], w_out_b, l, ng[1], cg1)

        x = _ffn_sublayer(x, ng[2], ng[3], sh2, sc2, g2, up_b, ffn_conv, down_b, l)
        if need_ctx:
            xc = _ffn_sublayer(xc, ng[2], ng[3], csh2, csc2, cg2, up_b, ffn_conv, down_b, l)
    return x
```

```python
import functools
import math

import jax
import jax.numpy as jnp
from jax import lax
from jax.experimental import pallas as pl
from jax.experimental.pallas import tpu as pltpu

F32 = jnp.float32
BF16 = jnp.bfloat16

D_MODEL = 2048
GRID_W = 64
HEAD_W = 128
HY_C = 512
HY_ORDER = 2
HY_SHORT = 3
HY_BANDS = 16
HY_FAST_DECAY = 0.3
HY_SLOW_DECAY = 1.5
HY_DECAY_TARGET = 1e-2
LRU_C = 512
LRU_BLOCKS = 4
LRU_BW = LRU_C // LRU_BLOCKS
LRU_CONV = 4
RG_C = 8.0
MLA_H = 8
Q_RANK = 512
KV_RANK = 256
NOPE_D = 128
ROPE_D = 64
V_D = 128
ROPE_BASE = 10000.0
MLA_SCALE = (NOPE_D + ROPE_D) ** -0.5
Q_SCALE = MLA_SCALE * math.log2(math.e)
D_MIX = HY_C + LRU_C + MLA_H * V_D
N_MIX_HEADS = D_MIX // HEAD_W
D_FF = 5632
FFN_CONV = 3
N_MOD = 6
EPS = 1e-6

OFF_HY = 0
OFF_LRU_G = OFF_HY + (HY_ORDER + 1) * HY_C
OFF_MLA_Q = OFF_LRU_G + LRU_C
OFF_LRU_X = OFF_MLA_Q + Q_RANK
OFF_MLA_KV = OFF_LRU_X + LRU_C
OFF_MLA_KR = OFF_MLA_KV + KV_RANK
D_IN = OFF_MLA_KR + ROPE_D

HEAD_QK = 2 * NOPE_D
VMEM_LIMIT = 48 * 1024 * 1024
FFN_VMEM_LIMIT = 56 * 1024 * 1024


def _pick(n, prefs):
    for p in prefs:
        if n % p == 0:
            return p
    return n


def _adaln_kernel(c_ref, w_ref, b_ref, o_ref):
    c = c_ref[...]
    a = (c * jax.nn.sigmoid(c)).astype(BF16)
    o_ref[...] = jnp.dot(a, w_ref[...].astype(BF16), preferred_element_type=F32) + b_ref[...]


def _adaln(cond, ada_w, ada_b):
    depth, d, n = ada_w.shape
    rows = cond.shape[0]
    rp = -(-rows // SUBLANES) * SUBLANES
    cond = jnp.pad(cond, ((0, rp - rows), (0, 0)))
    tn = 1024
    out = pl.pallas_call(
        _adaln_kernel,
        grid=(depth, n // tn),
        in_specs=[pl.BlockSpec((rp, d), lambda l, j: (0, 0)),
                  pl.BlockSpec((None, d, tn), lambda l, j: (l, 0, j)),
                  pl.BlockSpec((None, 1, tn), lambda l, j: (l, 0, j))],
        out_specs=pl.BlockSpec((None, rp, tn), lambda l, j: (l, 0, j)),
        out_shape=jax.ShapeDtypeStruct((depth, rp, n), F32),
        compiler_params=pltpu.CompilerParams(
            dimension_semantics=("parallel", "parallel"), vmem_limit_bytes=VMEM_LIMIT),
        name="adaln",
    )(cond, ada_w, ada_b[:, None])
    return out[:, :rows]


Q_SPLIT = 4

def _attn_kernel(*refs, n_seg):
    q_ref = refs[0]
    seg_refs = refs[1:1 + 3 * n_seg]
    o_ref = refs[1 + 3 * n_seg]
    k_scr, v_scr = refs[2 + 3 * n_seg:]

    @pl.when(pl.program_id(2) == 0)
    def _():
        off = 0
        for s in range(n_seg):
            kn_ref, v_ref, kr_ref = seg_refs[3 * s:3 * s + 3]
            n = kn_ref.shape[1]
            k_scr[off:off + n, 0:NOPE_D] = kn_ref[0]
            k_scr[off:off + n, NOPE_D:HEAD_QK] = kr_ref[0]
            v_scr[off:off + n, :] = v_ref[0]
            off += n

    sub = q_ref.shape[1] // Q_SPLIT
    parts = [slice(part * sub, (part + 1) * sub) for part in range(Q_SPLIT)]
    scores = [lax.dot_general(q_ref[0, rows], k_scr[...], (((1,), (1,)), ((), ())), preferred_element_type=F32)
              for rows in parts]
    probs = []
    for s in scores:
        m = jnp.max(s, axis=-1, keepdims=True)
        p = jnp.exp2(s - m)
        probs.append((p.astype(BF16), jnp.sum(p, axis=-1, keepdims=True)))
    for rows, (p, l) in zip(parts, probs):
        o_ref[0, rows] = jnp.dot(p, v_scr[...], preferred_element_type=F32) / l


def _attention(q, segs):
    bsz, lq, _ = q.shape
    tq = _pick(lq, (1024, 512, 256, 128))
    lk = sum(kv.shape[1] for kv, _ in segs)
    in_specs = [pl.BlockSpec((1, tq, HEAD_QK), lambda b, h, i: (b, i, h))]
    args = [q]
    for kv, kr in segs:
        n = kv.shape[1]
        in_specs += [pl.BlockSpec((1, n, NOPE_D), lambda b, h, i: (b, 0, 2 * h)),
                     pl.BlockSpec((1, n, V_D), lambda b, h, i: (b, 0, 2 * h + 1)),
                     pl.BlockSpec((1, n, NOPE_D), lambda b, h, i: (b, 0, 0))]
        args += [kv, kv, kr]
    return pl.pallas_call(
        functools.partial(_attn_kernel, n_seg=len(segs)),
        grid=(bsz, MLA_H, lq // tq),
        in_specs=in_specs,
        out_specs=pl.BlockSpec((1, tq, V_D), lambda b, h, i: (b, i, h)),
        out_shape=jax.ShapeDtypeStruct((bsz, lq, MLA_H * V_D), F32),
        scratch_shapes=[pltpu.VMEM((lk, HEAD_QK), BF16), pltpu.VMEM((lk, V_D), BF16)],
        compiler_params=pltpu.CompilerParams(
            dimension_semantics=("parallel", "parallel", "arbitrary"), vmem_limit_bytes=VMEM_LIMIT),
        name="attn",
    )(*args)


LANES = 128
PGROUP = 2
STAGE_UNROLL = 8


def _fft_dims(length):
    n = 2 * length
    bn = 128 if length >= 1024 else 32
    return n // bn, bn, bn + SUBLANES


def _dft_tables(length):
    a_n, bn, _ = _fft_dims(length)
    n, half = a_n * bn, a_n // 2
    p = jnp.arange(a_n, dtype=jnp.int32)
    b = jnp.arange(bn, dtype=jnp.int32)
    nn = bn * p[None, None, :] + b[:, None, None]
    ang = (2.0 * math.pi / n) * ((p[None, :, None] * nn) % n).astype(F32)
    cr, ci = jnp.cos(ang), -jnp.sin(ang)
    m1 = jnp.concatenate([jnp.concatenate([cr[:, :, :half], -ci[:, :, :half]], axis=2),
                          jnp.concatenate([ci[:, :, :half], cr[:, :, :half]], axis=2)], axis=1)
    m1f = jnp.concatenate([cr, ci], axis=1)
    crt, cit = jnp.swapaxes(cr, 1, 2)[:, :half], jnp.swapaxes(ci, 1, 2)[:, :half]
    m4 = jnp.concatenate([jnp.concatenate([crt, cit], axis=2),
                          jnp.concatenate([-cit, crt], axis=2)], axis=1) / n
    ang2 = (2.0 * math.pi / bn) * ((b[:, None] * b[None, :]) % bn).astype(F32)
    wr, wi = jnp.cos(ang2), -jnp.sin(ang2)
    m2 = jnp.concatenate([jnp.concatenate([wr, -wi], axis=1), jnp.concatenate([wi, wr], axis=1)], axis=0)
    m3 = jnp.concatenate([jnp.concatenate([wr, wi], axis=1), jnp.concatenate([-wi, wr], axis=1)], axis=0)
    return tuple(t.astype(BF16) for t in (m1, m1f, m2, m3, m4))


def _stage1(src, rows_per_part, parts, m_ref, t2, a_n, bn, pitch):
    def body(b, carry):
        rhs = jnp.concatenate([src[pl.ds(part * rows_per_part * pitch + b, rows_per_part, stride=pitch), :]
                               for part in range(parts)], axis=0)
        t = jnp.dot(m_ref[b], rhs.astype(BF16), preferred_element_type=F32)
        t2[pl.ds(b, a_n, stride=pitch), :] = t[:a_n]
        t2[pl.ds(a_n * pitch + b, a_n, stride=pitch), :] = t[a_n:]
        return carry
    lax.fori_loop(0, bn, body, 0, unroll=STAGE_UNROLL)


def _stage2_rhs(t2, g, a_n, bn, pitch):
    cols = []
    for k in range(PGROUP):
        r0 = pl.multiple_of((g * PGROUP + k) * pitch, SUBLANES)
        cols.append(jnp.concatenate([t2[pl.ds(r0, bn), :], t2[pl.ds(a_n * pitch + r0, bn), :]], axis=0))
    return jnp.concatenate(cols, axis=1).astype(BF16)


def _fftconv_kernel(z_ref, spec_ref, m1_ref, m2_ref, m3_ref, m4_ref, y_ref, zb, t2, *, length):
    a_n, bn, pitch = _fft_dims(length)
    half = a_n // 2
    for bi in range(2):
        for a in range(half):
            zb[(bi * half + a) * pitch:(bi * half + a) * pitch + bn, :] = z_ref[bi, a * bn:(a + 1) * bn, :]
    _stage1(zb, half, 2, m1_ref, t2, a_n, bn, pitch)

    def mid(g, carry):
        x = jnp.dot(m2_ref[...], _stage2_rhs(t2, g, a_n, bn, pitch), preferred_element_type=F32)
        k = spec_ref[0, 0, g].astype(F32)
        xr, xi, kr, ki = x[:bn], x[bn:], k[:bn], k[bn:]
        y = jnp.concatenate([xr * kr - xi * ki, xr * ki + xi * kr], axis=0).astype(BF16)
        u = jnp.dot(m3_ref[...], y, preferred_element_type=F32)
        for kk in range(PGROUP):
            r0 = pl.multiple_of((g * PGROUP + kk) * pitch, SUBLANES)
            t2[pl.ds(r0, bn), :] = u[:bn, kk * LANES:(kk + 1) * LANES]
            t2[pl.ds(a_n * pitch + r0, bn), :] = u[bn:, kk * LANES:(kk + 1) * LANES]
        return carry
    lax.fori_loop(0, a_n // PGROUP, mid, 0, unroll=2)

    def last(b, carry):
        rhs = jnp.concatenate([t2[pl.ds(b, a_n, stride=pitch), :],
                               t2[pl.ds(a_n * pitch + b, a_n, stride=pitch), :]], axis=0)
        y = jnp.dot(m4_ref[b], rhs.astype(BF16), preferred_element_type=F32)
        zb[pl.ds(b, half, stride=pitch), :] = y[:half]
        zb[pl.ds(half * pitch + b, half, stride=pitch), :] = y[half:]
        return carry
    lax.fori_loop(0, bn, last, 0, unroll=STAGE_UNROLL)
    for bi in range(2):
        for a in range(half):
            y_ref[bi, a * bn:(a + 1) * bn, :] = zb[(bi * half + a) * pitch:(bi * half + a) * pitch + bn, :]


def _fftconv(z, spec, tables):
    bsz, length, chans = z.shape
    assert bsz == 2, "the two batch rows are packed as one complex signal"
    a_n, bn, pitch = _fft_dims(length)
    m1, _, m2, m3, m4 = tables
    const3 = lambda c: (0, 0, 0)
    const2 = lambda c: (0, 0)
    once = pl.Buffered(1)
    return pl.pallas_call(
        functools.partial(_fftconv_kernel, length=length),
        grid=(chans // LANES,),
        in_specs=[pl.BlockSpec((2, length, LANES), lambda c: (0, 0, c)),
                  pl.BlockSpec((1, 1, a_n // PGROUP, 2 * bn, PGROUP * LANES), lambda c: (0, c, 0, 0, 0)),
                  pl.BlockSpec(m1.shape, const3, pipeline_mode=once),
                  pl.BlockSpec(m2.shape, const2, pipeline_mode=once),
                  pl.BlockSpec(m3.shape, const2, pipeline_mode=once),
                  pl.BlockSpec(m4.shape, const3, pipeline_mode=once)],
        out_specs=pl.BlockSpec((2, length, LANES), lambda c: (0, 0, c)),
        out_shape=jax.ShapeDtypeStruct(z.shape, F32),
        scratch_shapes=[pltpu.VMEM((a_n * pitch, LANES), F32), pltpu.VMEM((2 * a_n * pitch, LANES), F32)],
        compiler_params=pltpu.CompilerParams(dimension_semantics=("parallel",), vmem_limit_bytes=VMEM_LIMIT),
        name="fftconv",
    )(z, spec[None], m1, m2, m3, m4)


def _split_bf16(x):
    hi = x.astype(BF16)
    return hi, (x - hi.astype(F32)).astype(BF16)


def _dot_split(a, b):
    ah, al = _split_bf16(a)
    bh, bl = _split_bf16(b)
    d = functools.partial(jnp.dot, preferred_element_type=F32)
    return d(ah, bh) + (d(al, bh) + d(ah, bl))


def _filtspec_kernel(hf_ref, hb_ref, w3f_ref, w3b_ref, dec_ref, m1_ref, m2_ref, spec_ref, kb, t2, *, length):
    a_n, bn, pitch = _fft_dims(length)
    half = a_n // 2
    row = lax.broadcasted_iota(jnp.int32, (length, 1), 0).astype(F32)
    delta = dec_ref[...]
    kf = _dot_split(hf_ref[...], w3f_ref[...]) * jnp.exp(-(row / length) * delta)
    kbk = _dot_split(hb_ref[...], w3b_ref[...]) * jnp.exp(-((length - row) / length) * delta)
    kbk = jnp.where(row == 0.0, 0.0, kbk)
    norm = jnp.sum(jnp.abs(kf), axis=0, keepdims=True) + jnp.sum(jnp.abs(kbk), axis=0, keepdims=True)
    kf, kbk = kf / norm, kbk / norm
    for a in range(half):
        kb[a * pitch:a * pitch + bn, :] = kf[a * bn:(a + 1) * bn]
        kb[(half + a) * pitch:(half + a) * pitch + bn, :] = kbk[a * bn:(a + 1) * bn]
    _stage1(kb, a_n, 1, m1_ref, t2, a_n, bn, pitch)

    def mid(g, carry):
        spec_ref[0, 0, g] = jnp.dot(m2_ref[...], _stage2_rhs(t2, g, a_n, bn, pitch),
                                    preferred_element_type=F32).astype(spec_ref.dtype)
        return carry
    lax.fori_loop(0, a_n // PGROUP, mid, 0, unroll=2)


def _filter_spectra(length, w1, b1, w2, b2, w3, tables):
    a_n, bn, pitch = _fft_dims(length)
    hp = lax.Precision.HIGHEST
    t_f = jnp.arange(length, dtype=F32) / length
    band = jnp.arange(1, HY_BANDS + 1, dtype=F32)

    def hidden(t):
        ang = 2.0 * math.pi * t[:, None] * band
        feats = jnp.concatenate([t[:, None], jnp.sin(ang), jnp.cos(ang)], axis=-1)
        hid = jnp.sin(jnp.dot(feats, w1, precision=hp) + b1)
        return jnp.sin(jnp.dot(hid, w2, precision=hp) + b2)

    deltas = jnp.abs(jnp.linspace(math.log(HY_DECAY_TARGET) / HY_SLOW_DECAY,
                                  math.log(HY_DECAY_TARGET) / HY_FAST_DECAY, HY_C, dtype=F32))[None]
    _, m1f, m2, _, _ = tables
    nct = HY_C // LANES
    hid_f = hidden(t_f)
    hid_b = jnp.concatenate([hid_f[:1], hid_f[:0:-1]], axis=0)
    hidden_w = hid_f.shape[1]
    once = pl.Buffered(1)
    return pl.pallas_call(
        functools.partial(_filtspec_kernel, length=length),
        grid=(HY_ORDER, nct),
        in_specs=[pl.BlockSpec((length, hidden_w), lambda o, c: (0, 0), pipeline_mode=once),
                  pl.BlockSpec((length, hidden_w), lambda o, c: (0, 0), pipeline_mode=once),
                  pl.BlockSpec((hidden_w, LANES), lambda o, c: (0, o * 2 * nct + c)),
                  pl.BlockSpec((hidden_w, LANES), lambda o, c: (0, o * 2 * nct + nct + c)),
                  pl.BlockSpec((1, LANES), lambda o, c: (0, c)),
                  pl.BlockSpec(m1f.shape, lambda o, c: (0, 0, 0), pipeline_mode=once),
                  pl.BlockSpec(m2.shape, lambda o, c: (0, 0), pipeline_mode=once)],
        out_specs=pl.BlockSpec((1, 1, a_n // PGROUP, 2 * bn, PGROUP * LANES), lambda o, c: (o, c, 0, 0, 0)),
        out_shape=jax.ShapeDtypeStruct((HY_ORDER, nct, a_n // PGROUP, 2 * bn, PGROUP * LANES), BF16),
        scratch_shapes=[pltpu.VMEM((a_n * pitch, LANES), F32), pltpu.VMEM((2 * a_n * pitch, LANES), F32)],
        compiler_params=pltpu.CompilerParams(
            dimension_semantics=("parallel", "parallel"), vmem_limit_bytes=VMEM_LIMIT),
        name="filtspec",
    )(hid_f, hid_b, w3, w3, deltas, m1f, m2)


def _hygate_kernel(p_ref, w_ref, *rest, first):
    x = p_ref[0]
    n = x.shape[0]
    row = lax.broadcasted_iota(jnp.int32, x.shape, 0)
    prev = jnp.where(row == 0, 0.0, pltpu.roll(x, 1, axis=0))
    nxt = jnp.where(row == n - 1, 0.0, pltpu.roll(x, n - 1, axis=0))
    c = w_ref[0:1, :] * prev + w_ref[1:2, :] * x + w_ref[2:3, :] * nxt
    if first:
        (o_ref,) = rest
        o_ref[0] = c
    else:
        y_ref, z_ref, b_ref, o_ref = rest
        o_ref[0] = c * (y_ref[0] + z_ref[0] * b_ref[...])


def _hygate(u, part, conv_w, y=None, z=None, bias=None):
    bsz, length, _ = u.shape
    nct = HY_C // LANES
    blk = lambda b, c: (b, 0, c)
    in_specs = [pl.BlockSpec((1, length, LANES), lambda b, c: (b, 0, part * nct + c)),
                pl.BlockSpec((HY_SHORT, LANES), lambda b, c: (0, part * nct + c))]
    args = [u, conv_w]
    if y is not None:
        in_specs += [pl.BlockSpec((1, length, LANES), blk), pl.BlockSpec((1, length, LANES), blk),
                     pl.BlockSpec((1, LANES), lambda b, c: (0, c))]
        args += [y, z, bias]
    return pl.pallas_call(
        functools.partial(_hygate_kernel, first=y is None),
        grid=(bsz, nct),
        in_specs=in_specs,
        out_specs=pl.BlockSpec((1, length, LANES), blk),
        out_shape=jax.ShapeDtypeStruct((bsz, length, HY_C), F32),
        compiler_params=pltpu.CompilerParams(
            dimension_semantics=("parallel", "parallel"), vmem_limit_bytes=VMEM_LIMIT),
        name="hygate",
    )(*args)


def _hyena_mix(u, conv_w, spec, bias, tables):
    z = _hygate(u, 0, conv_w)
    for o in range(HY_ORDER):
        y = _fftconv(z, spec[o], tables)
        z = _hygate(u, o + 1, conv_w, y, z, bias[o:o + 1])
    return z


SUBLANES = 8
PAD = LRU_CONV - 1
SCAN_UNROLL = 8


def _lru_kernel(x_ref, wg_ref, pv_ref, h0_ref, o_ref, end_ref, xp, a_scr, b_scr, o_scr, *, steps, chunk):
    sub = lax.broadcasted_iota(jnp.int32, (SUBLANES, LRU_BW), 0)

    def gather(j, carry):
        xp[PAD + j] = x_ref[pl.ds(j, SUBLANES, stride=steps), :]
        return carry

    lax.fori_loop(0, steps, gather, 0, unroll=8)
    for k in range(PAD):
        tail = pltpu.roll(xp[steps + k], 1, axis=0)
        xp[k] = jnp.where(sub == 0, 0.0, tail)
        head = pltpu.roll(xp[PAD + k], SUBLANES - 1, axis=0)
        xp[PAD + steps + k] = jnp.where(sub == SUBLANES - 1, 0.0, head)

    n_chunks = steps // chunk
    for d in range(2):
        ba = pv_ref[d, 0:1, :]
        bx = pv_ref[d, 1:2, :]
        lam = pv_ref[d, 2:3, :]
        neg_c_sp = -RG_C * jax.nn.softplus(-lam)
        taps = [pv_ref[d, 3 + k:4 + k, :].reshape(1, 1, LRU_BW) for k in range(LRU_CONV)]
        w_gate = wg_ref[d, 0]
        base = 0 if d == 0 else PAD

        def gates(c, carry):
            j0 = pl.multiple_of(c * chunk, chunk)
            xc = taps[0] * xp[pl.ds(j0 + base, chunk)]
            for k in range(1, LRU_CONV):
                xc = xc + taps[k] * xp[pl.ds(j0 + base + k, chunk)]
            xc2 = xc.reshape(chunk * SUBLANES, LRU_BW)
            g = jnp.dot(xc2.astype(BF16), w_gate, preferred_element_type=F32)
            r = 0.5 * jnp.tanh(0.5 * (g[:, :LRU_BW] + ba)) + 0.5
            i = 0.5 * jnp.tanh(0.5 * (g[:, LRU_BW:] + bx)) + 0.5
            log_a = neg_c_sp * r
            a = jnp.exp(log_a)
            t = jnp.tanh(log_a)
            gap = -2.0 * t / (1.0 - t)
            root = jnp.where(gap > 0.0, gap * lax.rsqrt(gap), 0.0)
            b = root * (i * xc2)
            a_scr[pl.ds(j0, chunk)] = a.reshape(chunk, SUBLANES, LRU_BW)
            b_scr[pl.ds(j0, chunk)] = b.reshape(chunk, SUBLANES, LRU_BW)
            return carry

        lax.fori_loop(0, n_chunks, gates, 0)

        def steps8(k, carry):
            h, p = carry
            base = pl.multiple_of(k * SCAN_UNROLL if d == 0 else steps - SCAN_UNROLL * (k + 1), SCAN_UNROLL)
            for u in (range(SCAN_UNROLL) if d == 0 else range(SCAN_UNROLL - 1, -1, -1)):
                a = a_scr[base + u]
                h = a * h + b_scr[base + u]
                p = a * p
                b_scr[base + u] = h
                a_scr[base + u] = p
            return h, p

        h_loc, p_loc = lax.fori_loop(
            0, steps // SCAN_UNROLL, steps8,
            (jnp.zeros((SUBLANES, LRU_BW), F32), jnp.ones((SUBLANES, LRU_BW), F32)))

        carry = h0_ref[0, d:d + 1, :]
        cin = jnp.zeros((SUBLANES, LRU_BW), F32)
        order = range(SUBLANES) if d == 0 else range(SUBLANES - 1, -1, -1)
        for s in order:
            cin = jnp.where(sub == s, jnp.broadcast_to(carry, (SUBLANES, LRU_BW)), cin)
            carry = p_loc[s:s + 1, :] * carry + h_loc[s:s + 1, :]
        end_ref[0, d:d + 1, :] = carry

        def fix(c, carry_):
            j0 = pl.multiple_of(c * chunk, chunk)
            h = b_scr[pl.ds(j0, chunk)] + a_scr[pl.ds(j0, chunk)] * cin[None]
            if d == 0:
                o_scr[pl.ds(j0, chunk)] = h
            else:
                h = h + o_scr[pl.ds(j0, chunk)]
                for jj in range(chunk):
                    o_ref[pl.ds(j0 + jj, SUBLANES, stride=steps), :] = h[jj]
            return carry_

        lax.fori_loop(0, n_chunks, fix, 0)


def _rglru(u, length, lconv, lwa, lba, lwx, lbx, llam, h0):
    m = u.shape[0]
    bsz = m // length
    steps = length // SUBLANES
    chunk = min(32, steps)
    wg = jnp.concatenate([lwa, lwx], axis=-1).astype(BF16)
    pv = jnp.concatenate([lba[:, None], lbx[:, None], llam[:, None], lconv,
                          jnp.zeros((2, 1, LRU_C), F32)], axis=1)
    seg = pltpu.VMEM((steps, SUBLANES, LRU_BW), F32)
    return pl.pallas_call(
        functools.partial(_lru_kernel, steps=steps, chunk=chunk),
        grid=(bsz, LRU_BLOCKS),
        in_specs=[pl.BlockSpec((length, LRU_BW), lambda b, n: (b, OFF_LRU_X // LRU_BW + n)),
                  pl.BlockSpec((2, 1, LRU_BW, 2 * LRU_BW), lambda b, n: (0, n, 0, 0)),
                  pl.BlockSpec((2, SUBLANES, LRU_BW), lambda b, n: (0, 0, n)),
                  pl.BlockSpec((1, 2, LRU_BW), lambda b, n: (b, 0, n))],
        out_specs=[pl.BlockSpec((length, LRU_BW), lambda b, n: (b, n)),
                   pl.BlockSpec((1, 2, LRU_BW), lambda b, n: (b, 0, n))],
        out_shape=[jax.ShapeDtypeStruct((m, LRU_C), F32),
                   jax.ShapeDtypeStruct((bsz, 2, LRU_C), F32)],
        scratch_shapes=[pltpu.VMEM((steps + 2 * PAD, SUBLANES, LRU_BW), F32), seg, seg, seg],
        compiler_params=pltpu.CompilerParams(
            dimension_semantics=("parallel", "parallel"), vmem_limit_bytes=VMEM_LIMIT),
        name="rglru",
    )(u, wg, pv, h0)


def _rope_tables(length):
    rows = length // GRID_W
    row = jnp.repeat(jnp.arange(rows, dtype=F32), GRID_W)
    col = jnp.tile(jnp.arange(GRID_W, dtype=F32), rows)
    n_freq = ROPE_D // 4
    inv = ROPE_BASE ** (-jnp.arange(n_freq, dtype=F32) / n_freq)
    ang = jnp.concatenate([row[:, None] * inv, col[:, None] * inv], axis=-1)
    return jnp.cos(ang), jnp.sin(ang)


def _rope_lane_tables(rope):
    cos, sin = rope
    z = jnp.zeros_like(cos)
    return (jnp.concatenate([cos, cos, z, z], axis=1), jnp.concatenate([-sin, z, z, z], axis=1),
            jnp.concatenate([z, sin, z, z], axis=1))


def _pad_heads_q(wuq):
    w = wuq.reshape(Q_RANK, MLA_H, NOPE_D + ROPE_D)
    w = jnp.pad(w, ((0, 0), (0, 0), (0, HEAD_QK - NOPE_D - ROPE_D)))
    return w.reshape(Q_RANK, MLA_H * HEAD_QK)


def _rms(xv, g):
    return xv * lax.rsqrt(jnp.mean(xv * xv, axis=-1, keepdims=True) + EPS) * g


ROW_SPLIT = 2


def _proj_in_kernel(x_ref, g_ref, mod_ref, w_ref, o_ref):
    sub = x_ref.shape[0] // ROW_SPLIT
    for part in range(ROW_SPLIT):
        rows = slice(part * sub, (part + 1) * sub)
        y = _rms(x_ref[rows, :], g_ref[...])
        h = (y * (1.0 + mod_ref[0, 1:2, :]) + mod_ref[0, 0:1, :]).astype(BF16)
        o_ref[rows, :] = jnp.dot(h, w_ref[...], preferred_element_type=F32)


def _proj_in(x, g, shift, scale, w, layer):
    bsz, length, d = x.shape
    m, n = bsz * length, w.shape[2]
    tm = _pick(length, (512, 256, 128))
    bps = length // tm
    mod = jnp.stack([shift, scale], axis=1)
    mod_idx = (lambda i: (i // bps, 0, 0)) if shift.shape[0] > 1 else (lambda i: (0, 0, 0))
    return pl.pallas_call(
        _proj_in_kernel,
        grid=(m // tm,),
        in_specs=[pl.BlockSpec((tm, d), lambda i: (i, 0)),
                  pl.BlockSpec((1, d), lambda i: (0, 0)),
                  pl.BlockSpec((1, 2, d), mod_idx),
                  pl.BlockSpec((None, d, n), lambda i: (layer, 0, 0), pipeline_mode=pl.Buffered(1))],
        out_specs=pl.BlockSpec((tm, n), lambda i: (i, 0)),
        out_shape=jax.ShapeDtypeStruct((m, n), F32),
        compiler_params=pltpu.CompilerParams(dimension_semantics=("parallel",), vmem_limit_bytes=VMEM_LIMIT),
        name="proj_in",
    )(x.reshape(m, d), g[None], mod, w)


def _proj_qkv_kernel(*refs, rope, with_q):
    refs = list(refs)
    uq_ref = refs.pop(0) if with_q else None
    ukv_ref, ukr_ref = refs.pop(0), refs.pop(0)
    gq_ref, wq_ref = (refs.pop(0), refs.pop(0)) if with_q else (None, None)
    gkv_ref, wkv_ref = refs.pop(0), refs.pop(0)
    cos_ref, sa_ref, sb_ref = (refs.pop(0), refs.pop(0), refs.pop(0)) if rope else (None, None, None)
    q_ref = refs.pop(0) if with_q else None
    kv_ref, kr_ref = refs

    def rot(blk):
        if not rope:
            return blk
        return (blk * cos_ref[...] + pltpu.roll(blk, LANES - 32, axis=1) * sa_ref[...]
                + pltpu.roll(blk, 32, axis=1) * sb_ref[...])

    if with_q:
        nq = _rms(uq_ref[...], gq_ref[...]).astype(BF16)
        q = jnp.dot(nq, wq_ref[...], preferred_element_type=F32)
        for hd in range(MLA_H):
            lo = hd * HEAD_QK
            q_ref[:, lo:lo + NOPE_D] = (q[:, lo:lo + NOPE_D] * Q_SCALE).astype(BF16)
            q_ref[:, lo + NOPE_D:lo + HEAD_QK] = (rot(q[:, lo + NOPE_D:lo + HEAD_QK]) * Q_SCALE).astype(BF16)
    nkv = _rms(ukv_ref[...], gkv_ref[...]).astype(BF16)
    kv_ref[...] = jnp.dot(nkv, wkv_ref[...], preferred_element_type=F32).astype(BF16)
    kr_ref[...] = rot(ukr_ref[...]).astype(BF16)


def _proj_qkv(u, length, gq, wuq_pad, gkv, wukv, layer, rope_tabs, with_q=True):
    m = u.shape[0]
    tm = _pick(length, (512, 256, 128))
    bps = length // tm
    row = lambda i: (i, 0)
    const = lambda i: (0, 0)
    wsel = lambda i: (layer, 0, 0)
    in_specs, args = [], []
    if with_q:
        in_specs.append(pl.BlockSpec((tm, Q_RANK), lambda i: (i, OFF_MLA_Q // Q_RANK)))
        args.append(u)
    in_specs += [pl.BlockSpec((tm, KV_RANK), lambda i: (i, OFF_MLA_KV // KV_RANK)),
                 pl.BlockSpec((tm, LANES), lambda i: (i, OFF_MLA_KR // LANES))]
    args += [u, u]
    if with_q:
        in_specs += [pl.BlockSpec((1, Q_RANK), const), pl.BlockSpec((None,) + wuq_pad.shape[1:], wsel)]
        args += [gq[None], wuq_pad]
    in_specs += [pl.BlockSpec((1, KV_RANK), const), pl.BlockSpec((None,) + wukv.shape[1:], wsel)]
    args += [gkv[None], wukv]
    if rope_tabs is not None:
        in_specs += [pl.BlockSpec((tm, LANES), lambda i: (i % bps, 0))] * 3
        args += list(rope_tabs)
    out_specs = [pl.BlockSpec((tm, MLA_H * HEAD_QK), row), pl.BlockSpec((tm, LANES), row)]
    out_shape = [jax.ShapeDtypeStruct((m, MLA_H * HEAD_QK), BF16), jax.ShapeDtypeStruct((m, LANES), BF16)]
    if with_q:
        out_specs.insert(0, pl.BlockSpec((tm, MLA_H * HEAD_QK), row))
        out_shape.insert(0, jax.ShapeDtypeStruct((m, MLA_H * HEAD_QK), BF16))
    outs = pl.pallas_call(
        functools.partial(_proj_qkv_kernel, rope=rope_tabs is not None, with_q=with_q),
        grid=(m // tm,),
        in_specs=in_specs, out_specs=out_specs, out_shape=out_shape,
        compiler_params=pltpu.CompilerParams(dimension_semantics=("parallel",), vmem_limit_bytes=VMEM_LIMIT),
        name="proj_qkv",
    )(*args)
    return outs if with_q else [None] + list(outs)


def _mix_kernel(yhy_ref, hs_ref, gate_ref, ymla_ref, hg_ref, w_ref, x_ref, ng_ref, mod_ref, o_ref):
    sub = x_ref.shape[0] // ROW_SPLIT
    for part in range(ROW_SPLIT):
        rows = slice(part * sub, (part + 1) * sub)
        y = jnp.concatenate([yhy_ref[rows, :], jax.nn.gelu(gate_ref[rows, :], approximate=True) * hs_ref[rows, :],
                             ymla_ref[rows, :]], axis=1)
        heads = [_rms(y[:, hd * HEAD_W:(hd + 1) * HEAD_W], hg_ref[:, hd * HEAD_W:(hd + 1) * HEAD_W]).astype(BF16)
                 for hd in range(N_MIX_HEADS)]
        acc = jnp.dot(jnp.concatenate(heads, axis=1), w_ref[...], preferred_element_type=F32)
        o_ref[rows, :] = x_ref[rows, :] + mod_ref[0] * _rms(acc, ng_ref[...])


def _mix_out(x, y_hy, h_sum, u, y_mla, head_g, w_out, layer, ng_post, gate):
    bsz, length, d = x.shape
    m = bsz * length
    tm = _pick(length, (512, 256, 128))
    bps = length // tm
    row = lambda i: (i, 0)
    const = lambda i: (0, 0)
    mod_idx = (lambda i: (i // bps, 0, 0)) if gate.shape[0] > 1 else (lambda i: (0, 0, 0))
    out = pl.pallas_call(
        _mix_kernel,
        grid=(m // tm,),
        in_specs=[pl.BlockSpec((tm, HY_C), row), pl.BlockSpec((tm, LRU_C), row),
                  pl.BlockSpec((tm, LRU_C), lambda i: (i, OFF_LRU_G // LRU_C)),
                  pl.BlockSpec((tm, MLA_H * V_D), row),
                  pl.BlockSpec((1, D_MIX), const),
                  pl.BlockSpec((None,) + w_out.shape[1:], lambda i: (layer, 0, 0), pipeline_mode=pl.Buffered(1)),
                  pl.BlockSpec((tm, d), row), pl.BlockSpec((1, d), const), pl.BlockSpec((1, 1, d), mod_idx)],
        out_specs=pl.BlockSpec((tm, d), row),
        out_shape=jax.ShapeDtypeStruct((m, d), F32),
        compiler_params=pltpu.CompilerParams(dimension_semantics=("parallel",), vmem_limit_bytes=VMEM_LIMIT),
        name="mix_out",
    )(y_hy.reshape(m, HY_C), h_sum.reshape(m, LRU_C), u, y_mla.reshape(m, MLA_H * V_D), head_g[None], w_out,
      x.reshape(m, d), ng_post[None], gate[:, None])
    return out.reshape(bsz, length, d)


HALO = 16


def _ffn_kernel(x_ref, xprev_ref, xnext_ref, ng_ref, mod_ref, wg_ref, wv_ref, cg_ref, cv_ref, wd_ref,
                o_ref, hbuf, *, blocks_per_seq):
    i = pl.program_id(0)
    j = pl.program_id(1)
    tm = x_ref.shape[0]

    def norm_mod(xv):
        y = xv * lax.rsqrt(jnp.mean(xv * xv, axis=-1, keepdims=True) + EPS) * ng_ref[0:1, :]
        return (y * (1.0 + mod_ref[0, 1:2, :]) + mod_ref[0, 0:1, :]).astype(BF16)

    @pl.when(j == 0)
    def _():
        first = (i % blocks_per_seq) == 0
        last = (i % blocks_per_seq) == blocks_per_seq - 1
        hp = norm_mod(xprev_ref[...])
        hn = norm_mod(xnext_ref[...])
        hbuf[0:HALO] = jnp.where(first, jnp.zeros_like(hp), hp)
        hbuf[HALO:HALO + tm] = norm_mod(x_ref[...])
        hbuf[HALO + tm:] = jnp.where(last, jnp.zeros_like(hn), hn)
        o_ref[...] = jnp.zeros_like(o_ref)

    hb = hbuf[...]

    def conv(up, c_ref):
        prev = pltpu.roll(up, 1, axis=0)[HALO:HALO + tm]
        nxt = pltpu.roll(up, tm + 2 * HALO - 1, axis=0)[HALO:HALO + tm]
        return c_ref[0:1, :] * prev + c_ref[1:2, :] * up[HALO:HALO + tm] + c_ref[2:3, :] * nxt

    gate = conv(jnp.dot(hb, wg_ref[...], preferred_element_type=F32), cg_ref)
    val = conv(jnp.dot(hb, wv_ref[...], preferred_element_type=F32), cv_ref)
    act = (jax.nn.gelu(gate, approximate=True) * val).astype(BF16)
    o_ref[...] += jnp.dot(act, wd_ref[...], preferred_element_type=F32)

    @pl.when(j == pl.num_programs(1) - 1)
    def _():
        y = o_ref[...]
        y = y * lax.rsqrt(jnp.mean(y * y, axis=-1, keepdims=True) + EPS) * ng_ref[1:2, :]
        o_ref[...] = x_ref[...] + mod_ref[0, 2:3, :] * y


def _ffn_sublayer(x, ng_pre, ng_post, shift, scale, gate, w_up, w_conv, w_down, layer):
    bsz, length, d = x.shape
    m = bsz * length
    tm = _pick(length, (1024, 512, 256, 128))
    tf = _pick(D_FF, (512, 256, 128))
    nf = D_FF // tf
    bps = length // tm
    hb = tm // HALO
    nrow = m // HALO
    per_batch = shift.shape[0] > 1
    ng = jnp.stack([ng_pre, ng_post])
    mod = jnp.stack([shift, scale, gate], axis=1)
    out = pl.pallas_call(
        functools.partial(_ffn_kernel, blocks_per_seq=bps),
        grid=(m // tm, nf),
        in_specs=[pl.BlockSpec((tm, d), lambda i, j: (i, 0), pipeline_mode=pl.Buffered(1)),
                  pl.BlockSpec((HALO, d), lambda i, j: (jnp.maximum(i * hb - 1, 0), 0)),
                  pl.BlockSpec((HALO, d), lambda i, j: (jnp.minimum((i + 1) * hb, nrow - 1), 0)),
                  pl.BlockSpec((2, d), lambda i, j: (0, 0)),
                  pl.BlockSpec((1, 3, d), (lambda i, j: (i // bps, 0, 0)) if per_batch else (lambda i, j: (0, 0, 0))),
                  pl.BlockSpec((None, d, tf), lambda i, j: (layer, 0, j)),
                  pl.BlockSpec((None, d, tf), lambda i, j: (layer, 0, j + nf)),
                  pl.BlockSpec((None, FFN_CONV, tf), lambda i, j: (layer, 0, j)),
                  pl.BlockSpec((None, FFN_CONV, tf), lambda i, j: (layer, 0, j + nf)),
                  pl.BlockSpec((None, tf, d), lambda i, j: (layer, j, 0))],
        out_specs=pl.BlockSpec((tm, d), lambda i, j: (i, 0)),
        out_shape=jax.ShapeDtypeStruct((m, d), F32),
        scratch_shapes=[pltpu.VMEM((tm + 2 * HALO, d), BF16)],
        compiler_params=pltpu.CompilerParams(
            dimension_semantics=("parallel", "arbitrary"), vmem_limit_bytes=FFN_VMEM_LIMIT),
        name="ffn",
    )(x.reshape(m, d), x.reshape(m, d), x.reshape(m, d), ng, mod, w_up, w_up, w_conv, w_conv, w_down)
    return out.reshape(bsz, length, d)


def kernel(x, c, ctx, c_ctx, ada_w, ada_b, norm_g, w_in, hy_conv, hy_w1, hy_b1, hy_w2, hy_b2, hy_w3, hy_bias, lru_conv, lru_wa, lru_ba, lru_wx, lru_bx, lru_lam, mla_gq, mla_gkv, mla_wuq, mla_wukv, head_g, w_out, ffn_up, ffn_conv, ffn_down):
    depth = ada_w.shape[0]
    bsz, seq, _ = x.shape
    ctx_len = ctx.shape[1]
    rope_tabs = _rope_lane_tables(_rope_tables(seq))
    dft_lat, dft_ctx = _dft_tables(seq), _dft_tables(ctx_len)
    xc = ctx
    w_in_b = jnp.pad(w_in, ((0, 0), (0, 0), (0, NOPE_D - ROPE_D))).astype(BF16)
    wuq_pad = jax.vmap(_pad_heads_q)(mla_wuq).astype(BF16)
    wukv_b, w_out_b = mla_wukv.astype(BF16), w_out.astype(BF16)
    up_b, down_b = ffn_up.astype(BF16), ffn_down.astype(BF16)
    mods = _adaln(jnp.concatenate([c, c_ctx[None]], axis=0), ada_w, ada_b)

    def seq3(t, length):
        return None if t is None else t.reshape(bsz, length, t.shape[-1])

    for l in range(depth):
        need_ctx = l < depth - 1
        ng = norm_g[l]
        sh1, sc1, g1, sh2, sc2, g2 = jnp.split(mods[l, :bsz], N_MOD, axis=-1)
        csh1, csc1, cg1, csh2, csc2, cg2 = jnp.split(mods[l, bsz:], N_MOD, axis=-1)
        hy_f = (hy_w1[l], hy_b1[l], hy_w2[l], hy_b2[l], hy_w3[l])
        lru_p = (lru_conv[l], lru_wa[l], lru_ba[l], lru_wx[l], lru_bx[l], lru_lam[l])

        u = _proj_in(x, ng[0], sh1, sc1, w_in_b, l)
        uc = _proj_in(xc, ng[0], csh1, csc1, w_in_b, l)
        u3, uc3 = seq3(u, seq), seq3(uc, ctx_len)

        hc_sum, hc_end = _rglru(uc, ctx_len, *lru_p, jnp.zeros((bsz, 2, LRU_C), F32))
        q_c, kv_c, kr_c = [seq3(t, ctx_len) for t in
                           _proj_qkv(uc, ctx_len, mla_gq[l], wuq_pad, mla_gkv[l], wukv_b, l, None, with_q=need_ctx)]

        y_hy = _hyena_mix(u3, hy_conv[l], _filter_spectra(seq, *hy_f, dft_lat), hy_bias[l], dft_lat)
        h_sum, _ = _rglru(u, seq, *lru_p, hc_end)
        q, kv, kr = [seq3(t, seq) for t in
                     _proj_qkv(u, seq, mla_gq[l], wuq_pad, mla_gkv[l], wukv_b, l, rope_tabs)]
        y_mla = _attention(q, [(kv_c, kr_c), (kv, kr)])
        x = _mix_out(x, y_hy, h_sum, u, y_mla, head_g[l], w_out_b, l, ng[1], g1)

        if need_ctx:
            yc_hy = _hyena_mix(uc3, hy_conv[l], _filter_spectra(ctx_len, *hy_f, dft_ctx), hy_bias[l], dft_ctx)
            yc_mla = _attention(q_c, [(kv_c, kr_c)])
            xc = _mix_out(xc, yc_hy, hc_sum, uc, yc_mla, head_g[l], w_out_b, l, ng[1], cg1)

        x = _ffn_sublayer(x, ng[2], ng[3], sh2, sc2, g2, up_b, ffn_conv, down_b, l)
        if need_ctx:
            xc = _ffn_sublayer(xc, ng[2], ng[3], csh2, csc2, cg2, up_b, ffn_conv, down_b, l)
    return x
```

```python
import functools
import math

import jax
import jax.numpy as jnp
from jax import lax
from jax.experimental import pallas as pl
from jax.experimental.pallas import tpu as pltpu

F32 = jnp.float32
BF16 = jnp.bfloat16

D_MODEL = 2048
GRID_W = 64
HEAD_W = 128
HY_C = 512
HY_ORDER = 2
HY_SHORT = 3
HY_BANDS = 16
HY_FAST_DECAY = 0.3
HY_SLOW_DECAY = 1.5
HY_DECAY_TARGET = 1e-2
LRU_C = 512
LRU_BLOCKS = 4
LRU_BW = LRU_C // LRU_BLOCKS
LRU_CONV = 4
RG_C = 8.0
MLA_H = 8
Q_RANK = 512
KV_RANK = 256
NOPE_D = 128
ROPE_D = 64
V_D = 128
ROPE_BASE = 10000.0
MLA_SCALE = (NOPE_D + ROPE_D) ** -0.5
Q_SCALE = MLA_SCALE * math.log2(math.e)
D_MIX = HY_C + LRU_C + MLA_H * V_D
N_MIX_HEADS = D_MIX // HEAD_W
D_FF = 5632
FFN_CONV = 3
N_MOD = 6
EPS = 1e-6

OFF_HY = 0
OFF_LRU_G = OFF_HY + (HY_ORDER + 1) * HY_C
OFF_MLA_Q = OFF_LRU_G + LRU_C
OFF_LRU_X = OFF_MLA_Q + Q_RANK
OFF_MLA_KV = OFF_LRU_X + LRU_C
OFF_MLA_KR = OFF_MLA_KV + KV_RANK
D_IN = OFF_MLA_KR + ROPE_D

HEAD_QK = 2 * NOPE_D
VMEM_LIMIT = 48 * 1024 * 1024
FFN_VMEM_LIMIT = 56 * 1024 * 1024


def _pick(n, prefs):
    for p in prefs:
        if n % p == 0:
            return p
    return n


def _adaln_kernel(c_ref, w_ref, b_ref, o_ref):
    c = c_ref[...]
    a = (c * jax.nn.sigmoid(c)).astype(BF16)
    o_ref[...] = jnp.dot(a, w_ref[...].astype(BF16), preferred_element_type=F32) + b_ref[...]


def _adaln(cond, ada_w, ada_b):
    depth, d, n = ada_w.shape
    rows = cond.shape[0]
    rp = -(-rows // SUBLANES) * SUBLANES
    cond = jnp.pad(cond, ((0, rp - rows), (0, 0)))
    tn = 1024
    out = pl.pallas_call(
        _adaln_kernel,
        grid=(depth, n // tn),
        in_specs=[pl.BlockSpec((rp, d), lambda l, j: (0, 0)),
                  pl.BlockSpec((None, d, tn), lambda l, j: (l, 0, j)),
                  pl.BlockSpec((None, 1, tn), lambda l, j: (l, 0, j))],
        out_specs=pl.BlockSpec((None, rp, tn), lambda l, j: (l, 0, j)),
        out_shape=jax.ShapeDtypeStruct((depth, rp, n), F32),
        compiler_params=pltpu.CompilerParams(
            dimension_semantics=("parallel", "parallel"), vmem_limit_bytes=VMEM_LIMIT),
        name="adaln",
    )(cond, ada_w, ada_b[:, None])
    return out[:, :rows]


Q_SPLIT = 4

def _attn_kernel(*refs, n_seg):
    q_ref = refs[0]
    seg_refs = refs[1:1 + 3 * n_seg]
    o_ref = refs[1 + 3 * n_seg]
    k_scr, v_scr = refs[2 + 3 * n_seg:]

    @pl.when(pl.program_id(2) == 0)
    def _():
        off = 0
        for s in range(n_seg):
            kn_ref, v_ref, kr_ref = seg_refs[3 * s:3 * s + 3]
            n = kn_ref.shape[1]
            k_scr[off:off + n, 0:NOPE_D] = kn_ref[0]
            k_scr[off:off + n, NOPE_D:HEAD_QK] = kr_ref[0]
            v_scr[off:off + n, :] = v_ref[0]
            off += n

    sub = q_ref.shape[1] // Q_SPLIT
    parts = [slice(part * sub, (part + 1) * sub) for part in range(Q_SPLIT)]
    def qk(rows):
        return lax.dot_general(q_ref[0, rows], k_scr[...], (((1,), (1,)), ((), ())), preferred_element_type=F32)

    def soft(s):
        m = jnp.max(s, axis=-1, keepdims=True)
        p = jnp.exp2(s - m)
        return p.astype(BF16), jnp.sum(p, axis=-1, keepdims=True)

    scores, probs = {}, {}
    for t in range(Q_SPLIT + 2):
        if t < Q_SPLIT:
            scores[t] = qk(parts[t])
        if 1 <= t <= Q_SPLIT:
            probs[t - 1] = soft(scores.pop(t - 1))
        if t >= 2:
            p, l = probs.pop(t - 2)
            o_ref[0, parts[t - 2]] = jnp.dot(p, v_scr[...], preferred_element_type=F32) / l


def _attention(q, segs):
    bsz, lq, _ = q.shape
    tq = _pick(lq, (1024, 512, 256, 128))
    lk = sum(kv.shape[1] for kv, _ in segs)
    in_specs = [pl.BlockSpec((1, tq, HEAD_QK), lambda b, h, i: (b, i, h))]
    args = [q]
    for kv, kr in segs:
        n = kv.shape[1]
        in_specs += [pl.BlockSpec((1, n, NOPE_D), lambda b, h, i: (b, 0, 2 * h)),
                     pl.BlockSpec((1, n, V_D), lambda b, h, i: (b, 0, 2 * h + 1)),
                     pl.BlockSpec((1, n, NOPE_D), lambda b, h, i: (b, 0, 0))]
        args += [kv, kv, kr]
    return pl.pallas_call(
        functools.partial(_attn_kernel, n_seg=len(segs)),
        grid=(bsz, MLA_H, lq // tq),
        in_specs=in_specs,
        out_specs=pl.BlockSpec((1, tq, V_D), lambda b, h, i: (b, i, h)),
        out_shape=jax.ShapeDtypeStruct((bsz, lq, MLA_H * V_D), F32),
        scratch_shapes=[pltpu.VMEM((lk, HEAD_QK), BF16), pltpu.VMEM((lk, V_D), BF16)],
        compiler_params=pltpu.CompilerParams(
            dimension_semantics=("parallel", "parallel", "arbitrary"), vmem_limit_bytes=VMEM_LIMIT),
        name="attn",
    )(*args)


LANES = 128
PGROUP = 2
STAGE_UNROLL = 8


def _fft_dims(length):
    n = 2 * length
    bn = 128 if length >= 1024 else 32
    return n // bn, bn, bn + SUBLANES


def _dft_tables(length):
    a_n, bn, _ = _fft_dims(length)
    n, half = a_n * bn, a_n // 2
    p = jnp.arange(a_n, dtype=jnp.int32)
    b = jnp.arange(bn, dtype=jnp.int32)
    nn = bn * p[None, None, :] + b[:, None, None]
    ang = (2.0 * math.pi / n) * ((p[None, :, None] * nn) % n).astype(F32)
    cr, ci = jnp.cos(ang), -jnp.sin(ang)
    m1 = jnp.concatenate([jnp.concatenate([cr[:, :, :half], -ci[:, :, :half]], axis=2),
                          jnp.concatenate([ci[:, :, :half], cr[:, :, :half]], axis=2)], axis=1)
    m1f = jnp.concatenate([cr, ci], axis=1)
    crt, cit = jnp.swapaxes(cr, 1, 2)[:, :half], jnp.swapaxes(ci, 1, 2)[:, :half]
    m4 = jnp.concatenate([jnp.concatenate([crt, cit], axis=2),
                          jnp.concatenate([-cit, crt], axis=2)], axis=1) / n
    ang2 = (2.0 * math.pi / bn) * ((b[:, None] * b[None, :]) % bn).astype(F32)
    wr, wi = jnp.cos(ang2), -jnp.sin(ang2)
    m2 = jnp.concatenate([jnp.concatenate([wr, -wi], axis=1), jnp.concatenate([wi, wr], axis=1)], axis=0)
    m3 = jnp.concatenate([jnp.concatenate([wr, wi], axis=1), jnp.concatenate([-wi, wr], axis=1)], axis=0)
    return tuple(t.astype(BF16) for t in (m1, m1f, m2, m3, m4))


def _stage1(src, rows_per_part, parts, m_ref, t2, a_n, bn, pitch):
    def body(b, carry):
        rhs = jnp.concatenate([src[pl.ds(part * rows_per_part * pitch + b, rows_per_part, stride=pitch), :]
                               for part in range(parts)], axis=0)
        t = jnp.dot(m_ref[b], rhs.astype(BF16), preferred_element_type=F32)
        t2[pl.ds(b, a_n, stride=pitch), :] = t[:a_n]
        t2[pl.ds(a_n * pitch + b, a_n, stride=pitch), :] = t[a_n:]
        return carry
    lax.fori_loop(0, bn, body, 0, unroll=STAGE_UNROLL)


def _stage2_rhs(t2, g, a_n, bn, pitch):
    cols = []
    for k in range(PGROUP):
        r0 = pl.multiple_of((g * PGROUP + k) * pitch, SUBLANES)
        cols.append(jnp.concatenate([t2[pl.ds(r0, bn), :], t2[pl.ds(a_n * pitch + r0, bn), :]], axis=0))
    return jnp.concatenate(cols, axis=1).astype(BF16)


def _fftconv_kernel(z_ref, spec_ref, m1_ref, m2_ref, m3_ref, m4_ref, y_ref, zb, t2, *, length):
    a_n, bn, pitch = _fft_dims(length)
    half = a_n // 2
    for bi in range(2):
        for a in range(half):
            zb[(bi * half + a) * pitch:(bi * half + a) * pitch + bn, :] = z_ref[bi, a * bn:(a + 1) * bn, :]
    _stage1(zb, half, 2, m1_ref, t2, a_n, bn, pitch)

    def mid(g, carry):
        x = jnp.dot(m2_ref[...], _stage2_rhs(t2, g, a_n, bn, pitch), preferred_element_type=F32)
        k = spec_ref[0, 0, g].astype(F32)
        xr, xi, kr, ki = x[:bn], x[bn:], k[:bn], k[bn:]
        y = jnp.concatenate([xr * kr - xi * ki, xr * ki + xi * kr], axis=0).astype(BF16)
        u = jnp.dot(m3_ref[...], y, preferred_element_type=F32)
        for kk in range(PGROUP):
            r0 = pl.multiple_of((g * PGROUP + kk) * pitch, SUBLANES)
            t2[pl.ds(r0, bn), :] = u[:bn, kk * LANES:(kk + 1) * LANES]
            t2[pl.ds(a_n * pitch + r0, bn), :] = u[bn:, kk * LANES:(kk + 1) * LANES]
        return carry
    lax.fori_loop(0, a_n // PGROUP, mid, 0, unroll=2)

    def last(b, carry):
        rhs = jnp.concatenate([t2[pl.ds(b, a_n, stride=pitch), :],
                               t2[pl.ds(a_n * pitch + b, a_n, stride=pitch), :]], axis=0)
        y = jnp.dot(m4_ref[b], rhs.astype(BF16), preferred_element_type=F32)
        zb[pl.ds(b, half, stride=pitch), :] = y[:half]
        zb[pl.ds(half * pitch + b, half, stride=pitch), :] = y[half:]
        return carry
    lax.fori_loop(0, bn, last, 0, unroll=STAGE_UNROLL)
    for bi in range(2):
        for a in range(half):
            y_ref[bi, a * bn:(a + 1) * bn, :] = zb[(bi * half + a) * pitch:(bi * half + a) * pitch + bn, :]


def _fftconv(z, spec, tables):
    bsz, length, chans = z.shape
    assert bsz == 2, "the two batch rows are packed as one complex signal"
    a_n, bn, pitch = _fft_dims(length)
    m1, _, m2, m3, m4 = tables
    const3 = lambda c: (0, 0, 0)
    const2 = lambda c: (0, 0)
    once = pl.Buffered(1)
    return pl.pallas_call(
        functools.partial(_fftconv_kernel, length=length),
        grid=(chans // LANES,),
        in_specs=[pl.BlockSpec((2, length, LANES), lambda c: (0, 0, c)),
                  pl.BlockSpec((1, 1, a_n // PGROUP, 2 * bn, PGROUP * LANES), lambda c: (0, c, 0, 0, 0)),
                  pl.BlockSpec(m1.shape, const3, pipeline_mode=once),
                  pl.BlockSpec(m2.shape, const2, pipeline_mode=once),
                  pl.BlockSpec(m3.shape, const2, pipeline_mode=once),
                  pl.BlockSpec(m4.shape, const3, pipeline_mode=once)],
        out_specs=pl.BlockSpec((2, length, LANES), lambda c: (0, 0, c)),
        out_shape=jax.ShapeDtypeStruct(z.shape, F32),
        scratch_shapes=[pltpu.VMEM((a_n * pitch, LANES), F32), pltpu.VMEM((2 * a_n * pitch, LANES), F32)],
        compiler_params=pltpu.CompilerParams(dimension_semantics=("parallel",), vmem_limit_bytes=VMEM_LIMIT),
        name="fftconv",
    )(z, spec[None], m1, m2, m3, m4)


def _split_bf16(x):
    hi = x.astype(BF16)
    return hi, (x - hi.astype(F32)).astype(BF16)


def _dot_split(a, b):
    ah, al = _split_bf16(a)
    bh, bl = _split_bf16(b)
    d = functools.partial(jnp.dot, preferred_element_type=F32)
    return d(ah, bh) + (d(al, bh) + d(ah, bl))


def _filtspec_kernel(hf_ref, hb_ref, w3f_ref, w3b_ref, dec_ref, m1_ref, m2_ref, spec_ref, kb, t2, *, length):
    a_n, bn, pitch = _fft_dims(length)
    half = a_n // 2
    row = lax.broadcasted_iota(jnp.int32, (length, 1), 0).astype(F32)
    delta = dec_ref[...]
    kf = _dot_split(hf_ref[...], w3f_ref[...]) * jnp.exp(-(row / length) * delta)
    kbk = _dot_split(hb_ref[...], w3b_ref[...]) * jnp.exp(-((length - row) / length) * delta)
    kbk = jnp.where(row == 0.0, 0.0, kbk)
    norm = jnp.sum(jnp.abs(kf), axis=0, keepdims=True) + jnp.sum(jnp.abs(kbk), axis=0, keepdims=True)
    kf, kbk = kf / norm, kbk / norm
    for a in range(half):
        kb[a * pitch:a * pitch + bn, :] = kf[a * bn:(a + 1) * bn]
        kb[(half + a) * pitch:(half + a) * pitch + bn, :] = kbk[a * bn:(a + 1) * bn]
    _stage1(kb, a_n, 1, m1_ref, t2, a_n, bn, pitch)

    def mid(g, carry):
        spec_ref[0, 0, g] = jnp.dot(m2_ref[...], _stage2_rhs(t2, g, a_n, bn, pitch),
                                    preferred_element_type=F32).astype(spec_ref.dtype)
        return carry
    lax.fori_loop(0, a_n // PGROUP, mid, 0, unroll=2)


def _filter_spectra(length, w1, b1, w2, b2, w3, tables):
    a_n, bn, pitch = _fft_dims(length)
    hp = lax.Precision.HIGHEST
    t_f = jnp.arange(length, dtype=F32) / length
    band = jnp.arange(1, HY_BANDS + 1, dtype=F32)

    def hidden(t):
        ang = 2.0 * math.pi * t[:, None] * band
        feats = jnp.concatenate([t[:, None], jnp.sin(ang), jnp.cos(ang)], axis=-1)
        hid = jnp.sin(jnp.dot(feats, w1, precision=hp) + b1)
        return jnp.sin(jnp.dot(hid, w2, precision=hp) + b2)

    deltas = jnp.abs(jnp.linspace(math.log(HY_DECAY_TARGET) / HY_SLOW_DECAY,
                                  math.log(HY_DECAY_TARGET) / HY_FAST_DECAY, HY_C, dtype=F32))[None]
    _, m1f, m2, _, _ = tables
    nct = HY_C // LANES
    hid_f = hidden(t_f)
    hid_b = jnp.concatenate([hid_f[:1], hid_f[:0:-1]], axis=0)
    hidden_w = hid_f.shape[1]
    once = pl.Buffered(1)
    return pl.pallas_call(
        functools.partial(_filtspec_kernel, length=length),
        grid=(HY_ORDER, nct),
        in_specs=[pl.BlockSpec((length, hidden_w), lambda o, c: (0, 0), pipeline_mode=once),
                  pl.BlockSpec((length, hidden_w), lambda o, c: (0, 0), pipeline_mode=once),
                  pl.BlockSpec((hidden_w, LANES), lambda o, c: (0, o * 2 * nct + c)),
                  pl.BlockSpec((hidden_w, LANES), lambda o, c: (0, o * 2 * nct + nct + c)),
                  pl.BlockSpec((1, LANES), lambda o, c: (0, c)),
                  pl.BlockSpec(m1f.shape, lambda o, c: (0, 0, 0), pipeline_mode=once),
                  pl.BlockSpec(m2.shape, lambda o, c: (0, 0), pipeline_mode=once)],
        out_specs=pl.BlockSpec((1, 1, a_n // PGROUP, 2 * bn, PGROUP * LANES), lambda o, c: (o, c, 0, 0, 0)),
        out_shape=jax.ShapeDtypeStruct((HY_ORDER, nct, a_n // PGROUP, 2 * bn, PGROUP * LANES), BF16),
        scratch_shapes=[pltpu.VMEM((a_n * pitch, LANES), F32), pltpu.VMEM((2 * a_n * pitch, LANES), F32)],
        compiler_params=pltpu.CompilerParams(
            dimension_semantics=("parallel", "parallel"), vmem_limit_bytes=VMEM_LIMIT),
        name="filtspec",
    )(hid_f, hid_b, w3, w3, deltas, m1f, m2)


def _hygate_kernel(p_ref, w_ref, *rest, first):
    x = p_ref[0]
    n = x.shape[0]
    row = lax.broadcasted_iota(jnp.int32, x.shape, 0)
    prev = jnp.where(row == 0, 0.0, pltpu.roll(x, 1, axis=0))
    nxt = jnp.where(row == n - 1, 0.0, pltpu.roll(x, n - 1, axis=0))
    c = w_ref[0:1, :] * prev + w_ref[1:2, :] * x + w_ref[2:3, :] * nxt
    if first:
        (o_ref,) = rest
        o_ref[0] = c
    else:
        y_ref, z_ref, b_ref, o_ref = rest
        o_ref[0] = c * (y_ref[0] + z_ref[0] * b_ref[...])


def _hygate(u, part, conv_w, y=None, z=None, bias=None):
    bsz, length, _ = u.shape
    nct = HY_C // LANES
    blk = lambda b, c: (b, 0, c)
    in_specs = [pl.BlockSpec((1, length, LANES), lambda b, c: (b, 0, part * nct + c)),
                pl.BlockSpec((HY_SHORT, LANES), lambda b, c: (0, part * nct + c))]
    args = [u, conv_w]
    if y is not None:
        in_specs += [pl.BlockSpec((1, length, LANES), blk), pl.BlockSpec((1, length, LANES), blk),
                     pl.BlockSpec((1, LANES), lambda b, c: (0, c))]
        args += [y, z, bias]
    return pl.pallas_call(
        functools.partial(_hygate_kernel, first=y is None),
        grid=(bsz, nct),
        in_specs=in_specs,
        out_specs=pl.BlockSpec((1, length, LANES), blk),
        out_shape=jax.ShapeDtypeStruct((bsz, length, HY_C), F32),
        compiler_params=pltpu.CompilerParams(
            dimension_semantics=("parallel", "parallel"), vmem_limit_bytes=VMEM_LIMIT),
        name="hygate",
    )(*args)


def _hyena_mix(u, conv_w, spec, bias, tables):
    z = _hygate(u, 0, conv_w)
    for o in range(HY_ORDER):
        y = _fftconv(z, spec[o], tables)
        z = _hygate(u, o + 1, conv_w, y, z, bias[o:o + 1])
    return z


SUBLANES = 8
PAD = LRU_CONV - 1
SCAN_UNROLL = 8


def _lru_kernel(x_ref, wg_ref, pv_ref, h0_ref, o_ref, end_ref, xp, a_scr, b_scr, o_scr, *, steps, chunk):
    sub = lax.broadcasted_iota(jnp.int32, (SUBLANES, LRU_BW), 0)

    def gather(j, carry):
        xp[PAD + j] = x_ref[pl.ds(j, SUBLANES, stride=steps), :]
        return carry

    lax.fori_loop(0, steps, gather, 0, unroll=8)
    for k in range(PAD):
        tail = pltpu.roll(xp[steps + k], 1, axis=0)
        xp[k] = jnp.where(sub == 0, 0.0, tail)
        head = pltpu.roll(xp[PAD + k], SUBLANES - 1, axis=0)
        xp[PAD + steps + k] = jnp.where(sub == SUBLANES - 1, 0.0, head)

    n_chunks = steps // chunk
    for d in range(2):
        ba = pv_ref[d, 0:1, :]
        bx = pv_ref[d, 1:2, :]
        lam = pv_ref[d, 2:3, :]
        neg_c_sp = -RG_C * jax.nn.softplus(-lam)
        taps = [pv_ref[d, 3 + k:4 + k, :].reshape(1, 1, LRU_BW) for k in range(LRU_CONV)]
        w_gate = wg_ref[d, 0]
        base = 0 if d == 0 else PAD

        def gates(c, carry):
            j0 = pl.multiple_of(c * chunk, chunk)
            xc = taps[0] * xp[pl.ds(j0 + base, chunk)]
            for k in range(1, LRU_CONV):
                xc = xc + taps[k] * xp[pl.ds(j0 + base + k, chunk)]
            xc2 = xc.reshape(chunk * SUBLANES, LRU_BW)
            g = jnp.dot(xc2.astype(BF16), w_gate, preferred_element_type=F32)
            r = 0.5 * jnp.tanh(0.5 * (g[:, :LRU_BW] + ba)) + 0.5
            i = 0.5 * jnp.tanh(0.5 * (g[:, LRU_BW:] + bx)) + 0.5
            log_a = neg_c_sp * r
            a = jnp.exp(log_a)
            t = jnp.tanh(log_a)
            gap = -2.0 * t / (1.0 - t)
            root = jnp.where(gap > 0.0, gap * lax.rsqrt(gap), 0.0)
            b = root * (i * xc2)
            a_scr[pl.ds(j0, chunk)] = a.reshape(chunk, SUBLANES, LRU_BW)
            b_scr[pl.ds(j0, chunk)] = b.reshape(chunk, SUBLANES, LRU_BW)
            return carry

        lax.fori_loop(0, n_chunks, gates, 0)

        def steps8(k, carry):
            h, p = carry
            base = pl.multiple_of(k * SCAN_UNROLL if d == 0 else steps - SCAN_UNROLL * (k + 1), SCAN_UNROLL)
            for u in (range(SCAN_UNROLL) if d == 0 else range(SCAN_UNROLL - 1, -1, -1)):
                a = a_scr[base + u]
                h = a * h + b_scr[base + u]
                p = a * p
                b_scr[base + u] = h
                a_scr[base + u] = p
            return h, p

        h_loc, p_loc = lax.fori_loop(
            0, steps // SCAN_UNROLL, steps8,
            (jnp.zeros((SUBLANES, LRU_BW), F32), jnp.ones((SUBLANES, LRU_BW), F32)))

        carry = h0_ref[0, d:d + 1, :]
        cin = jnp.zeros((SUBLANES, LRU_BW), F32)
        order = range(SUBLANES) if d == 0 else range(SUBLANES - 1, -1, -1)
        for s in order:
            cin = jnp.where(sub == s, jnp.broadcast_to(carry, (SUBLANES, LRU_BW)), cin)
            carry = p_loc[s:s + 1, :] * carry + h_loc[s:s + 1, :]
        end_ref[0, d:d + 1, :] = carry

        def fix(c, carry_):
            j0 = pl.multiple_of(c * chunk, chunk)
            h = b_scr[pl.ds(j0, chunk)] + a_scr[pl.ds(j0, chunk)] * cin[None]
            if d == 0:
                o_scr[pl.ds(j0, chunk)] = h
            else:
                h = h + o_scr[pl.ds(j0, chunk)]
                for jj in range(chunk):
                    o_ref[pl.ds(j0 + jj, SUBLANES, stride=steps), :] = h[jj]
            return carry_

        lax.fori_loop(0, n_chunks, fix, 0)


def _rglru(u, length, lconv, lwa, lba, lwx, lbx, llam, h0):
    m = u.shape[0]
    bsz = m // length
    steps = length // SUBLANES
    chunk = min(32, steps)
    wg = jnp.concatenate([lwa, lwx], axis=-1).astype(BF16)
    pv = jnp.concatenate([lba[:, None], lbx[:, None], llam[:, None], lconv,
                          jnp.zeros((2, 1, LRU_C), F32)], axis=1)
    seg = pltpu.VMEM((steps, SUBLANES, LRU_BW), F32)
    return pl.pallas_call(
        functools.partial(_lru_kernel, steps=steps, chunk=chunk),
        grid=(bsz, LRU_BLOCKS),
        in_specs=[pl.BlockSpec((length, LRU_BW), lambda b, n: (b, OFF_LRU_X // LRU_BW + n)),
                  pl.BlockSpec((2, 1, LRU_BW, 2 * LRU_BW), lambda b, n: (0, n, 0, 0)),
                  pl.BlockSpec((2, SUBLANES, LRU_BW), lambda b, n: (0, 0, n)),
                  pl.BlockSpec((1, 2, LRU_BW), lambda b, n: (b, 0, n))],
        out_specs=[pl.BlockSpec((length, LRU_BW), lambda b, n: (b, n)),
                   pl.BlockSpec((1, 2, LRU_BW), lambda b, n: (b, 0, n))],
        out_shape=[jax.ShapeDtypeStruct((m, LRU_C), F32),
                   jax.ShapeDtypeStruct((bsz, 2, LRU_C), F32)],
        scratch_shapes=[pltpu.VMEM((steps + 2 * PAD, SUBLANES, LRU_BW), F32), seg, seg, seg],
        compiler_params=pltpu.CompilerParams(
            dimension_semantics=("parallel", "parallel"), vmem_limit_bytes=VMEM_LIMIT),
        name="rglru",
    )(u, wg, pv, h0)


def _rope_tables(length):
    rows = length // GRID_W
    row = jnp.repeat(jnp.arange(rows, dtype=F32), GRID_W)
    col = jnp.tile(jnp.arange(GRID_W, dtype=F32), rows)
    n_freq = ROPE_D // 4
    inv = ROPE_BASE ** (-jnp.arange(n_freq, dtype=F32) / n_freq)
    ang = jnp.concatenate([row[:, None] * inv, col[:, None] * inv], axis=-1)
    return jnp.cos(ang), jnp.sin(ang)


def _rope_lane_tables(rope):
    cos, sin = rope
    z = jnp.zeros_like(cos)
    return (jnp.concatenate([cos, cos, z, z], axis=1), jnp.concatenate([-sin, z, z, z], axis=1),
            jnp.concatenate([z, sin, z, z], axis=1))


def _pad_heads_q(wuq):
    w = wuq.reshape(Q_RANK, MLA_H, NOPE_D + ROPE_D)
    w = jnp.pad(w, ((0, 0), (0, 0), (0, HEAD_QK - NOPE_D - ROPE_D)))
    return w.reshape(Q_RANK, MLA_H * HEAD_QK)


def _rms(xv, g):
    return xv * lax.rsqrt(jnp.mean(xv * xv, axis=-1, keepdims=True) + EPS) * g


ROW_SPLIT = 2


def _proj_in_kernel(x_ref, g_ref, mod_ref, w_ref, o_ref):
    sub = x_ref.shape[0] // ROW_SPLIT
    for part in range(ROW_SPLIT):
        rows = slice(part * sub, (part + 1) * sub)
        y = _rms(x_ref[rows, :], g_ref[...])
        h = (y * (1.0 + mod_ref[0, 1:2, :]) + mod_ref[0, 0:1, :]).astype(BF16)
        o_ref[rows, :] = jnp.dot(h, w_ref[...], preferred_element_type=F32)


def _proj_in(x, g, shift, scale, w, layer):
    bsz, length, d = x.shape
    m, n = bsz * length, w.shape[2]
    tm = _pick(length, (512, 256, 128))
    bps = length // tm
    mod = jnp.stack([shift, scale], axis=1)
    mod_idx = (lambda i: (i // bps, 0, 0)) if shift.shape[0] > 1 else (lambda i: (0, 0, 0))
    return pl.pallas_call(
        _proj_in_kernel,
        grid=(m // tm,),
        in_specs=[pl.BlockSpec((tm, d), lambda i: (i, 0)),
                  pl.BlockSpec((1, d), lambda i: (0, 0)),
                  pl.BlockSpec((1, 2, d), mod_idx),
                  pl.BlockSpec((None, d, n), lambda i: (layer, 0, 0), pipeline_mode=pl.Buffered(1))],
        out_specs=pl.BlockSpec((tm, n), lambda i: (i, 0)),
        out_shape=jax.ShapeDtypeStruct((m, n), F32),
        compiler_params=pltpu.CompilerParams(dimension_semantics=("parallel",), vmem_limit_bytes=VMEM_LIMIT),
        name="proj_in",
    )(x.reshape(m, d), g[None], mod, w)


def _proj_qkv_kernel(*refs, rope, with_q):
    refs = list(refs)
    uq_ref = refs.pop(0) if with_q else None
    ukv_ref, ukr_ref = refs.pop(0), refs.pop(0)
    gq_ref, wq_ref = (refs.pop(0), refs.pop(0)) if with_q else (None, None)
    gkv_ref, wkv_ref = refs.pop(0), refs.pop(0)
    cos_ref, sa_ref, sb_ref = (refs.pop(0), refs.pop(0), refs.pop(0)) if rope else (None, None, None)
    q_ref = refs.pop(0) if with_q else None
    kv_ref, kr_ref = refs

    def rot(blk):
        if not rope:
            return blk
        return (blk * cos_ref[...] + pltpu.roll(blk, LANES - 32, axis=1) * sa_ref[...]
                + pltpu.roll(blk, 32, axis=1) * sb_ref[...])

    if with_q:
        nq = _rms(uq_ref[...], gq_ref[...]).astype(BF16)
        q = jnp.dot(nq, wq_ref[...], preferred_element_type=F32)
        for hd in range(MLA_H):
            lo = hd * HEAD_QK
            q_ref[:, lo:lo + NOPE_D] = (q[:, lo:lo + NOPE_D] * Q_SCALE).astype(BF16)
            q_ref[:, lo + NOPE_D:lo + HEAD_QK] = (rot(q[:, lo + NOPE_D:lo + HEAD_QK]) * Q_SCALE).astype(BF16)
    nkv = _rms(ukv_ref[...], gkv_ref[...]).astype(BF16)
    kv_ref[...] = jnp.dot(nkv, wkv_ref[...], preferred_element_type=F32).astype(BF16)
    kr_ref[...] = rot(ukr_ref[...]).astype(BF16)


def _proj_qkv(u, length, gq, wuq_pad, gkv, wukv, layer, rope_tabs, with_q=True):
    m = u.shape[0]
    tm = _pick(length, (512, 256, 128))
    bps = length // tm
    row = lambda i: (i, 0)
    const = lambda i: (0, 0)
    wsel = lambda i: (layer, 0, 0)
    in_specs, args = [], []
    if with_q:
        in_specs.append(pl.BlockSpec((tm, Q_RANK), lambda i: (i, OFF_MLA_Q // Q_RANK)))
        args.append(u)
    in_specs += [pl.BlockSpec((tm, KV_RANK), lambda i: (i, OFF_MLA_KV // KV_RANK)),
                 pl.BlockSpec((tm, LANES), lambda i: (i, OFF_MLA_KR // LANES))]
    args += [u, u]
    if with_q:
        in_specs += [pl.BlockSpec((1, Q_RANK), const), pl.BlockSpec((None,) + wuq_pad.shape[1:], wsel)]
        args += [gq[None], wuq_pad]
    in_specs += [pl.BlockSpec((1, KV_RANK), const), pl.BlockSpec((None,) + wukv.shape[1:], wsel)]
    args += [gkv[None], wukv]
    if rope_tabs is not None:
        in_specs += [pl.BlockSpec((tm, LANES), lambda i: (i % bps, 0))] * 3
        args += list(rope_tabs)
    out_specs = [pl.BlockSpec((tm, MLA_H * HEAD_QK), row), pl.BlockSpec((tm, LANES), row)]
    out_shape = [jax.ShapeDtypeStruct((m, MLA_H * HEAD_QK), BF16), jax.ShapeDtypeStruct((m, LANES), BF16)]
    if with_q:
        out_specs.insert(0, pl.BlockSpec((tm, MLA_H * HEAD_QK), row))
        out_shape.insert(0, jax.ShapeDtypeStruct((m, MLA_H * HEAD_QK), BF16))
    outs = pl.pallas_call(
        functools.partial(_proj_qkv_kernel, rope=rope_tabs is not None, with_q=with_q),
        grid=(m // tm,),
        in_specs=in_specs, out_specs=out_specs, out_shape=out_shape,
        compiler_params=pltpu.CompilerParams(dimension_semantics=("parallel",), vmem_limit_bytes=VMEM_LIMIT),
        name="proj_qkv",
    )(*args)
    return outs if with_q else [None] + list(outs)


def _mix_kernel(yhy_ref, hs_ref, gate_ref, ymla_ref, hg_ref, w_ref, x_ref, ng_ref, mod_ref, o_ref):
    sub = x_ref.shape[0] // ROW_SPLIT
    parts = [slice(part * sub, (part + 1) * sub) for part in range(ROW_SPLIT)]
    normed = []
    for rows in parts:
        y = jnp.concatenate([yhy_ref[rows, :], jax.nn.gelu(gate_ref[rows, :], approximate=True) * hs_ref[rows, :],
                             ymla_ref[rows, :]], axis=1)
        heads = [_rms(y[:, hd * HEAD_W:(hd + 1) * HEAD_W], hg_ref[:, hd * HEAD_W:(hd + 1) * HEAD_W]).astype(BF16)
                 for hd in range(N_MIX_HEADS)]
        normed.append(jnp.concatenate(heads, axis=1))
    accs = [jnp.dot(yn, w_ref[...], preferred_element_type=F32) for yn in normed]
    for rows, acc in zip(parts, accs):
        o_ref[rows, :] = x_ref[rows, :] + mod_ref[0] * _rms(acc, ng_ref[...])


def _mix_out(x, y_hy, h_sum, u, y_mla, head_g, w_out, layer, ng_post, gate):
    bsz, length, d = x.shape
    m = bsz * length
    tm = _pick(length, (512, 256, 128))
    bps = length // tm
    row = lambda i: (i, 0)
    const = lambda i: (0, 0)
    mod_idx = (lambda i: (i // bps, 0, 0)) if gate.shape[0] > 1 else (lambda i: (0, 0, 0))
    out = pl.pallas_call(
        _mix_kernel,
        grid=(m // tm,),
        in_specs=[pl.BlockSpec((tm, HY_C), row), pl.BlockSpec((tm, LRU_C), row),
                  pl.BlockSpec((tm, LRU_C), lambda i: (i, OFF_LRU_G // LRU_C)),
                  pl.BlockSpec((tm, MLA_H * V_D), row),
                  pl.BlockSpec((1, D_MIX), const),
                  pl.BlockSpec((None,) + w_out.shape[1:], lambda i: (layer, 0, 0), pipeline_mode=pl.Buffered(1)),
                  pl.BlockSpec((tm, d), row), pl.BlockSpec((1, d), const), pl.BlockSpec((1, 1, d), mod_idx)],
        out_specs=pl.BlockSpec((tm, d), row),
        out_shape=jax.ShapeDtypeStruct((m, d), F32),
        compiler_params=pltpu.CompilerParams(dimension_semantics=("parallel",), vmem_limit_bytes=VMEM_LIMIT),
        name="mix_out",
    )(y_hy.reshape(m, HY_C), h_sum.reshape(m, LRU_C), u, y_mla.reshape(m, MLA_H * V_D), head_g[None], w_out,
      x.reshape(m, d), ng_post[None], gate[:, None])
    return out.reshape(bsz, length, d)


HALO = 16


def _ffn_kernel(x_ref, xprev_ref, xnext_ref, ng_ref, mod_ref, wg_ref, wv_ref, cg_ref, cv_ref, wd_ref,
                o_ref, hbuf, *, blocks_per_seq):
    i = pl.program_id(0)
    j = pl.program_id(1)
    tm = x_ref.shape[0]

    def norm_mod(xv):
        y = xv * lax.rsqrt(jnp.mean(xv * xv, axis=-1, keepdims=True) + EPS) * ng_ref[0:1, :]
        return (y * (1.0 + mod_ref[0, 1:2, :]) + mod_ref[0, 0:1, :]).astype(BF16)

    @pl.when(j == 0)
    def _():
        first = (i % blocks_per_seq) == 0
        last = (i % blocks_per_seq) == blocks_per_seq - 1
        hp = norm_mod(xprev_ref[...])
        hn = norm_mod(xnext_ref[...])
        hbuf[0:HALO] = jnp.where(first, jnp.zeros_like(hp), hp)
        hbuf[HALO:HALO + tm] = norm_mod(x_ref[...])
        hbuf[HALO + tm:] = jnp.where(last, jnp.zeros_like(hn), hn)
        o_ref[...] = jnp.zeros_like(o_ref)

    hb = hbuf[...]

    def conv(up, c_ref):
        prev = pltpu.roll(up, 1, axis=0)[HALO:HALO + tm]
        nxt = pltpu.roll(up, tm + 2 * HALO - 1, axis=0)[HALO:HALO + tm]
        return c_ref[0:1, :] * prev + c_ref[1:2, :] * up[HALO:HALO + tm] + c_ref[2:3, :] * nxt

    gate = conv(jnp.dot(hb, wg_ref[...], preferred_element_type=F32), cg_ref)
    val = conv(jnp.dot(hb, wv_ref[...], preferred_element_type=F32), cv_ref)
    act = (jax.nn.gelu(gate, approximate=True) * val).astype(BF16)
    o_ref[...] += jnp.dot(act, wd_ref[...], preferred_element_type=F32)

    @pl.when(j == pl.num_programs(1) - 1)
    def _():
        y = o_ref[...]
        y = y * lax.rsqrt(jnp.mean(y * y, axis=-1, keepdims=True) + EPS) * ng_ref[1:2, :]
        o_ref[...] = x_ref[...] + mod_ref[0, 2:3, :] * y


def _ffn_sublayer(x, ng_pre, ng_post, shift, scale, gate, w_up, w_conv, w_down, layer):
    bsz, length, d = x.shape
    m = bsz * length
    tm = _pick(length, (1024, 512, 256, 128))
    tf = _pick(D_FF, (512, 256, 128))
    nf = D_FF // tf
    bps = length // tm
    hb = tm // HALO
    nrow = m // HALO
    per_batch = shift.shape[0] > 1
    ng = jnp.stack([ng_pre, ng_post])
    mod = jnp.stack([shift, scale, gate], axis=1)
    out = pl.pallas_call(
        functools.partial(_ffn_kernel, blocks_per_seq=bps),
        grid=(m // tm, nf),
        in_specs=[pl.BlockSpec((tm, d), lambda i, j: (i, 0), pipeline_mode=pl.Buffered(1)),
                  pl.BlockSpec((HALO, d), lambda i, j: (jnp.maximum(i * hb - 1, 0), 0)),
                  pl.BlockSpec((HALO, d), lambda i, j: (jnp.minimum((i + 1) * hb, nrow - 1), 0)),
                  pl.BlockSpec((2, d), lambda i, j: (0, 0)),
                  pl.BlockSpec((1, 3, d), (lambda i, j: (i // bps, 0, 0)) if per_batch else (lambda i, j: (0, 0, 0))),
                  pl.BlockSpec((None, d, tf), lambda i, j: (layer, 0, j)),
                  pl.BlockSpec((None, d, tf), lambda i, j: (layer, 0, j + nf)),
                  pl.BlockSpec((None, FFN_CONV, tf), lambda i, j: (layer, 0, j)),
                  pl.BlockSpec((None, FFN_CONV, tf), lambda i, j: (layer, 0, j + nf)),
                  pl.BlockSpec((None, tf, d), lambda i, j: (layer, j, 0))],
        out_specs=pl.BlockSpec((tm, d), lambda i, j: (i, 0)),
        out_shape=jax.ShapeDtypeStruct((m, d), F32),
        scratch_shapes=[pltpu.VMEM((tm + 2 * HALO, d), BF16)],
        compiler_params=pltpu.CompilerParams(
            dimension_semantics=("parallel", "arbitrary"), vmem_limit_bytes=FFN_VMEM_LIMIT),
        name="ffn",
    )(x.reshape(m, d), x.reshape(m, d), x.reshape(m, d), ng, mod, w_up, w_up, w_conv, w_conv, w_down)
    return out.reshape(bsz, length, d)


def kernel(x, c, ctx, c_ctx, ada_w, ada_b, norm_g, w_in, hy_conv, hy_w1, hy_b1, hy_w2, hy_b2, hy_w3, hy_bias, lru_conv, lru_wa, lru_ba, lru_wx, lru_bx, lru_lam, mla_gq, mla_gkv, mla_wuq, mla_wukv, head_g, w_out, ffn_up, ffn_conv, ffn_down):
    depth = ada_w.shape[0]
    bsz, seq, _ = x.shape
    ctx_len = ctx.shape[1]
    rope_tabs = _rope_lane_tables(_rope_tables(seq))
    dft_lat, dft_ctx = _dft_tables(seq), _dft_tables(ctx_len)
    xc = ctx
    w_in_b = jnp.pad(w_in, ((0, 0), (0, 0), (0, NOPE_D - ROPE_D))).astype(BF16)
    wuq_pad = jax.vmap(_pad_heads_q)(mla_wuq).astype(BF16)
    wukv_b, w_out_b = mla_wukv.astype(BF16), w_out.astype(BF16)
    up_b, down_b = ffn_up.astype(BF16), ffn_down.astype(BF16)
    mods = _adaln(jnp.concatenate([c, c_ctx[None]], axis=0), ada_w, ada_b)

    def seq3(t, length):
        return None if t is None else t.reshape(bsz, length, t.shape[-1])

    for l in range(depth):
        need_ctx = l < depth - 1
        ng = norm_g[l]
        sh1, sc1, g1, sh2, sc2, g2 = jnp.split(mods[l, :bsz], N_MOD, axis=-1)
        csh1, csc1, cg1, csh2, csc2, cg2 = jnp.split(mods[l, bsz:], N_MOD, axis=-1)
        hy_f = (hy_w1[l], hy_b1[l], hy_w2[l], hy_b2[l], hy_w3[l])
        lru_p = (lru_conv[l], lru_wa[l], lru_ba[l], lru_wx[l], lru_bx[l], lru_lam[l])

        u = _proj_in(x, ng[0], sh1, sc1, w_in_b, l)
        uc = _proj_in(xc, ng[0], csh1, csc1, w_in_b, l)
        u3, uc3 = seq3(u, seq), seq3(uc, ctx_len)

        hc_sum, hc_end = _rglru(uc, ctx_len, *lru_p, jnp.zeros((bsz, 2, LRU_C), F32))
        q_c, kv_c, kr_c = [seq3(t, ctx_len) for t in
                           _proj_qkv(uc, ctx_len, mla_gq[l], wuq_pad, mla_gkv[l], wukv_b, l, None, with_q=need_ctx)]

        y_hy = _hyena_mix(u3, hy_conv[l], _filter_spectra(seq, *hy_f, dft_lat), hy_bias[l], dft_lat)
        h_sum, _ = _rglru(u, seq, *lru_p, hc_end)
        q, kv, kr = [seq3(t, seq) for t in
                     _proj_qkv(u, seq, mla_gq[l], wuq_pad, mla_gkv[l], wukv_b, l, rope_tabs)]
        y_mla = _attention(q, [(kv_c, kr_c), (kv, kr)])
        x = _mix_out(x, y_hy, h_sum, u, y_mla, head_g[l], w_out_b, l, ng[1], g1)

        if need_ctx:
            yc_hy = _hyena_mix(uc3, hy_conv[l], _filter_spectra(ctx_len, *hy_f, dft_ctx), hy_bias[l], dft_ctx)
            yc_mla = _attention(q_c, [(kv_c, kr_c)])
            xc = _mix_out(xc, yc_hy, hc_sum, uc, yc_mla, head_g[l], w_out_b, l, ng[1], cg1)

        x = _ffn_sublayer(x, ng[2], ng[3], sh2, sc2, g2, up_b, ffn_conv, down_b, l)
        if need_ctx:
            xc = _ffn_sublayer(xc, ng[2], ng[3], csh2, csc2, cg2, up_b, ffn_conv, down_b, l)
    return x
```
